```python
import math, functools
import jax, jax.numpy as jnp
from jax import lax
import numpy as np

D_MODEL = 2048
BATCH = 4
SEQ = 2048
DEPTH = 1
DEC_BATCH = 32
DEC_SEQ = 4
PAST_LEN = 16384
PAGE_SIZE = 128

DELTA_HEADS = 8
DELTA_DK = 128
DELTA_DV = 128
CONV_WIDTH = 4
DELTA_CHUNK = 64
DIFF_HEADS = 8
DIFF_COMP = 64
DIFF_DV = 2 * DIFF_COMP
ROPE_THETA = 10000.0
Q_BLOCK = 128
N_GROUPS = 4
EXPERTS_PER_GROUP = 8
N_EXPERTS = N_GROUPS * EXPERTS_PER_GROUP
TOP_K = 2
EXPERT_FF = 512
MOE_BLOCK = 128
PLE_DIM = 256
EPS = 1e-6

CONV_DIM = 2 * DELTA_HEADS * DELTA_DK + DELTA_HEADS * DELTA_DV
Z_DIM = DELTA_HEADS * DELTA_DV
DIFF_QK_DIM = DIFF_HEADS * 2 * DIFF_COMP
DIFF_V_DIM = DIFF_HEADS * DIFF_DV
IN_SIZES = (CONV_DIM, Z_DIM, DELTA_HEADS, DELTA_HEADS, DIFF_QK_DIM, DIFF_QK_DIM, DIFF_V_DIM, D_MODEL, D_MODEL)
IN_COLS = sum(IN_SIZES)

kernel_name = 'hybrid_deltanet_diffattn_hmoe_step'


def _rmsnorm(x, g):
    xf = x.astype(jnp.float32)
    y = xf * lax.rsqrt(jnp.mean(xf * xf, axis=-1, keepdims=True) + EPS)
    return (y * g.astype(jnp.float32)).astype(x.dtype)


def _l2norm(x):
    xf = x.astype(jnp.float32)
    return xf * lax.rsqrt(jnp.sum(xf * xf, axis=-1, keepdims=True) + EPS)


def _rope(x, pos):
    half = DIFF_COMP // 2
    inv_freq = ROPE_THETA ** (-jnp.arange(half, dtype=jnp.float32) / half)
    ang = pos.astype(jnp.float32)[:, None] * inv_freq[None, :]
    cos = jnp.cos(ang)[:, None, None, :]
    sin = jnp.sin(ang)[:, None, None, :]
    x1, x2 = x[..., :half], x[..., half:]
    return jnp.concatenate([x1 * cos - x2 * sin, x2 * cos + x1 * sin], axis=-1)


def _short_conv(u, buf, w):
    L = u.shape[1]
    xc = jnp.concatenate([buf.astype(u.dtype), u], axis=1)
    out = sum(xc[:, i:i + L] * w[i] for i in range(CONV_WIDTH))
    return jax.nn.silu(out), xc[:, -(CONV_WIDTH - 1):]


def _gated_delta_chunked(q, k, v, g, beta, s0):
    B, L, H, DK = q.shape
    DV = v.shape[-1]
    C = math.gcd(L, DELTA_CHUNK)
    N = L // C

    def chunks(t):
        t = t.reshape((B, N, C, H) + t.shape[3:])
        return jnp.moveaxis(t, (1, 3), (0, 2))

    qc, kc, vc, gr, bc = map(chunks, (q, k, v, g, beta))
    gc = jnp.cumsum(gr, axis=-1)
    idx = jnp.arange(C)
    incl = idx[:, None] >= idx[None, :]
    strict = idx[:, None] > idx[None, :]
    diff = gc[..., :, None] - gc[..., None, :]
    decay = jnp.where(incl, jnp.exp(jnp.where(incl, diff, 0.0)), 0.0)
    kb = kc * bc[..., None]
    a = jnp.where(strict, jnp.einsum('nbhik,nbhjk->nbhij', kb, kc) * decay, 0.0)
    eye = jnp.eye(C, dtype=jnp.float32)
    t = lax.linalg.triangular_solve(eye + a, jnp.broadcast_to(eye, a.shape),
                                    left_side=True, lower=True, unit_diagonal=True)
    u = jnp.einsum('nbhij,nbhjv->nbhiv', t, vc * bc[..., None])
    w = jnp.einsum('nbhij,nbhjk->nbhik', t, kb * jnp.exp(gc)[..., None])
    qk = jnp.einsum('nbhik,nbhjk->nbhij', qc, kc) * decay

    def step(s, inp):
        qi, ki, ui, wi, gi, qki = inp
        v_new = ui - jnp.einsum('bhck,bhkv->bhcv', wi, s)
        o = (jnp.einsum('bhck,bhkv->bhcv', qi * jnp.exp(gi)[..., None], s)
             + jnp.einsum('bhij,bhjv->bhiv', qki, v_new))
        g_last = gi[..., -1:]
        s = (s * jnp.exp(g_last)[..., None]
             + jnp.einsum('bhck,bhcv->bhkv', ki * jnp.exp(g_last - gi)[..., None], v_new))
        return s, o

    s_final, o = lax.scan(step, s0, (qc, kc, u, w, gc, qk))
    o = jnp.moveaxis(o, (0, 2), (1, 3)).reshape(B, L, H, DV)
    return o, s_final


def _delta_branch(qkv_raw, z, b_raw, a_raw, conv_buf, s0, conv_w, a_log, dt_bias, delta_norm):
    B, L, _ = qkv_raw.shape
    qkv, conv_new = _short_conv(qkv_raw, conv_buf, conv_w)
    nqk = DELTA_HEADS * DELTA_DK
    q = _l2norm(qkv[..., :nqk].reshape(B, L, DELTA_HEADS, DELTA_DK)) * (DELTA_DK ** -0.5)
    k = _l2norm(qkv[..., nqk:2 * nqk].reshape(B, L, DELTA_HEADS, DELTA_DK))
    v = qkv[..., 2 * nqk:].reshape(B, L, DELTA_HEADS, DELTA_DV).astype(jnp.float32)
    beta = jax.nn.sigmoid(b_raw.astype(jnp.float32))
    g = -jnp.exp(a_log.astype(jnp.float32)) * jax.nn.softplus(
        a_raw.astype(jnp.float32) + dt_bias.astype(jnp.float32))
    o, s_new = _gated_delta_chunked(q, k, v, g, beta, s0.astype(jnp.float32))
    o = _rmsnorm(o, delta_norm) * jax.nn.silu(z.reshape(B, L, DELTA_HEADS, DELTA_DV).astype(jnp.float32))
    return o.reshape(B, L, Z_DIM), conv_new, s_new


def _diff_qkv(q_raw, k_raw, v_raw, pos, q_norm, k_norm):
    B, L, _ = q_raw.shape
    q = _rope(_rmsnorm(q_raw.reshape(B, L, DIFF_HEADS, 2, DIFF_COMP), q_norm).astype(jnp.float32), pos)
    k = _rope(_rmsnorm(k_raw.reshape(B, L, DIFF_HEADS, 2, DIFF_COMP), k_norm).astype(jnp.float32), pos)
    v = v_raw.reshape(B, L, DIFF_HEADS, DIFF_DV).astype(jnp.float32)
    return q, k, v


def _diff_attn_prompt(q, k, v, lam):
    B, L = q.shape[:2]
    scale = DIFF_COMP ** -0.5
    kpos = jnp.arange(L)

    def block(i):
        qb = lax.dynamic_slice_in_dim(q, i * Q_BLOCK, Q_BLOCK, axis=1)
        s = jnp.einsum('bqhcd,bkhcd->bhcqk', qb, k) * scale
        qpos = i * Q_BLOCK + jnp.arange(Q_BLOCK)
        s = jnp.where(qpos[:, None] >= kpos[None, :], s, -jnp.inf)
        p = jax.nn.softmax(s, axis=-1)
        pd = p[:, :, 0] - lam * p[:, :, 1]
        return jnp.einsum('bhqk,bkhd->bqhd', pd, v)

    o = lax.map(block, jnp.arange(L // Q_BLOCK))
    return jnp.moveaxis(o, 0, 1).reshape(B, L, DIFF_HEADS, DIFF_DV)


def _online_update(carry, s, vv):
    m, l, acc = carry
    m_new = jnp.maximum(m, jnp.max(s, axis=-1))
    p = jnp.exp(s - m_new[..., None])
    corr = jnp.exp(m - m_new)
    return (m_new, l * corr + jnp.sum(p, axis=-1),
            acc * corr[..., None] + jnp.einsum('bhcqk,bkhd->bhcqd', p, vv))


def _diff_attn_sample(q, k, v, lam, cache_k, cache_v, page_table):
    B, L = q.shape[:2]
    scale = DIFF_COMP ** -0.5

    def page_step(carry, phys):
        kp = cache_k[phys].astype(jnp.float32)
        vp = cache_v[phys].astype(jnp.float32)
        s = jnp.einsum('bqhcd,bkhcd->bhcqk', q, kp) * scale
        return _online_update(carry, s, vp), None

    init = (jnp.full((B, DIFF_HEADS, 2, L), -jnp.inf, jnp.float32),
            jnp.zeros((B, DIFF_HEADS, 2, L), jnp.float32),
            jnp.zeros((B, DIFF_HEADS, 2, L, DIFF_DV), jnp.float32))
    carry, _ = lax.scan(page_step, init, page_table.T)
    s = jnp.einsum('bqhcd,bkhcd->bhcqk', q, k) * scale
    idx = jnp.arange(L)
    s = jnp.where(idx[:, None] >= idx[None, :], s, -jnp.inf)
    m, l, acc = _online_update(carry, s, v)
    o = acc / l[..., None]
    o = o[:, :, 0] - lam * o[:, :, 1]
    return jnp.transpose(o, (0, 2, 1, 3))


def _grouped_experts(xf, eidx, wts, w_gate, w_up, w_down):
    T, D = xf.shape
    S = T * TOP_K
    flat_e = eidx.reshape(-1)
    order = jnp.argsort(flat_e)
    sorted_e = flat_e[order]
    counts = jnp.zeros((N_EXPERTS,), jnp.int32).at[flat_e].add(1)
    padded = (counts + MOE_BLOCK - 1) // MOE_BLOCK * MOE_BLOCK
    ends = jnp.cumsum(padded)
    pad_start = ends - padded
    start = jnp.cumsum(counts) - counts
    dest = pad_start[sorted_e] + jnp.arange(S, dtype=jnp.int32) - start[sorted_e]
    n_blocks = -(-(S + N_EXPERTS * (MOE_BLOCK - 1)) // MOE_BLOCK)
    rows = n_blocks * MOE_BLOCK
    src_tok = jnp.full((rows,), T, jnp.int32).at[dest].set((order // TOP_K).astype(jnp.int32))
    x_pad = jnp.concatenate([xf, jnp.zeros((1, D), xf.dtype)], axis=0)
    xb = x_pad[src_tok].reshape(n_blocks, MOE_BLOCK, D)
    block_e = jnp.minimum(jnp.searchsorted(ends, jnp.arange(n_blocks) * MOE_BLOCK, side='right'), N_EXPERTS - 1)

    def expert_block(inp):
        xblk, e = inp
        hdn = jax.nn.silu(xblk @ w_gate[e]) * (xblk @ w_up[e])
        return hdn @ w_down[e]

    yb = lax.map(expert_block, (xb, block_e)).reshape(rows, D)
    slot_row = jnp.zeros((S,), jnp.int32).at[order].set(dest)
    y_slots = yb[slot_row].reshape(T, TOP_K, D)
    return jnp.einsum('tk,tkd->td', wts.astype(y_slots.dtype), y_slots)


def _hier_moe(h, w_rg, b_rg, w_re, b_re, w_gate, w_up, w_down):
    B, L, D = h.shape
    hf = h.reshape(B * L, D)
    pg = jax.nn.softmax((hf @ w_rg + b_rg).astype(jnp.float32), axis=-1)
    g_top = jnp.argmax(pg, axis=-1)
    g_w = jnp.max(pg, axis=-1, keepdims=True)
    le = (hf @ w_re + b_re).astype(jnp.float32).reshape(-1, N_GROUPS, EXPERTS_PER_GROUP)
    le_sel = jnp.sum(le * jax.nn.one_hot(g_top, N_GROUPS, dtype=jnp.float32)[:, :, None], axis=1)
    pe = jax.nn.softmax(le_sel, axis=-1)
    top_v, top_i = lax.top_k(pe, TOP_K)
    wts = g_w * top_v / jnp.sum(top_v, axis=-1, keepdims=True)
    eidx = (g_top[:, None] * EXPERTS_PER_GROUP + top_i).astype(jnp.int32)
    return _grouped_experts(hf, eidx, wts, w_gate, w_up, w_down).reshape(B, L, D)


def _layer(x, p, pos, conv_buf, s0, attend, lam_init, lw):
    h = _rmsnorm(x, lw['norm_attn'])
    proj = h @ lw['w_in']
    splits = [int(c) for c in np.cumsum(IN_SIZES)[:-1]]
    qkv_raw, z, b_raw, a_raw, q_raw, k_raw, v_raw, gate_a, gate_b = jnp.split(proj, splits, axis=-1)
    o_a, conv_new, s_new = _delta_branch(qkv_raw, z, b_raw, a_raw, conv_buf, s0, lw['conv_w'],
                                         lw['a_log'], lw['dt_bias'], lw['delta_norm'])
    q, k, v = _diff_qkv(q_raw, k_raw, v_raw, pos, lw['q_norm'], lw['k_norm'])
    lam = (jnp.exp(jnp.sum(lw['lam_q1'].astype(jnp.float32) * lw['lam_k1'].astype(jnp.float32)))
           - jnp.exp(jnp.sum(lw['lam_q2'].astype(jnp.float32) * lw['lam_k2'].astype(jnp.float32)))
           + lam_init)
    o_b = attend(q, k, v, lam)
    o_b = (_rmsnorm(o_b, lw['subln']) * (1.0 - lam_init)).reshape(x.shape[0], x.shape[1], DIFF_V_DIM)
    merged = (jax.nn.sigmoid(gate_a) * (o_a @ lw['w_oa'])
              + jax.nn.sigmoid(gate_b) * (o_b @ lw['w_ob']))
    x = x + (merged @ lw['w_out']).astype(x.dtype)
    x = x + _hier_moe(_rmsnorm(x, lw['norm_ffn']), lw['w_rg'], lw['b_rg'], lw['w_re'], lw['b_re'],
                      lw['exp_gate'], lw['exp_up'], lw['exp_down']).astype(x.dtype)
    x = x + ((p @ lw['w_ple']) * jax.nn.sigmoid(_rmsnorm(x, lw['norm_ple']) @ lw['w_ple_gate'])).astype(x.dtype)
    return x, k, v, conv_new, s_new


def setup_inputs(seed: int = 0) -> dict:
    key = jax.random.key(seed)
    keys = iter(jax.random.split(key, 48))

    def nrm(shape, scale):
        return jax.random.normal(next(keys), shape, jnp.float32) * scale

    def gain(shape):
        return 1.0 + nrm(shape, 0.05)

    n_pages = PAST_LEN // PAGE_SIZE
    used = DEC_BATCH * n_pages
    n_phys = used + (used + 3) // 4
    return {
        'x_prompt': nrm((BATCH, SEQ, D_MODEL), 1.0),
        'x_sample': nrm((DEC_BATCH, DEC_SEQ, D_MODEL), 1.0),
        'cache_k': nrm((DEPTH, n_phys, PAGE_SIZE, DIFF_HEADS, 2, DIFF_COMP), 1.0),
        'cache_v': nrm((DEPTH, n_phys, PAGE_SIZE, DIFF_HEADS, DIFF_DV), 1.0),
        'state_conv': nrm((DEPTH, DEC_BATCH, CONV_WIDTH - 1, CONV_DIM), 1.0),
        'state_delta': nrm((DEPTH, DEC_BATCH, DELTA_HEADS, DELTA_DK, DELTA_DV), 0.1),
        'page_table': jax.random.permutation(next(keys), n_phys)[:used].reshape(DEC_BATCH, n_pages).astype(jnp.int32),
        'p_prompt': nrm((DEPTH, BATCH, SEQ, PLE_DIM), 1.0),
        'p_sample': nrm((DEPTH, DEC_BATCH, DEC_SEQ, PLE_DIM), 1.0),
        'norm_attn': gain((DEPTH, D_MODEL)),
        'w_in': nrm((DEPTH, D_MODEL, IN_COLS), D_MODEL ** -0.5),
        'conv_w': nrm((DEPTH, CONV_WIDTH, CONV_DIM), CONV_WIDTH ** -0.5),
        'a_log': jnp.log(jax.random.uniform(next(keys), (DEPTH, DELTA_HEADS), jnp.float32, 1.0, 16.0)),
        'dt_bias': jnp.log(jnp.expm1(jax.random.uniform(next(keys), (DEPTH, DELTA_HEADS), jnp.float32, 0.001, 0.1))),
        'delta_norm': gain((DEPTH, DELTA_DV)),
        'q_norm': gain((DEPTH, DIFF_COMP)),
        'k_norm': gain((DEPTH, DIFF_COMP)),
        'lam_q1': nrm((DEPTH, DIFF_COMP), 0.1),
        'lam_k1': nrm((DEPTH, DIFF_COMP), 0.1),
        'lam_q2': nrm((DEPTH, DIFF_COMP), 0.1),
        'lam_k2': nrm((DEPTH, DIFF_COMP), 0.1),
        'subln': gain((DEPTH, DIFF_DV)),
        'w_oa': nrm((DEPTH, Z_DIM, D_MODEL), Z_DIM ** -0.5),
        'w_ob': nrm((DEPTH, DIFF_V_DIM, D_MODEL), DIFF_V_DIM ** -0.5),
        'w_out': nrm((DEPTH, D_MODEL, D_MODEL), D_MODEL ** -0.5),
        'norm_ffn': gain((DEPTH, D_MODEL)),
        'w_rg': nrm((DEPTH, D_MODEL, N_GROUPS), D_MODEL ** -0.5),
        'b_rg': nrm((DEPTH, N_GROUPS), 0.01),
        'w_re': nrm((DEPTH, D_MODEL, N_EXPERTS), D_MODEL ** -0.5),
        'b_re': nrm((DEPTH, N_EXPERTS), 0.01),
        'exp_gate': nrm((DEPTH, N_EXPERTS, D_MODEL, EXPERT_FF), D_MODEL ** -0.5),
        'exp_up': nrm((DEPTH, N_EXPERTS, D_MODEL, EXPERT_FF), D_MODEL ** -0.5),
        'exp_down': nrm((DEPTH, N_EXPERTS, EXPERT_FF, D_MODEL), EXPERT_FF ** -0.5),
        'norm_ple': gain((DEPTH, D_MODEL)),
        'w_ple': nrm((DEPTH, PLE_DIM, D_MODEL), PLE_DIM ** -0.5),
        'w_ple_gate': nrm((DEPTH, D_MODEL, D_MODEL), D_MODEL ** -0.5),
    }


def reference(x_prompt, x_sample, cache_k, cache_v, state_conv, state_delta, page_table, p_prompt, p_sample,
              norm_attn, w_in, conv_w, a_log, dt_bias, delta_norm, q_norm, k_norm, lam_q1, lam_k1, lam_q2, lam_k2,
              subln, w_oa, w_ob, w_out, norm_ffn, w_rg, b_rg, w_re, b_re, exp_gate, exp_up, exp_down,
              norm_ple, w_ple, w_ple_gate):
    pos_prompt = jnp.arange(x_prompt.shape[1], dtype=jnp.int32)
    pos_sample = PAST_LEN + jnp.arange(x_sample.shape[1], dtype=jnp.int32)
    yp, ys = x_prompt, x_sample
    kp_l, vp_l, ks_l, vs_l, cp_l, cs_l, dp_l, ds_l = [], [], [], [], [], [], [], []
    for i in range(DEPTH):
        lw = dict(norm_attn=norm_attn[i], w_in=w_in[i], conv_w=conv_w[i], a_log=a_log[i], dt_bias=dt_bias[i],
                  delta_norm=delta_norm[i], q_norm=q_norm[i], k_norm=k_norm[i], lam_q1=lam_q1[i], lam_k1=lam_k1[i],
                  lam_q2=lam_q2[i], lam_k2=lam_k2[i], subln=subln[i], w_oa=w_oa[i], w_ob=w_ob[i], w_out=w_out[i],
                  norm_ffn=norm_ffn[i], w_rg=w_rg[i], b_rg=b_rg[i], w_re=w_re[i], b_re=b_re[i],
                  exp_gate=exp_gate[i], exp_up=exp_up[i], exp_down=exp_down[i],
                  norm_ple=norm_ple[i], w_ple=w_ple[i], w_ple_gate=w_ple_gate[i])
        lam_init = 0.8 - 0.6 * math.exp(-0.3 * i)
        conv0 = jnp.zeros((x_prompt.shape[0], CONV_WIDTH - 1, CONV_DIM), x_prompt.dtype)
        s00 = jnp.zeros((x_prompt.shape[0], DELTA_HEADS, DELTA_DK, DELTA_DV), jnp.float32)
        yp, kp, vp, cp, dp = _layer(yp, p_prompt[i], pos_prompt, conv0, s00, _diff_attn_prompt, lam_init, lw)
        attend_s = functools.partial(_diff_attn_sample, cache_k=cache_k[i], cache_v=cache_v[i], page_table=page_table)
        ys, ks, vs, cs, ds = _layer(ys, p_sample[i], pos_sample, state_conv[i], state_delta[i], attend_s, lam_init, lw)
        kp_l.append(kp); vp_l.append(vp); ks_l.append(ks); vs_l.append(vs)
        cp_l.append(cp); cs_l.append(cs); dp_l.append(dp); ds_l.append(ds)
    return (yp, ys,
            jnp.stack(kp_l).astype(cache_k.dtype), jnp.stack(vp_l).astype(cache_v.dtype),
            jnp.stack(ks_l).astype(cache_k.dtype), jnp.stack(vs_l).astype(cache_v.dtype),
            jnp.stack(cp_l).astype(state_conv.dtype), jnp.stack(cs_l).astype(state_conv.dtype),
            jnp.stack(dp_l).astype(state_delta.dtype), jnp.stack(ds_l).astype(state_delta.dtype))
```

```python
import functools
import math

import jax
import jax.numpy as jnp
from jax import lax
from jax.experimental import pallas as pl
from jax.experimental.pallas import tpu as pltpu

F32 = jnp.float32
BF16 = jnp.bfloat16
HIGHEST = lax.Precision.HIGHEST

LANES = 128
DELTA_CHUNK = 64
ROPE_THETA = 10000.0
TOP_K = 2
MOE_BLOCK = 128
EPS = 1e-6
VMEM_LIMIT = 48 * 1024 * 1024


def _cparams(*sem):
    return pltpu.CompilerParams(dimension_semantics=sem, vmem_limit_bytes=VMEM_LIMIT)


def _row_tile(n, target, align=16):
    best = None
    for t in range(align, min(n, target) + 1, align):
        if n % t == 0:
            best = t
    assert best is not None, (n, target, align)
    return best


def _nt_dot(a, b, precision=None):
    return lax.dot_general(a, b, (((1,), (1,)), ((), ())), precision=precision,
                           preferred_element_type=F32)


def _tn_dot(a, b, precision=None):
    return lax.dot_general(a, b, (((0,), (0,)), ((), ())), precision=precision,
                           preferred_element_type=F32)


def _dot(a, b, precision=None):
    return jnp.dot(a, b, precision=precision, preferred_element_type=F32)


def _sigmoid(x):
    return 1.0 / (1.0 + jnp.exp(-x))


def _silu(x):
    return x * _sigmoid(x)


def _iota(shape, dim):
    return lax.broadcasted_iota(jnp.int32, shape, dim)


def _inproj_kernel(x_ref, g_ref, w_ref, wba_ref, p_ref, ba_ref, h_scr):
    @pl.when(pl.program_id(1) == 0)
    def _():
        x = x_ref[...]
        h = x * lax.rsqrt(jnp.mean(x * x, axis=-1, keepdims=True) + EPS) * g_ref[...]
        h_scr[...] = h.astype(BF16)
        ba_ref[...] = _dot(h_scr[...], wba_ref[...])

    p_ref[...] = _dot(h_scr[...], w_ref[...])


def _inproj(x, gain, w_main, w_ba):
    t, d = x.shape
    n = w_main.shape[1]
    tm = _row_tile(t, 640)
    tn = 1024
    assert n % tn == 0
    return pl.pallas_call(
        _inproj_kernel,
        out_shape=(jax.ShapeDtypeStruct((t, n), F32), jax.ShapeDtypeStruct((t, LANES), F32)),
        grid=(t // tm, n // tn),
        in_specs=[pl.BlockSpec((tm, d), lambda i, j: (i, 0)),
                  pl.BlockSpec((1, d), lambda i, j: (0, 0)),
                  pl.BlockSpec((d, tn), lambda i, j: (0, j)),
                  pl.BlockSpec((d, LANES), lambda i, j: (0, 0))],
        out_specs=(pl.BlockSpec((tm, tn), lambda i, j: (i, j)),
                   pl.BlockSpec((tm, LANES), lambda i, j: (i, 0))),
        scratch_shapes=[pltpu.VMEM((tm, d), BF16)],
        compiler_params=_cparams("parallel", "arbitrary"),
        name="inproj",
    )(x, gain, w_main, w_ba)


def _delta_kernel(qkv_ref, z_ref, ba_ref, tail0_ref, s0_ref, cw_ref, alog_ref, dtb_ref, dn_ref,
                  o_ref, sfin_ref, ext_scr, s_scr, *, chunk, valid, heads, dk):
    n = pl.program_id(1)
    c = chunk
    hk = heads * dk

    @pl.when(n == 0)
    def _():
        ext_scr[0:8, :] = tail0_ref[0]
        s_scr[...] = s0_ref[0]

    ext_scr[8:8 + c, :] = qkv_ref[...]

    row = _iota((c, 1), 0)
    rowmask = (row < valid).astype(F32) if valid < c else None

    ba = ba_ref[...]
    beta_all = _sigmoid(ba)
    xg = ba + dtb_ref[...]
    softplus = jnp.maximum(xg, 0.0) + jnp.log1p(jnp.exp(-jnp.abs(xg)))
    g_all = -jnp.exp(alog_ref[...]) * softplus
    if rowmask is not None:
        beta_all = beta_all * rowmask
        g_all = g_all * rowmask
    r_i = _iota((c, c), 0)
    c_i = _iota((c, c), 1)
    incl = r_i >= c_i
    strict = r_i > c_i
    gc_all = _dot(incl.astype(F32), g_all, precision=HIGHEST)
    gc_t = jnp.transpose(gc_all)
    eye = (r_i == c_i).astype(F32)

    def conv(col0):
        sl = slice(col0, col0 + dk)
        acc = ext_scr[8:8 + c, sl] * cw_ref[3:4, sl]
        acc = acc + ext_scr[7:7 + c, sl] * cw_ref[2:3, sl]
        acc = acc + ext_scr[6:6 + c, sl] * cw_ref[1:2, sl]
        acc = acc + ext_scr[5:5 + c, sl] * cw_ref[0:1, sl]
        return _silu(acc)

    for h in range(heads):
        q = conv(h * dk)
        k = conv(hk + h * dk)
        v = conv(2 * hk + h * dk)
        q = q * lax.rsqrt(jnp.sum(q * q, axis=-1, keepdims=True) + EPS) * (dk ** -0.5)
        k = k * lax.rsqrt(jnp.sum(k * k, axis=-1, keepdims=True) + EPS)
        if rowmask is not None:
            q = q * rowmask
            k = k * rowmask
            v = v * rowmask
        beta = beta_all[:, h:h + 1]
        gc = gc_all[:, heads + h:heads + h + 1]
        gc_row = gc_t[heads + h:heads + h + 1, :]
        g_last = gc_all[c - 1:c, heads + h:heads + h + 1]
        decay = jnp.where(incl, jnp.exp(jnp.where(incl, gc - gc_row, 0.0)), 0.0)
        kb = k * beta
        a = jnp.where(strict, _nt_dot(kb, k) * decay, 0.0)
        tinv = eye - a
        apow = a
        span = 2
        while span < c:
            apow = _dot(apow, apow, precision=HIGHEST)
            tinv = tinv + _dot(tinv, apow, precision=HIGHEST)
            span *= 2
        egc = jnp.exp(gc)
        u = _dot(tinv, v * beta)
        w = _dot(tinv, kb * egc)
        qk = _nt_dot(q, k) * decay
        s = s_scr[h]
        v_new = u - _dot(w, s)
        o = _dot(q * egc, s) + _dot(qk, v_new)
        s_scr[h] = s * jnp.exp(g_last) + _tn_dot(k * jnp.exp(g_last - gc), v_new)
        on = o * lax.rsqrt(jnp.mean(o * o, axis=-1, keepdims=True) + EPS) * dn_ref[...]
        zh = z_ref[:, h * dk:(h + 1) * dk]
        o_ref[:, h * dk:(h + 1) * dk] = (on * _silu(zh)).astype(o_ref.dtype)

    ext_scr[0:8, :] = ext_scr[c:c + 8, :]

    @pl.when(n == pl.num_programs(1) - 1)
    def _():
        sfin_ref[0] = s_scr[...]


def _delta_branch(proj, ba, row0, nb, seq, tail0, s0, conv_w, alog_row, dtb_row, delta_norm, *,
                  chunk, valid, heads, dk, z_col_block):
    hk = heads * dk
    nchunks = seq // chunk
    assert seq % chunk == 0 and row0 % chunk == 0
    r0 = row0 // chunk
    kern = functools.partial(_delta_kernel, chunk=chunk, valid=valid, heads=heads, dk=dk)
    return pl.pallas_call(
        kern,
        out_shape=(jax.ShapeDtypeStruct((nb * seq, hk), BF16),
                   jax.ShapeDtypeStruct((nb, heads, dk, dk), F32)),
        grid=(nb, nchunks),
        in_specs=[pl.BlockSpec((chunk, 3 * hk), lambda b, n: (r0 + b * nchunks + n, 0)),
                  pl.BlockSpec((chunk, hk), lambda b, n: (r0 + b * nchunks + n, z_col_block)),
                  pl.BlockSpec((chunk, LANES), lambda b, n: (r0 + b * nchunks + n, 0)),
                  pl.BlockSpec((1, 8, 3 * hk), lambda b, n: (b, 0, 0)),
                  pl.BlockSpec((1, heads, dk, dk), lambda b, n: (b, 0, 0, 0)),
                  pl.BlockSpec((4, 3 * hk), lambda b, n: (0, 0)),
                  pl.BlockSpec((1, LANES), lambda b, n: (0, 0)),
                  pl.BlockSpec((1, LANES), lambda b, n: (0, 0)),
                  pl.BlockSpec((1, dk), lambda b, n: (0, 0))],
        out_specs=(pl.BlockSpec((chunk, hk), lambda b, n: (b * nchunks + n, 0)),
                   pl.BlockSpec((1, heads, dk, dk), lambda b, n: (b, 0, 0, 0))),
        scratch_shapes=[pltpu.VMEM((8 + chunk, 3 * hk), F32),
                        pltpu.VMEM((heads, dk, dk), F32)],
        compiler_params=_cparams("parallel", "arbitrary"),
        name="delta_c%d" % chunk,
    )(proj, proj, ba, tail0, s0, conv_w, alog_row, dtb_row, delta_norm)


def _qkprep_kernel(q_ref, k_ref, v_ref, cos_ref, sin_ref, qg_ref, kg_ref,
                   qo_ref, kbo_ref, vo_ref, kfo_ref, *, comp, scale):
    tm = q_ref.shape[0]
    r_i = _iota((LANES, LANES), 0)
    c_i = _iota((LANES, LANES), 1)
    group = (r_i // comp == c_i // comp).astype(BF16)
    lane = _iota((tm, LANES), 1)
    first_half = (lane % comp) < (comp // 2)
    cos = cos_ref[...]
    sin = sin_ref[...]

    def norm_rope(x, gain):
        sq = x * x
        hi = sq.astype(BF16)
        lo = (sq - hi.astype(F32)).astype(BF16)
        ms = (_dot(hi, group) + _dot(lo, group)) * (1.0 / comp)
        y = x * lax.rsqrt(ms + EPS) * gain
        swapped = jnp.where(first_half, pltpu.roll(y, LANES - comp // 2, 1), pltpu.roll(y, comp // 2, 1))
        return y * cos + swapped * sin

    for j in range(q_ref.shape[1] // LANES):
        sl = slice(j * LANES, (j + 1) * LANES)
        qr = norm_rope(q_ref[:, sl], qg_ref[...])
        kr = norm_rope(k_ref[:, sl], kg_ref[...])
        qo_ref[:, sl] = (qr * scale).astype(BF16)
        kbo_ref[:, sl] = kr.astype(BF16)
        kfo_ref[:, sl] = kr
    vo_ref[...] = v_ref[...].astype(BF16)


def _qkprep(proj, cos_t, sin_t, q_gain, k_gain, *, q_col_block, width, comp):
    t = proj.shape[0]
    tm = _row_tile(t, 640)
    kern = functools.partial(_qkprep_kernel, comp=comp, scale=comp ** -0.5)
    col = lambda cb: pl.BlockSpec((tm, width), lambda i: (i, cb))
    small = pl.BlockSpec((tm, LANES), lambda i: (i, 0))
    gain = pl.BlockSpec((1, LANES), lambda i: (0, 0))
    out = pl.BlockSpec((tm, width), lambda i: (i, 0))
    return pl.pallas_call(
        kern,
        out_shape=(jax.ShapeDtypeStruct((t, width), BF16), jax.ShapeDtypeStruct((t, width), BF16),
                   jax.ShapeDtypeStruct((t, width), BF16), jax.ShapeDtypeStruct((t, width), F32)),
        grid=(t // tm,),
        in_specs=[col(q_col_block), col(q_col_block + 1), col(q_col_block + 2), small, small, gain, gain],
        out_specs=(out, out, out, out),
        compiler_params=_cparams("parallel"),
        name="qkprep",
    )(proj, proj, proj, cos_t, sin_t, q_gain, k_gain)


def _lambda_value(lam_ref, lam_init):
    lv = lam_ref[...]
    s1 = jnp.sum(lv[0:1] * lv[1:2], axis=-1, keepdims=True)
    s2 = jnp.sum(lv[2:3] * lv[3:4], axis=-1, keepdims=True)
    return jnp.exp(s1) - jnp.exp(s2) + lam_init


def _attn_prompt_kernel(q_ref, k_ref, v_ref, lam_ref, sub_ref, o_ref, *, tq, comp, lam_init):
    i = pl.program_id(2)
    dv = v_ref.shape[1]
    lane = _iota((tq, LANES), 1)
    q = q_ref[...]
    zero = jnp.zeros_like(q)
    qc = (jnp.where(lane < comp, q, zero), jnp.where(lane >= comp, q, zero))

    def update(carry, kt, vt, mask):
        new = []
        for c in range(2):
            m, l, acc = carry[c]
            s = _nt_dot(qc[c], kt)
            if mask is not None:
                s = jnp.where(mask, s, -jnp.inf)
            m_new = jnp.maximum(m, jnp.max(s, axis=-1, keepdims=True))
            p = jnp.exp(s - m_new)
            corr = jnp.exp(m - m_new)
            l = l * corr + jnp.sum(p, axis=-1, keepdims=True)
            acc = acc * corr + _dot(p.astype(BF16), vt)
            new.append((m_new, l, acc))
        return tuple(new)

    def body(j, carry):
        start = pl.multiple_of(j * tq, tq)
        return update(carry, k_ref[pl.ds(start, tq), :], v_ref[pl.ds(start, tq), :], None)

    init = tuple((jnp.full((tq, 1), -jnp.inf, F32), jnp.zeros((tq, 1), F32), jnp.zeros((tq, dv), F32))
                 for _ in range(2))
    carry = lax.fori_loop(0, i, body, init)
    start = pl.multiple_of(i * tq, tq)
    diag = _iota((tq, tq), 0) >= _iota((tq, tq), 1)
    carry = update(carry, k_ref[pl.ds(start, tq), :], v_ref[pl.ds(start, tq), :], diag)
    (_, l0, a0), (_, l1, a1) = carry
    lam = _lambda_value(lam_ref, lam_init)
    o = a0 / l0 - lam * (a1 / l1)
    on = o * lax.rsqrt(jnp.mean(o * o, axis=-1, keepdims=True) + EPS) * sub_ref[...]
    o_ref[...] = (on * (1.0 - lam_init)).astype(o_ref.dtype)


def _attn_prompt(q, k, v, lam_vecs, subln, *, nb, seq, heads, comp, lam_init):
    dv = 2 * comp
    tq = _row_tile(seq, 256)
    nq = seq // tq
    kern = functools.partial(_attn_prompt_kernel, tq=tq, comp=comp, lam_init=lam_init)
    return pl.pallas_call(
        kern,
        out_shape=jax.ShapeDtypeStruct((nb * seq, heads * dv), BF16),
        grid=(nb, heads, nq),
        in_specs=[pl.BlockSpec((tq, dv), lambda b, h, i: (b * nq + i, h)),
                  pl.BlockSpec((seq, dv), lambda b, h, i: (b, h)),
                  pl.BlockSpec((seq, dv), lambda b, h, i: (b, h)),
                  pl.BlockSpec((4, comp), lambda b, h, i: (0, 0)),
                  pl.BlockSpec((1, dv), lambda b, h, i: (0, 0))],
        out_specs=pl.BlockSpec((tq, dv), lambda b, h, i: (b * nq + i, h)),
        compiler_params=_cparams("parallel", "parallel", "arbitrary"),
        name="attn_prompt",
    )(q, k, v, lam_vecs, subln)


def _attn_sample_kernel(pt_ref, q_ref, kn_ref, vn_ref, lam_ref, sub_ref, *rest,
                        pages_per_step, new_len, heads, comp, lam_init):
    pp = pages_per_step
    k_refs = rest[:pp]
    v_refs = rest[pp:2 * pp]
    o_ref = rest[2 * pp]
    qbd_scr, m_scr, l_scr, acc_scr = rest[2 * pp + 1:]
    p_id = pl.program_id(1)
    hc = 2 * heads
    rows = new_len * hc
    width = hc * comp

    @pl.when(p_id == 0)
    def _():
        q = q_ref[0]
        blockmask = (_iota((hc, width), 1) // comp == _iota((hc, width), 0)).astype(F32)
        for t in range(new_len):
            qbd_scr[t * hc:(t + 1) * hc, :] = jnp.broadcast_to(q[t:t + 1, :], (hc, width)) * blockmask
        m_scr[...] = jnp.full(m_scr.shape, -jnp.inf, F32)
        l_scr[...] = jnp.zeros(l_scr.shape, F32)
        acc_scr[...] = jnp.zeros(acc_scr.shape, F32)

    qbd = qbd_scr[...]

    def online(s_list, v_list):
        m = m_scr[...]
        m_new = m
        for s in s_list:
            m_new = jnp.maximum(m_new, jnp.max(s, axis=-1, keepdims=True))
        corr = jnp.exp(m - m_new)
        l = l_scr[...] * corr
        acc = acc_scr[...] * corr
        for s, vv in zip(s_list, v_list):
            p = jnp.exp(s - m_new)
            l = l + jnp.sum(p, axis=-1, keepdims=True)
            acc = acc + _dot(p, vv)
        m_scr[...] = m_new
        l_scr[...] = l
        acc_scr[...] = acc

    online([_nt_dot(qbd, k_refs[r][0]) for r in range(pp)], [v_refs[r][0] for r in range(pp)])

    @pl.when(p_id == pl.num_programs(1) - 1)
    def _():
        s = _nt_dot(qbd, kn_ref[0])
        tok = _iota((rows, 8), 0) // hc
        key = _iota((rows, 8), 1)
        s = jnp.where((tok >= key) & (key < new_len), s, -jnp.inf)
        online([s], [vn_ref[0]])
        lam = _lambda_value(lam_ref, lam_init)
        nrm = acc_scr[...] / l_scr[...]
        r_hc = _iota((hc, width), 0)
        own = (_iota((hc, width), 1) // (2 * comp)) == (r_hc // 2)
        sign = jnp.where(r_hc % 2 == 0, 1.0, -lam)
        pick = jnp.where(own, sign, 0.0)
        outs = []
        for t in range(new_len):
            outs.append(jnp.sum(nrm[t * hc:(t + 1) * hc, :] * pick, axis=0, keepdims=True))
        if new_len < 8:
            outs.append(jnp.zeros((8 - new_len, width), F32))
        o = jnp.concatenate(outs, axis=0)
        dv = 2 * comp
        for h in range(heads):
            oh = o[:, h * dv:(h + 1) * dv]
            on = oh * lax.rsqrt(jnp.mean(oh * oh, axis=-1, keepdims=True) + EPS) * sub_ref[...]
            o_ref[0, :, h * dv:(h + 1) * dv] = (on * (1.0 - lam_init)).astype(o_ref.dtype)


def _attn_sample(q8, k8, v8, cache_k, cache_v, page_table, lam_vecs, subln, *, new_len, heads, comp, lam_init):
    nb, n_pages = page_table.shape
    page, width = cache_k.shape[1], cache_k.shape[2]
    pp = 4
    assert n_pages % pp == 0
    hc = 2 * heads
    kern = functools.partial(_attn_sample_kernel, pages_per_step=pp, new_len=new_len, heads=heads,
                             comp=comp, lam_init=lam_init)
    per_req = pl.BlockSpec((1, 8, width), lambda b, p, pt: (b, 0, 0))

    def page_spec(r):
        return pl.BlockSpec((1, page, width), lambda b, p, pt: (pt[b * n_pages + p * pp + r], 0, 0))

    grid_spec = pltpu.PrefetchScalarGridSpec(
        num_scalar_prefetch=1,
        grid=(nb, n_pages // pp),
        in_specs=[per_req, per_req, per_req,
                  pl.BlockSpec((4, comp), lambda b, p, pt: (0, 0)),
                  pl.BlockSpec((1, 2 * comp), lambda b, p, pt: (0, 0))]
                 + [page_spec(r) for r in range(pp)] + [page_spec(r) for r in range(pp)],
        out_specs=pl.BlockSpec((1, 8, width), lambda b, p, pt: (b, 0, 0)),
        scratch_shapes=[pltpu.VMEM((new_len * hc, width), F32), pltpu.VMEM((new_len * hc, 1), F32),
                        pltpu.VMEM((new_len * hc, 1), F32), pltpu.VMEM((new_len * hc, width), F32)],
    )
    return pl.pallas_call(
        kern,
        out_shape=jax.ShapeDtypeStruct((nb, 8, width), BF16),
        grid_spec=grid_spec,
        compiler_params=_cparams("parallel", "arbitrary"),
        name="attn_sample",
    )(page_table.reshape(-1), q8, k8, v8, lam_vecs, subln, *([cache_k] * pp), *([cache_v] * pp))


def _merge_kernel(oa_ref, ob_ref, ga_ref, gb_ref, woa_ref, wob_ref, m_ref):
    ua = _dot(oa_ref[...], woa_ref[...])
    ub = _dot(ob_ref[...], wob_ref[...])
    m_ref[...] = (_sigmoid(ga_ref[...]) * ua + _sigmoid(gb_ref[...]) * ub).astype(m_ref.dtype)


def _merge(o_a, o_b, proj, w_oa, w_ob, *, gate_col_block):
    t, zd = o_a.shape
    d = w_oa.shape[1]
    tm = _row_tile(t, 320)
    return pl.pallas_call(
        _merge_kernel,
        out_shape=jax.ShapeDtypeStruct((t, d), BF16),
        grid=(t // tm,),
        in_specs=[pl.BlockSpec((tm, zd), lambda i: (i, 0)),
                  pl.BlockSpec((tm, o_b.shape[1]), lambda i: (i, 0)),
                  pl.BlockSpec((tm, d), lambda i: (i, gate_col_block)),
                  pl.BlockSpec((tm, d), lambda i: (i, gate_col_block + 1)),
                  pl.BlockSpec(w_oa.shape, lambda i: (0, 0)),
                  pl.BlockSpec(w_ob.shape, lambda i: (0, 0))],
        out_specs=pl.BlockSpec((tm, d), lambda i: (i, 0)),
        compiler_params=_cparams("parallel"),
        name="merge",
    )(o_a, o_b, proj, proj, w_oa, w_ob)


def _outproj_route_kernel(x_ref, m_ref, wo_ref, g_ref, wr_ref, br_ref,
                          x1_ref, h2_ref, rw_ref, re_ref, *, n_groups, per_group):
    x1 = x_ref[...] + _dot(m_ref[...], wo_ref[...])
    x1_ref[...] = x1
    h2 = x1 * lax.rsqrt(jnp.mean(x1 * x1, axis=-1, keepdims=True) + EPS) * g_ref[...]
    h2_ref[...] = h2
    logits = _dot(h2.astype(BF16), wr_ref[...]) + br_ref[...]
    tm = logits.shape[0]
    lane = _iota((tm, LANES), 1).astype(F32)
    big = float(LANES)

    def masked_softmax(mask):
        lm = jnp.where(mask, logits, -jnp.inf)
        e = jnp.exp(lm - jnp.max(lm, axis=-1, keepdims=True))
        return e / jnp.sum(e, axis=-1, keepdims=True)

    def top1(p, mask):
        v = jnp.max(jnp.where(mask, p, -1.0), axis=-1, keepdims=True)
        idx = jnp.min(jnp.where(mask & (p == v), lane, big), axis=-1, keepdims=True)
        return v, idx

    gmask = lane < n_groups
    pg = masked_softmax(gmask)
    g_w, g_top = top1(pg, gmask)
    lo = n_groups + g_top * per_group
    emask = (lane >= lo) & (lane < lo + per_group)
    pe = masked_softmax(emask)
    v1, i1 = top1(pe, emask)
    emask2 = emask & (lane != i1)
    v2, i2 = top1(pe, emask2)
    denom = v1 + v2
    w1 = g_w * v1 / denom
    w2 = g_w * v2 / denom
    rw_ref[...] = jnp.where(lane == 0.0, w1, jnp.where(lane == 1.0, w2, 0.0))
    re_ref[...] = jnp.where(lane == 0.0, i1 - n_groups,
                            jnp.where(lane == 1.0, i2 - n_groups, 0.0)).astype(jnp.int32)


def _outproj_route(x, merged, w_out, gain, w_route, b_route, *, n_groups, per_group):
    t, d = x.shape
    tm = _row_tile(t, 320)
    kern = functools.partial(_outproj_route_kernel, n_groups=n_groups, per_group=per_group)
    row = lambda w: pl.BlockSpec((tm, w), lambda i: (i, 0))
    return pl.pallas_call(
        kern,
        out_shape=(jax.ShapeDtypeStruct((t, d), F32), jax.ShapeDtypeStruct((t, d), F32),
                   jax.ShapeDtypeStruct((t, LANES), F32), jax.ShapeDtypeStruct((t, LANES), jnp.int32)),
        grid=(t // tm,),
        in_specs=[row(d), row(d),
                  pl.BlockSpec(w_out.shape, lambda i: (0, 0)),
                  pl.BlockSpec((1, d), lambda i: (0, 0)),
                  pl.BlockSpec((d, LANES), lambda i: (0, 0)),
                  pl.BlockSpec((1, LANES), lambda i: (0, 0))],
        out_specs=(row(d), row(d), row(LANES), row(LANES)),
        compiler_params=_cparams("parallel"),
        name="outproj_route",
    )(x, merged, w_out, gain, w_route, b_route)


def _experts_kernel(be_ref, nused_ref, src_ref, src_next_ref, dst_ref, h_hbm, wg_ref, wu_ref, wd_ref, y_hbm,
                    xbuf, ybuf, gsem, ssem, *, n_slots):
    g = pl.program_id(0)
    nused = nused_ref[0]
    blk = MOE_BLOCK

    def gather_copy(idx_ref, slot, r):
        tok = idx_ref[0, 0, r]
        return pltpu.make_async_copy(h_hbm.at[pl.ds(tok, 1), :], xbuf.at[slot, pl.ds(r, 1), :], gsem.at[slot])

    def start_gather(idx_ref, slot):
        def body(r, carry):
            gather_copy(idx_ref, slot, r).start()
            return carry
        lax.fori_loop(0, blk, body, 0)

    def wait_gather(idx_ref, slot):
        def body(r, carry):
            gather_copy(idx_ref, slot, r).wait()
            return carry
        lax.fori_loop(0, blk, body, 0)

    def scatter_copy(r):
        dst = dst_ref[0, 0, r]
        return dst, pltpu.make_async_copy(ybuf.at[pl.ds(r, 1), :], y_hbm.at[pl.ds(dst, 1), :], ssem.at[0])

    @pl.when(g < nused)
    def _():
        slot = g % 2

        @pl.when(g == 0)
        def _():
            start_gather(src_ref, slot)

        @pl.when(g + 1 < nused)
        def _():
            start_gather(src_next_ref, 1 - slot)

        wait_gather(src_ref, slot)
        x = xbuf[slot]
        hdn = _silu(_dot(x, wg_ref[0])) * _dot(x, wu_ref[0])
        ybuf[...] = _dot(hdn, wd_ref[0])

        def sbody(r, carry):
            dst, cp = scatter_copy(r)

            @pl.when(dst < n_slots)
            def _():
                cp.start()
            return carry
        lax.fori_loop(0, blk, sbody, 0)

        def wbody(r, carry):
            dst, cp = scatter_copy(r)

            @pl.when(dst < n_slots)
            def _():
                cp.wait()
            return carry
        lax.fori_loop(0, blk, wbody, 0)


def _experts(h2, block_e, src_tok, dst_slot, nused, w_gate, w_up, w_down):
    t, d = h2.shape
    n_blocks = block_e.shape[0]
    n_slots = t * TOP_K
    ff = w_gate.shape[2]
    kern = functools.partial(_experts_kernel, n_slots=n_slots)
    src3 = src_tok.reshape(n_blocks, 1, MOE_BLOCK)
    dst3 = dst_slot.reshape(n_blocks, 1, MOE_BLOCK)
    idx_spec = lambda f: pl.BlockSpec((1, 1, MOE_BLOCK), lambda g, be, nu: (f(g), 0, 0), memory_space=pltpu.SMEM)
    grid_spec = pltpu.PrefetchScalarGridSpec(
        num_scalar_prefetch=2,
        grid=(n_blocks,),
        in_specs=[idx_spec(lambda g: g),
                  idx_spec(lambda g: jnp.minimum(g + 1, n_blocks - 1)),
                  idx_spec(lambda g: g),
                  pl.BlockSpec(memory_space=pl.ANY),
                  pl.BlockSpec((1, d, ff), lambda g, be, nu: (be[g], 0, 0)),
                  pl.BlockSpec((1, d, ff), lambda g, be, nu: (be[g], 0, 0)),
                  pl.BlockSpec((1, ff, d), lambda g, be, nu: (be[g], 0, 0))],
        out_specs=pl.BlockSpec(memory_space=pl.ANY),
        scratch_shapes=[pltpu.VMEM((2, MOE_BLOCK, d), F32), pltpu.VMEM((MOE_BLOCK, d), F32),
                        pltpu.SemaphoreType.DMA((2,)), pltpu.SemaphoreType.DMA((1,))],
    )
    return pl.pallas_call(
        kern,
        out_shape=jax.ShapeDtypeStruct((n_slots, d), F32),
        grid_spec=grid_spec,
        compiler_params=_cparams("arbitrary"),
        name="experts",
    )(block_e, nused, src3, src3, dst3, h2, w_gate, w_up, w_down)


def _expert_plan(eidx, n_experts):
    t = eidx.shape[0]
    s = t * TOP_K
    flat_e = eidx.reshape(-1)
    order = jnp.argsort(flat_e).astype(jnp.int32)
    sorted_e = flat_e[order]
    counts = jnp.zeros((n_experts,), jnp.int32).at[flat_e].add(1)
    padded = (counts + MOE_BLOCK - 1) // MOE_BLOCK * MOE_BLOCK
    ends = jnp.cumsum(padded)
    pad_start = ends - padded
    start = jnp.cumsum(counts) - counts
    dest = pad_start[sorted_e] + jnp.arange(s, dtype=jnp.int32) - start[sorted_e]
    n_blocks = -(-(s + n_experts * (MOE_BLOCK - 1)) // MOE_BLOCK)
    rows = n_blocks * MOE_BLOCK
    src_tok = jnp.zeros((rows,), jnp.int32).at[dest].set(order // TOP_K)
    dst_slot = jnp.full((rows,), s, jnp.int32).at[dest].set(order)
    block_e = jnp.minimum(jnp.searchsorted(ends, jnp.arange(n_blocks, dtype=jnp.int32) * MOE_BLOCK, side='right'),
                          n_experts - 1).astype(jnp.int32)
    nused = (ends[-1:] // MOE_BLOCK).astype(jnp.int32)
    return block_e, src_tok, dst_slot, nused


def _combine_ple_kernel(x1_ref, ys_ref, rw_ref, p_ref, wp_ref, g_ref, wpg_ref, y_ref):
    d = x1_ref.shape[1]
    rw = rw_ref[...]
    moe = rw[:, 0:1] * ys_ref[:, 0:d] + rw[:, 1:2] * ys_ref[:, d:2 * d]
    x2 = x1_ref[...] + moe
    hn = x2 * lax.rsqrt(jnp.mean(x2 * x2, axis=-1, keepdims=True) + EPS) * g_ref[...]
    gate = _sigmoid(_dot(hn.astype(BF16), wpg_ref[...]))
    y_ref[...] = x2 + _dot(p_ref[...], wp_ref[...]) * gate


def _combine_ple(x1, y_slots2, route_w, p, w_ple, gain, w_ple_gate):
    t, d = x1.shape
    tm = _row_tile(t, 320)
    row = lambda w: pl.BlockSpec((tm, w), lambda i: (i, 0))
    return pl.pallas_call(
        _combine_ple_kernel,
        out_shape=jax.ShapeDtypeStruct((t, d), F32),
        grid=(t // tm,),
        in_specs=[row(d), row(2 * d), row(LANES), row(p.shape[1]),
                  pl.BlockSpec(w_ple.shape, lambda i: (0, 0)),
                  pl.BlockSpec((1, d), lambda i: (0, 0)),
                  pl.BlockSpec(w_ple_gate.shape, lambda i: (0, 0))],
        out_specs=row(d),
        compiler_params=_cparams("parallel"),
        name="combine_ple",
    )(x1, y_slots2, route_w, p, w_ple, gain, w_ple_gate)


def _pad_rows(a, rows):
    return jnp.pad(a, ((0, 0), (0, rows - a.shape[1]), (0, 0)))


def _layer(x_all, p_all, nb_p, seq_p, nb_s, seq_s, past_len, state_conv, state_delta, cache_k, cache_v,
           page_table, lam_init, lw):
    tp = nb_p * seq_p
    ts = nb_s * seq_s
    d = x_all.shape[1]
    heads_d = lw['a_log'].shape[0]
    dk = lw['delta_norm'].shape[0]
    hk = heads_d * dk
    conv_dim = 3 * hk
    comp = lw['q_norm'].shape[0]
    dv = 2 * comp
    width = lw['w_ob'].shape[0]
    heads_a = width // dv
    n_groups = lw['w_rg'].shape[1]
    n_experts = lw['w_re'].shape[1]
    assert seq_s <= 8 and conv_dim % 1024 == 0 and hk == 1024 and width == 1024 and d % 1024 == 0
    assert 2 * heads_d <= LANES and n_groups + n_experts <= LANES and lw['conv_w'].shape[0] == 4

    w_in = lw['w_in']
    ba0 = conv_dim + hk
    q0 = ba0 + 2 * heads_d
    gate0 = q0 + 3 * width
    w_main = jnp.concatenate([w_in[:, :ba0], w_in[:, gate0:], w_in[:, q0:gate0]], axis=1).astype(BF16)
    w_ba = jnp.pad(w_in[:, ba0:q0], ((0, 0), (0, LANES - 2 * heads_d))).astype(BF16)
    z_cb = conv_dim // hk
    gate_cb = (conv_dim + hk) // d
    q_cb = (conv_dim + hk + 2 * d) // width
    assert (conv_dim + hk) % d == 0 and (conv_dim + hk + 2 * d) % width == 0

    proj, ba = _inproj(x_all, lw['norm_attn'].reshape(1, d), w_main, w_ba)

    alog_row = jnp.pad(lw['a_log'], (heads_d, LANES - 2 * heads_d)).reshape(1, LANES)
    dtb_row = jnp.pad(lw['dt_bias'], (heads_d, LANES - 2 * heads_d)).reshape(1, LANES)
    dn = lw['delta_norm'].reshape(1, dk)
    chunk_p = math.gcd(seq_p, DELTA_CHUNK)
    oa_p, s_p = _delta_branch(proj, ba, 0, nb_p, seq_p,
                              jnp.zeros((nb_p, 8, conv_dim), F32), jnp.zeros((nb_p, heads_d, dk, dk), F32),
                              lw['conv_w'], alog_row, dtb_row, dn,
                              chunk=chunk_p, valid=chunk_p, heads=heads_d, dk=dk, z_col_block=z_cb)
    qkv_s = proj[tp:, :conv_dim].reshape(nb_s, seq_s, conv_dim)
    proj_s8 = _pad_rows(proj[tp:].reshape(nb_s, seq_s, -1), 8).reshape(nb_s * 8, -1)
    ba_s8 = _pad_rows(ba[tp:].reshape(nb_s, seq_s, LANES), 8).reshape(nb_s * 8, LANES)
    tail_s = jnp.pad(state_conv, ((0, 0), (8 - state_conv.shape[1], 0), (0, 0)))
    oa_s8, s_s = _delta_branch(proj_s8, ba_s8, 0, nb_s, 8, tail_s, state_delta,
                               lw['conv_w'], alog_row, dtb_row, dn,
                               chunk=8, valid=seq_s, heads=heads_d, dk=dk, z_col_block=z_cb)
    oa_s = oa_s8.reshape(nb_s, 8, hk)[:, :seq_s].reshape(ts, hk)
    o_a = jnp.concatenate([oa_p, oa_s], axis=0)
    conv_p = proj[:tp, :conv_dim].reshape(nb_p, seq_p, conv_dim)[:, seq_p - 3:, :]
    conv_s = jnp.concatenate([state_conv, qkv_s], axis=1)[:, -3:, :]

    half = comp // 2
    inv_freq = ROPE_THETA ** (-jnp.arange(half, dtype=F32) / half)
    pos = jnp.concatenate([jnp.tile(jnp.arange(seq_p, dtype=jnp.int32), nb_p),
                           jnp.tile(past_len + jnp.arange(seq_s, dtype=jnp.int32), nb_s)]).astype(F32)
    ang = pos[:, None] * inv_freq[None, :]
    cos_t = jnp.tile(jnp.cos(ang), (1, LANES // half))
    sin_h = jnp.sin(ang)
    sin_t = jnp.tile(jnp.concatenate([-sin_h, sin_h], axis=1), (1, LANES // comp))
    q_bf, k_bf, v_bf, k_f = _qkprep(proj, cos_t, sin_t,
                                    jnp.tile(lw['q_norm'], LANES // comp).reshape(1, LANES),
                                    jnp.tile(lw['k_norm'], LANES // comp).reshape(1, LANES),
                                    q_col_block=q_cb, width=width, comp=comp)
    lam_vecs = jnp.stack([lw['lam_q1'], lw['lam_k1'], lw['lam_q2'], lw['lam_k2']])
    subln = lw['subln'].reshape(1, dv)
    ob_p = _attn_prompt(q_bf, k_bf, v_bf, lam_vecs, subln, nb=nb_p, seq=seq_p, heads=heads_a, comp=comp,
                        lam_init=lam_init)
    v_f_s = proj[tp:, (q_cb + 2) * width:(q_cb + 3) * width]
    q8 = _pad_rows(q_bf[tp:].astype(F32).reshape(nb_s, seq_s, width), 8)
    k8 = _pad_rows(k_f[tp:].reshape(nb_s, seq_s, width), 8)
    v8 = _pad_rows(v_f_s.reshape(nb_s, seq_s, width), 8)
    n_phys, page = cache_k.shape[0], cache_k.shape[1]
    ob_s8 = _attn_sample(q8, k8, v8, cache_k.reshape(n_phys, page, width), cache_v.reshape(n_phys, page, width),
                         page_table, lam_vecs, subln, new_len=seq_s, heads=heads_a, comp=comp, lam_init=lam_init)
    o_b = jnp.concatenate([ob_p, ob_s8[:, :seq_s].reshape(ts, width)], axis=0)

    merged = _merge(o_a, o_b, proj, lw['w_oa'].astype(BF16), lw['w_ob'].astype(BF16), gate_col_block=gate_cb)
    w_route = jnp.pad(jnp.concatenate([lw['w_rg'], lw['w_re']], axis=1),
                      ((0, 0), (0, LANES - n_groups - n_experts))).astype(BF16)
    b_route = jnp.pad(jnp.concatenate([lw['b_rg'], lw['b_re']]), (0, LANES - n_groups - n_experts)).reshape(1, LANES)
    x1, h2, route_w, route_e = _outproj_route(x_all, merged, lw['w_out'].astype(BF16),
                                              lw['norm_ffn'].reshape(1, d), w_route, b_route,
                                              n_groups=n_groups, per_group=n_experts // n_groups)

    block_e, src_tok, dst_slot, nused = _expert_plan(route_e[:, :TOP_K], n_experts)
    y_slots = _experts(h2, block_e, src_tok, dst_slot, nused, lw['exp_gate'], lw['exp_up'], lw['exp_down'])

    y = _combine_ple(x1, y_slots.reshape(-1, TOP_K * d), route_w, p_all.astype(BF16),
                     lw['w_ple'].astype(BF16), lw['norm_ple'].reshape(1, d), lw['w_ple_gate'].astype(BF16))

    k_p = k_f[:tp].reshape(nb_p, seq_p, heads_a, 2, comp)
    v_p = proj[:tp, (q_cb + 2) * width:(q_cb + 3) * width].reshape(nb_p, seq_p, heads_a, dv)
    k_s = k_f[tp:].reshape(nb_s, seq_s, heads_a, 2, comp)
    v_s = v_f_s.reshape(nb_s, seq_s, heads_a, dv)
    return y, k_p, v_p, k_s, v_s, conv_p, conv_s, s_p, s_s


def kernel(x_prompt, x_sample, cache_k, cache_v, state_conv, state_delta, page_table, p_prompt, p_sample,
           norm_attn, w_in, conv_w, a_log, dt_bias, delta_norm, q_norm, k_norm, lam_q1, lam_k1, lam_q2, lam_k2,
           subln, w_oa, w_ob, w_out, norm_ffn, w_rg, b_rg, w_re, b_re, exp_gate, exp_up, exp_down,
           norm_ple, w_ple, w_ple_gate):
    nb_p, seq_p, d = x_prompt.shape
    nb_s, seq_s, _ = x_sample.shape
    tp = nb_p * seq_p
    depth = w_in.shape[0]
    past_len = page_table.shape[1] * cache_k.shape[2]
    x_all = jnp.concatenate([x_prompt.reshape(tp, d), x_sample.reshape(nb_s * seq_s, d)], axis=0)
    outs = [[] for _ in range(8)]
    for i in range(depth):
        lw = dict(norm_attn=norm_attn[i], w_in=w_in[i], conv_w=conv_w[i], a_log=a_log[i], dt_bias=dt_bias[i],
                  delta_norm=delta_norm[i], q_norm=q_norm[i], k_norm=k_norm[i], lam_q1=lam_q1[i], lam_k1=lam_k1[i],
                  lam_q2=lam_q2[i], lam_k2=lam_k2[i], subln=subln[i], w_oa=w_oa[i], w_ob=w_ob[i], w_out=w_out[i],
                  norm_ffn=norm_ffn[i], w_rg=w_rg[i], b_rg=b_rg[i], w_re=w_re[i], b_re=b_re[i],
                  exp_gate=exp_gate[i], exp_up=exp_up[i], exp_down=exp_down[i],
                  norm_ple=norm_ple[i], w_ple=w_ple[i], w_ple_gate=w_ple_gate[i])
        lam_init = 0.8 - 0.6 * math.exp(-0.3 * i)
        p_all = jnp.concatenate([p_prompt[i].reshape(tp, -1), p_sample[i].reshape(nb_s * seq_s, -1)], axis=0)
        res = _layer(x_all, p_all, nb_p, seq_p, nb_s, seq_s, past_len, state_conv[i], state_delta[i],
                     cache_k[i], cache_v[i], page_table, lam_init, lw)
        x_all = res[0]
        for lst, val in zip(outs, res[1:]):
            lst.append(val)
    y_p = x_all[:tp].reshape(nb_p, seq_p, d)
    y_s = x_all[tp:].reshape(nb_s, seq_s, d)
    return (y_p, y_s) + tuple(jnp.stack(lst) for lst in outs)
```

```python
import functools
import math

import jax
import jax.numpy as jnp
from jax import lax
from jax.experimental import pallas as pl
from jax.experimental.pallas import tpu as pltpu

F32 = jnp.float32
BF16 = jnp.bfloat16
HIGHEST = lax.Precision.HIGHEST

LANES = 128
DELTA_CHUNK = 64
ROPE_THETA = 10000.0
TOP_K = 2
MOE_BLOCK = 128
EPS = 1e-6
VMEM_LIMIT = 48 * 1024 * 1024


def _cparams(*sem):
    return pltpu.CompilerParams(dimension_semantics=sem, vmem_limit_bytes=VMEM_LIMIT)


def _row_tile(n, target, align=16):
    best = None
    for t in range(align, min(n, target) + 1, align):
        if n % t == 0:
            best = t
    assert best is not None, (n, target, align)
    return best


def _nt_dot(a, b, precision=None):
    return lax.dot_general(a, b, (((1,), (1,)), ((), ())), precision=precision,
                           preferred_element_type=F32)


def _tn_dot(a, b, precision=None):
    return lax.dot_general(a, b, (((0,), (0,)), ((), ())), precision=precision,
                           preferred_element_type=F32)


def _dot(a, b, precision=None):
    return jnp.dot(a, b, precision=precision, preferred_element_type=F32)


def _sigmoid(x):
    return 1.0 / (1.0 + jnp.exp(-x))


def _silu(x):
    return x * _sigmoid(x)


def _iota(shape, dim):
    return lax.broadcasted_iota(jnp.int32, shape, dim)


def _inproj_kernel(x_ref, g_ref, w_ref, wba_ref, p_ref, ba_ref, h_scr):
    @pl.when(pl.program_id(1) == 0)
    def _():
        x = x_ref[...]
        h = x * lax.rsqrt(jnp.mean(x * x, axis=-1, keepdims=True) + EPS) * g_ref[...]
        h_scr[...] = h.astype(BF16)
        ba_ref[...] = _dot(h_scr[...], wba_ref[...])

    p_ref[...] = _dot(h_scr[...], w_ref[...])


def _inproj(x, gain, w_main, w_ba):
    t, d = x.shape
    n = w_main.shape[1]
    tm = _row_tile(t, 640)
    tn = 1024
    assert n % tn == 0
    return pl.pallas_call(
        _inproj_kernel,
        out_shape=(jax.ShapeDtypeStruct((t, n), F32), jax.ShapeDtypeStruct((t, LANES), F32)),
        grid=(t // tm, n // tn),
        in_specs=[pl.BlockSpec((tm, d), lambda i, j: (i, 0)),
                  pl.BlockSpec((1, d), lambda i, j: (0, 0)),
                  pl.BlockSpec((d, tn), lambda i, j: (0, j)),
                  pl.BlockSpec((d, LANES), lambda i, j: (0, 0))],
        out_specs=(pl.BlockSpec((tm, tn), lambda i, j: (i, j)),
                   pl.BlockSpec((tm, LANES), lambda i, j: (i, 0))),
        scratch_shapes=[pltpu.VMEM((tm, d), BF16)],
        compiler_params=_cparams("parallel", "arbitrary"),
        name="inproj",
    )(x, gain, w_main, w_ba)


def _delta_kernel(qkv_ref, z_ref, ba_ref, tail0_ref, s0_ref, cw_ref, alog_ref, dtb_ref, dn_ref,
                  o_ref, sfin_ref, ext_scr, s_scr, *, chunk, valid, heads, dk):
    n = pl.program_id(1)
    c = chunk
    hk = heads * dk

    @pl.when(n == 0)
    def _():
        ext_scr[0:8, :] = tail0_ref[0]
        s_scr[...] = s0_ref[0]

    ext_scr[8:8 + c, :] = qkv_ref[...]

    row = _iota((c, 1), 0)
    rowmask = (row < valid).astype(F32) if valid < c else None

    ba = ba_ref[...]
    beta_all = _sigmoid(ba)
    xg = ba + dtb_ref[...]
    softplus = jnp.maximum(xg, 0.0) + jnp.log1p(jnp.exp(-jnp.abs(xg)))
    g_all = -jnp.exp(alog_ref[...]) * softplus
    if rowmask is not None:
        beta_all = beta_all * rowmask
        g_all = g_all * rowmask
    r_i = _iota((c, c), 0)
    c_i = _iota((c, c), 1)
    incl = r_i >= c_i
    strict = r_i > c_i
    gc_all = _dot(incl.astype(F32), g_all, precision=HIGHEST)
    gc_t = jnp.transpose(gc_all)
    eye = (r_i == c_i).astype(F32)

    def conv(col0):
        sl = slice(col0, col0 + dk)
        acc = ext_scr[8:8 + c, sl] * cw_ref[3:4, sl]
        acc = acc + ext_scr[7:7 + c, sl] * cw_ref[2:3, sl]
        acc = acc + ext_scr[6:6 + c, sl] * cw_ref[1:2, sl]
        acc = acc + ext_scr[5:5 + c, sl] * cw_ref[0:1, sl]
        return _silu(acc)

    hs = range(heads)
    q, k, v, beta, gc, egc, g_last, decay = [], [], [], [], [], [], [], []
    for h in hs:
        qh = conv(h * dk)
        kh = conv(hk + h * dk)
        vh = conv(2 * hk + h * dk)
        qh = qh * lax.rsqrt(jnp.sum(qh * qh, axis=-1, keepdims=True) + EPS) * (dk ** -0.5)
        kh = kh * lax.rsqrt(jnp.sum(kh * kh, axis=-1, keepdims=True) + EPS)
        if rowmask is not None:
            qh, kh, vh = qh * rowmask, kh * rowmask, vh * rowmask
        q.append(qh)
        k.append(kh)
        v.append(vh)
        beta.append(beta_all[:, h:h + 1])
        gc.append(gc_all[:, heads + h:heads + h + 1])
        egc.append(jnp.exp(gc[h]))
        g_last.append(gc_all[c - 1:c, heads + h:heads + h + 1])
        gc_row = gc_t[heads + h:heads + h + 1, :]
        decay.append(jnp.where(incl, jnp.exp(jnp.where(incl, gc[h] - gc_row, 0.0)), 0.0))
    kb = [k[h] * beta[h] for h in hs]
    kk = [_nt_dot(kb[h], k[h]) for h in hs]
    qk = [_nt_dot(q[h], k[h]) for h in hs]
    apow = [jnp.where(strict, kk[h] * decay[h], 0.0) for h in hs]
    qk = [qk[h] * decay[h] for h in hs]
    tinv = [eye - apow[h] for h in hs]
    span = 2
    while span < c:
        apow = [_dot(apow[h], apow[h], precision=HIGHEST) for h in hs]
        tinv = [tinv[h] + _dot(tinv[h], apow[h], precision=HIGHEST) for h in hs]
        span *= 2
    u = [_dot(tinv[h], v[h] * beta[h]) for h in hs]
    w = [_dot(tinv[h], kb[h] * egc[h]) for h in hs]
    s = [s_scr[h] for h in hs]
    ws = [_dot(w[h], s[h]) for h in hs]
    qs = [_dot(q[h] * egc[h], s[h]) for h in hs]
    v_new = [u[h] - ws[h] for h in hs]
    o = [qs[h] + _dot(qk[h], v_new[h]) for h in hs]
    ds = [_tn_dot(k[h] * jnp.exp(g_last[h] - gc[h]), v_new[h]) for h in hs]
    for h in hs:
        s_scr[h] = s[h] * jnp.exp(g_last[h]) + ds[h]
        on = o[h] * lax.rsqrt(jnp.mean(o[h] * o[h], axis=-1, keepdims=True) + EPS) * dn_ref[...]
        zh = z_ref[:, h * dk:(h + 1) * dk]
        o_ref[:, h * dk:(h + 1) * dk] = (on * _silu(zh)).astype(o_ref.dtype)

    ext_scr[0:8, :] = ext_scr[c:c + 8, :]

    @pl.when(n == pl.num_programs(1) - 1)
    def _():
        sfin_ref[0] = s_scr[...]


def _delta_branch(proj, ba, row0, nb, seq, tail0, s0, conv_w, alog_row, dtb_row, delta_norm, *,
                  chunk, valid, heads, dk, z_col_block):
    hk = heads * dk
    nchunks = seq // chunk
    assert seq % chunk == 0 and row0 % chunk == 0
    r0 = row0 // chunk
    kern = functools.partial(_delta_kernel, chunk=chunk, valid=valid, heads=heads, dk=dk)
    return pl.pallas_call(
        kern,
        out_shape=(jax.ShapeDtypeStruct((nb * seq, hk), BF16),
                   jax.ShapeDtypeStruct((nb, heads, dk, dk), F32)),
        grid=(nb, nchunks),
        in_specs=[pl.BlockSpec((chunk, 3 * hk), lambda b, n: (r0 + b * nchunks + n, 0)),
                  pl.BlockSpec((chunk, hk), lambda b, n: (r0 + b * nchunks + n, z_col_block)),
                  pl.BlockSpec((chunk, LANES), lambda b, n: (r0 + b * nchunks + n, 0)),
                  pl.BlockSpec((1, 8, 3 * hk), lambda b, n: (b, 0, 0)),
                  pl.BlockSpec((1, heads, dk, dk), lambda b, n: (b, 0, 0, 0)),
                  pl.BlockSpec((4, 3 * hk), lambda b, n: (0, 0)),
                  pl.BlockSpec((1, LANES), lambda b, n: (0, 0)),
                  pl.BlockSpec((1, LANES), lambda b, n: (0, 0)),
                  pl.BlockSpec((1, dk), lambda b, n: (0, 0))],
        out_specs=(pl.BlockSpec((chunk, hk), lambda b, n: (b * nchunks + n, 0)),
                   pl.BlockSpec((1, heads, dk, dk), lambda b, n: (b, 0, 0, 0))),
        scratch_shapes=[pltpu.VMEM((8 + chunk, 3 * hk), F32),
                        pltpu.VMEM((heads, dk, dk), F32)],
        compiler_params=_cparams("parallel", "arbitrary"),
        name="delta_c%d" % chunk,
    )(proj, proj, ba, tail0, s0, conv_w, alog_row, dtb_row, delta_norm)


def _qkprep_kernel(q_ref, k_ref, v_ref, cos_ref, sin_ref, qg_ref, kg_ref,
                   qo_ref, kbo_ref, vo_ref, kfo_ref, *, comp, scale):
    tm = q_ref.shape[0]
    r_i = _iota((LANES, LANES), 0)
    c_i = _iota((LANES, LANES), 1)
    group = (r_i // comp == c_i // comp).astype(BF16)
    lane = _iota((tm, LANES), 1)
    first_half = (lane % comp) < (comp // 2)
    cos = cos_ref[...]
    sin = sin_ref[...]

    def norm_rope(x, gain):
        sq = x * x
        hi = sq.astype(BF16)
        lo = (sq - hi.astype(F32)).astype(BF16)
        ms = (_dot(hi, group) + _dot(lo, group)) * (1.0 / comp)
        y = x * lax.rsqrt(ms + EPS) * gain
        swapped = jnp.where(first_half, pltpu.roll(y, LANES - comp // 2, 1), pltpu.roll(y, comp // 2, 1))
        return y * cos + swapped * sin

    for j in range(q_ref.shape[1] // LANES):
        sl = slice(j * LANES, (j + 1) * LANES)
        qr = norm_rope(q_ref[:, sl], qg_ref[...])
        kr = norm_rope(k_ref[:, sl], kg_ref[...])
        qo_ref[:, sl] = (qr * scale).astype(BF16)
        kbo_ref[:, sl] = kr.astype(BF16)
        kfo_ref[:, sl] = kr
    vo_ref[...] = v_ref[...].astype(BF16)


def _qkprep(proj, cos_t, sin_t, q_gain, k_gain, *, q_col_block, width, comp):
    t = proj.shape[0]
    tm = _row_tile(t, 640)
    kern = functools.partial(_qkprep_kernel, comp=comp, scale=comp ** -0.5)
    col = lambda cb: pl.BlockSpec((tm, width), lambda i: (i, cb))
    small = pl.BlockSpec((tm, LANES), lambda i: (i, 0))
    gain = pl.BlockSpec((1, LANES), lambda i: (0, 0))
    out = pl.BlockSpec((tm, width), lambda i: (i, 0))
    return pl.pallas_call(
        kern,
        out_shape=(jax.ShapeDtypeStruct((t, width), BF16), jax.ShapeDtypeStruct((t, width), BF16),
                   jax.ShapeDtypeStruct((t, width), BF16), jax.ShapeDtypeStruct((t, width), F32)),
        grid=(t // tm,),
        in_specs=[col(q_col_block), col(q_col_block + 1), col(q_col_block + 2), small, small, gain, gain],
        out_specs=(out, out, out, out),
        compiler_params=_cparams("parallel"),
        name="qkprep",
    )(proj, proj, proj, cos_t, sin_t, q_gain, k_gain)


def _lambda_value(lam_ref, lam_init):
    lv = lam_ref[...]
    s1 = jnp.sum(lv[0:1] * lv[1:2], axis=-1, keepdims=True)
    s2 = jnp.sum(lv[2:3] * lv[3:4], axis=-1, keepdims=True)
    return jnp.exp(s1) - jnp.exp(s2) + lam_init


def _attn_prompt_kernel(q_ref, k_ref, v_ref, lam_ref, sub_ref, o_ref, *, tq, tk, comp, heads_per_step, lam_init):
    i = pl.program_id(2)
    dv = 2 * comp
    hr = range(heads_per_step)
    lane = _iota((tq, dv), 1)
    qs = []
    for hh in hr:
        q = q_ref[:, hh * dv:(hh + 1) * dv]
        zero = jnp.zeros_like(q)
        qs.append(jnp.concatenate([jnp.where(lane < comp, q, zero), jnp.where(lane >= comp, q, zero)], axis=0))

    def update(carry, start, mask):
        kt = [k_ref[pl.ds(start, tk), hh * dv:(hh + 1) * dv] for hh in hr]
        vt = [v_ref[pl.ds(start, tk), hh * dv:(hh + 1) * dv] for hh in hr]
        s = [_nt_dot(qs[hh], kt[hh]) for hh in hr]
        if mask is not None:
            s = [jnp.where(mask, sn, -jnp.inf) for sn in s]
        m_new = [jnp.maximum(carry[hh][0], jnp.max(s[hh], axis=-1, keepdims=True)) for hh in hr]
        p = [jnp.exp(s[hh] - m_new[hh]) for hh in hr]
        corr = [jnp.exp(carry[hh][0] - m_new[hh]) for hh in hr]
        l = [carry[hh][1] * corr[hh] + jnp.sum(p[hh], axis=-1, keepdims=True) for hh in hr]
        pv = [_dot(p[hh].astype(BF16), vt[hh]) for hh in hr]
        return tuple((m_new[hh], l[hh], carry[hh][2] * corr[hh] + pv[hh]) for hh in hr)

    def body(j, carry):
        return update(carry, pl.multiple_of(j * tk, tk), None)

    init = tuple((jnp.full((2 * tq, 1), -jnp.inf, F32), jnp.zeros((2 * tq, 1), F32),
                  jnp.zeros((2 * tq, dv), F32)) for _ in hr)
    n_full = (i * tq) // tk
    carry = lax.fori_loop(0, n_full, body, init)
    qpos = i * tq + _iota((2 * tq, tk), 0) % tq
    kpos = n_full * tk + _iota((2 * tq, tk), 1)
    carry = update(carry, pl.multiple_of(n_full * tk, tk), qpos >= kpos)
    lam = _lambda_value(lam_ref, lam_init)
    for hh in hr:
        _, l, a = carry[hh]
        n = a / l
        o = n[0:tq] - lam * n[tq:2 * tq]
        on = o * lax.rsqrt(jnp.mean(o * o, axis=-1, keepdims=True) + EPS) * sub_ref[...]
        o_ref[:, hh * dv:(hh + 1) * dv] = (on * (1.0 - lam_init)).astype(o_ref.dtype)


def _attn_prompt(q, k, v, lam_vecs, subln, *, nb, seq, heads, comp, lam_init):
    dv = 2 * comp
    tq = _row_tile(seq, 256)
    tk = _row_tile(seq, 256)
    assert tk % tq == 0
    nq = seq // tq
    hps = 2
    assert heads % hps == 0
    kern = functools.partial(_attn_prompt_kernel, tq=tq, tk=tk, comp=comp, heads_per_step=hps, lam_init=lam_init)
    return pl.pallas_call(
        kern,
        out_shape=jax.ShapeDtypeStruct((nb * seq, heads * dv), BF16),
        grid=(nb, heads // hps, nq),
        in_specs=[pl.BlockSpec((tq, hps * dv), lambda b, h, i: (b * nq + i, h)),
                  pl.BlockSpec((seq, hps * dv), lambda b, h, i: (b, h)),
                  pl.BlockSpec((seq, hps * dv), lambda b, h, i: (b, h)),
                  pl.BlockSpec((4, comp), lambda b, h, i: (0, 0)),
                  pl.BlockSpec((1, dv), lambda b, h, i: (0, 0))],
        out_specs=pl.BlockSpec((tq, hps * dv), lambda b, h, i: (b * nq + i, h)),
        compiler_params=_cparams("parallel", "parallel", "arbitrary"),
        name="attn_prompt",
    )(q, k, v, lam_vecs, subln)


def _attn_sample_kernel(pt_ref, q_ref, kn_ref, vn_ref, lam_ref, sub_ref, *rest,
                        pages_per_step, new_len, heads, comp, lam_init):
    pp = pages_per_step
    kt_refs = rest[:pp]
    v_refs = rest[pp:2 * pp]
    o_ref = rest[2 * pp]
    qbd_scr, m_scr, l_scr, acc_scr = rest[2 * pp + 1:]
    p_id = pl.program_id(1)
    rph = 2 * new_len
    rows = heads * rph
    dv = 2 * comp
    width = heads * dv
    page = kt_refs[0].shape[2]

    @pl.when(p_id == 0)
    def _():
        q = q_ref[0][0:new_len, :]
        lane_grp = _iota((new_len, width), 1) // comp
        for h in range(heads):
            for c in range(2):
                r0 = h * rph + c * new_len
                qbd_scr[r0:r0 + new_len, :] = jnp.where(lane_grp == 2 * h + c, q, 0.0)
        m_scr[...] = jnp.full(m_scr.shape, -jnp.inf, F32)
        l_scr[...] = jnp.zeros(l_scr.shape, F32)
        acc_scr[...] = jnp.zeros(acc_scr.shape, F32)

    qbd = qbd_scr[...]

    def online(s_list, v_of):
        m = m_scr[...]
        m_new = m
        for s in s_list:
            m_new = jnp.maximum(m_new, jnp.max(s, axis=-1, keepdims=True))
        corr = jnp.exp(m - m_new)
        p_list = [jnp.exp(s - m_new) for s in s_list]
        l = l_scr[...] * corr
        for p in p_list:
            l = l + jnp.sum(p, axis=-1, keepdims=True)
        m_scr[...] = m_new
        l_scr[...] = l
        for h in range(heads):
            hs = slice(h * rph, (h + 1) * rph)
            acc = acc_scr[hs, :] * corr[hs, :]
            for j, p in enumerate(p_list):
                acc = acc + _dot(p[hs, :], v_of(j, h))
            acc_scr[hs, :] = acc

    online([_dot(qbd, kt_refs[r][0]) for r in range(pp)],
           lambda j, h: v_refs[j][0, pl.ds(h, page, stride=heads), :])

    @pl.when(p_id == pl.num_programs(1) - 1)
    def _():
        s = _nt_dot(qbd, kn_ref[0])
        tok = _iota((rows, 8), 0) % new_len
        key = _iota((rows, 8), 1)
        s = jnp.where((tok >= key) & (key < new_len), s, -jnp.inf)
        online([s], lambda j, h: vn_ref[0][:, h * dv:(h + 1) * dv])
        lam = _lambda_value(lam_ref, lam_init)
        nrm = acc_scr[...] / l_scr[...]
        for h in range(heads):
            r0 = h * rph
            oh = nrm[r0:r0 + new_len, :] - lam * nrm[r0 + new_len:r0 + rph, :]
            on = oh * lax.rsqrt(jnp.mean(oh * oh, axis=-1, keepdims=True) + EPS) * sub_ref[...]
            o_ref[0, 0:new_len, h * dv:(h + 1) * dv] = (on * (1.0 - lam_init)).astype(o_ref.dtype)
        if new_len < 8:
            o_ref[0, new_len:8, :] = jnp.zeros((8 - new_len, width), o_ref.dtype)


def _attn_sample(q8, k8, v8, cache_kt, cache_v2, page_table, lam_vecs, subln, *, new_len, heads, comp, lam_init):
    nb, n_pages = page_table.shape
    width, page = cache_kt.shape[1], cache_kt.shape[2]
    dv = 2 * comp
    pp = 8
    assert n_pages % pp == 0 and 2 * new_len == 8
    rows = heads * 2 * new_len
    kern = functools.partial(_attn_sample_kernel, pages_per_step=pp, new_len=new_len, heads=heads,
                             comp=comp, lam_init=lam_init)
    per_req = pl.BlockSpec((1, 8, width), lambda b, p, pt: (b, 0, 0))

    def page_spec(r, shape):
        return pl.BlockSpec((1,) + shape, lambda b, p, pt: (pt[b * n_pages + p * pp + r], 0, 0))

    grid_spec = pltpu.PrefetchScalarGridSpec(
        num_scalar_prefetch=1,
        grid=(nb, n_pages // pp),
        in_specs=[per_req, per_req, per_req,
                  pl.BlockSpec((4, comp), lambda b, p, pt: (0, 0)),
                  pl.BlockSpec((1, dv), lambda b, p, pt: (0, 0))]
                 + [page_spec(r, (width, page)) for r in range(pp)]
                 + [page_spec(r, (page * heads, dv)) for r in range(pp)],
        out_specs=pl.BlockSpec((1, 8, width), lambda b, p, pt: (b, 0, 0)),
        scratch_shapes=[pltpu.VMEM((rows, width), F32), pltpu.VMEM((rows, 1), F32),
                        pltpu.VMEM((rows, 1), F32), pltpu.VMEM((rows, dv), F32)],
    )
    return pl.pallas_call(
        kern,
        out_shape=jax.ShapeDtypeStruct((nb, 8, width), BF16),
        grid_spec=grid_spec,
        compiler_params=_cparams("parallel", "arbitrary"),
        name="attn_sample",
    )(page_table.reshape(-1), q8, k8, v8, lam_vecs, subln, *([cache_kt] * pp), *([cache_v2] * pp))


def _merge_kernel(oa_ref, ob_ref, ga_ref, gb_ref, woa_ref, wob_ref, m_ref):
    ua = _dot(oa_ref[...], woa_ref[...])
    ub = _dot(ob_ref[...], wob_ref[...])
    m_ref[...] = (_sigmoid(ga_ref[...]) * ua + _sigmoid(gb_ref[...]) * ub).astype(m_ref.dtype)


def _merge(o_a, o_b, proj, w_oa, w_ob, *, gate_col_block):
    t, zd = o_a.shape
    d = w_oa.shape[1]
    tm = _row_tile(t, 320)
    return pl.pallas_call(
        _merge_kernel,
        out_shape=jax.ShapeDtypeStruct((t, d), BF16),
        grid=(t // tm,),
        in_specs=[pl.BlockSpec((tm, zd), lambda i: (i, 0)),
                  pl.BlockSpec((tm, o_b.shape[1]), lambda i: (i, 0)),
                  pl.BlockSpec((tm, d), lambda i: (i, gate_col_block)),
                  pl.BlockSpec((tm, d), lambda i: (i, gate_col_block + 1)),
                  pl.BlockSpec(w_oa.shape, lambda i: (0, 0)),
                  pl.BlockSpec(w_ob.shape, lambda i: (0, 0))],
        out_specs=pl.BlockSpec((tm, d), lambda i: (i, 0)),
        compiler_params=_cparams("parallel"),
        name="merge",
    )(o_a, o_b, proj, proj, w_oa, w_ob)


def _outproj_route_kernel(x_ref, m_ref, wo_ref, g_ref, wr_ref, br_ref,
                          x1_ref, h2_ref, rw_ref, re_ref, *, n_groups, per_group):
    x1 = x_ref[...] + _dot(m_ref[...], wo_ref[...])
    x1_ref[...] = x1
    h2 = x1 * lax.rsqrt(jnp.mean(x1 * x1, axis=-1, keepdims=True) + EPS) * g_ref[...]
    h2_ref[...] = h2
    logits = _dot(h2.astype(BF16), wr_ref[...]) + br_ref[...]
    tm = logits.shape[0]
    lane = _iota((tm, LANES), 1).astype(F32)
    big = float(LANES)

    def masked_softmax(mask):
        lm = jnp.where(mask, logits, -jnp.inf)
        e = jnp.exp(lm - jnp.max(lm, axis=-1, keepdims=True))
        return e / jnp.sum(e, axis=-1, keepdims=True)

    def top1(p, mask):
        v = jnp.max(jnp.where(mask, p, -1.0), axis=-1, keepdims=True)
        idx = jnp.min(jnp.where(mask & (p == v), lane, big), axis=-1, keepdims=True)
        return v, idx

    gmask = lane < n_groups
    pg = masked_softmax(gmask)
    g_w, g_top = top1(pg, gmask)
    lo = n_groups + g_top * per_group
    emask = (lane >= lo) & (lane < lo + per_group)
    pe = masked_softmax(emask)
    v1, i1 = top1(pe, emask)
    emask2 = emask & (lane != i1)
    v2, i2 = top1(pe, emask2)
    denom = v1 + v2
    w1 = g_w * v1 / denom
    w2 = g_w * v2 / denom
    rw_ref[...] = jnp.where(lane == 0.0, w1, jnp.where(lane == 1.0, w2, 0.0))
    re_ref[...] = jnp.where(lane == 0.0, i1 - n_groups,
                            jnp.where(lane == 1.0, i2 - n_groups, 0.0)).astype(jnp.int32)


def _outproj_route(x, merged, w_out, gain, w_route, b_route, *, n_groups, per_group):
    t, d = x.shape
    tm = _row_tile(t, 320)
    kern = functools.partial(_outproj_route_kernel, n_groups=n_groups, per_group=per_group)
    row = lambda w: pl.BlockSpec((tm, w), lambda i: (i, 0))
    return pl.pallas_call(
        kern,
        out_shape=(jax.ShapeDtypeStruct((t, d), F32), jax.ShapeDtypeStruct((t, d), F32),
                   jax.ShapeDtypeStruct((t, LANES), F32), jax.ShapeDtypeStruct((t, LANES), jnp.int32)),
        grid=(t // tm,),
        in_specs=[row(d), row(d),
                  pl.BlockSpec(w_out.shape, lambda i: (0, 0)),
                  pl.BlockSpec((1, d), lambda i: (0, 0)),
                  pl.BlockSpec((d, LANES), lambda i: (0, 0)),
                  pl.BlockSpec((1, LANES), lambda i: (0, 0))],
        out_specs=(row(d), row(d), row(LANES), row(LANES)),
        compiler_params=_cparams("parallel"),
        name="outproj_route",
    )(x, merged, w_out, gain, w_route, b_route)


def _experts_kernel(be_ref, nused_ref, src_ref, src_next_ref, dst_ref, h_hbm, wg_ref, wu_ref, wd_ref, y_hbm,
                    xbuf, ybuf, gsem, ssem, *, n_slots):
    g = pl.program_id(0)
    nused = nused_ref[0]
    blk = MOE_BLOCK

    def gather_copy(idx_ref, slot, r):
        tok = idx_ref[0, 0, r]
        return pltpu.make_async_copy(h_hbm.at[pl.ds(tok, 1), :], xbuf.at[slot, pl.ds(r, 1), :], gsem.at[slot])

    def start_gather(idx_ref, slot):
        def body(r, carry):
            gather_copy(idx_ref, slot, r).start()
            return carry
        lax.fori_loop(0, blk, body, 0)

    def wait_gather(idx_ref, slot):
        def body(r, carry):
            gather_copy(idx_ref, slot, r).wait()
            return carry
        lax.fori_loop(0, blk, body, 0)

    def scatter_copy(r):
        dst = dst_ref[0, 0, r]
        return dst, pltpu.make_async_copy(ybuf.at[pl.ds(r, 1), :], y_hbm.at[pl.ds(dst, 1), :], ssem.at[0])

    @pl.when(g < nused)
    def _():
        slot = g % 2

        @pl.when(g == 0)
        def _():
            start_gather(src_ref, slot)

        @pl.when(g + 1 < nused)
        def _():
            start_gather(src_next_ref, 1 - slot)

        wait_gather(src_ref, slot)
        x = xbuf[slot]
        hdn = _silu(_dot(x, wg_ref[0])) * _dot(x, wu_ref[0])
        ybuf[...] = _dot(hdn, wd_ref[0])

        def sbody(r, carry):
            dst, cp = scatter_copy(r)

            @pl.when(dst < n_slots)
            def _():
                cp.start()
            return carry
        lax.fori_loop(0, blk, sbody, 0)

        def wbody(r, carry):
            dst, cp = scatter_copy(r)

            @pl.when(dst < n_slots)
            def _():
                cp.wait()
            return carry
        lax.fori_loop(0, blk, wbody, 0)


def _experts(h2, block_e, src_tok, dst_slot, nused, w_gate, w_up, w_down):
    t, d = h2.shape
    n_blocks = block_e.shape[0]
    n_slots = t * TOP_K
    ff = w_gate.shape[2]
    kern = functools.partial(_experts_kernel, n_slots=n_slots)
    src3 = src_tok.reshape(n_blocks, 1, MOE_BLOCK)
    dst3 = dst_slot.reshape(n_blocks, 1, MOE_BLOCK)
    idx_spec = lambda f: pl.BlockSpec((1, 1, MOE_BLOCK), lambda g, be, nu: (f(g), 0, 0), memory_space=pltpu.SMEM)
    grid_spec = pltpu.PrefetchScalarGridSpec(
        num_scalar_prefetch=2,
        grid=(n_blocks,),
        in_specs=[idx_spec(lambda g: g),
                  idx_spec(lambda g: jnp.minimum(g + 1, n_blocks - 1)),
                  idx_spec(lambda g: g),
                  pl.BlockSpec(memory_space=pl.ANY),
                  pl.BlockSpec((1, d, ff), lambda g, be, nu: (be[g], 0, 0)),
                  pl.BlockSpec((1, d, ff), lambda g, be, nu: (be[g], 0, 0)),
                  pl.BlockSpec((1, ff, d), lambda g, be, nu: (be[g], 0, 0))],
        out_specs=pl.BlockSpec(memory_space=pl.ANY),
        scratch_shapes=[pltpu.VMEM((2, MOE_BLOCK, d), F32), pltpu.VMEM((MOE_BLOCK, d), F32),
                        pltpu.SemaphoreType.DMA((2,)), pltpu.SemaphoreType.DMA((1,))],
    )
    return pl.pallas_call(
        kern,
        out_shape=jax.ShapeDtypeStruct((n_slots, d), F32),
        grid_spec=grid_spec,
        compiler_params=_cparams("arbitrary"),
        name="experts",
    )(block_e, nused, src3, src3, dst3, h2, w_gate, w_up, w_down)


def _expert_plan(eidx, n_experts):
    t = eidx.shape[0]
    s = t * TOP_K
    flat_e = eidx.reshape(-1)
    order = jnp.argsort(flat_e).astype(jnp.int32)
    sorted_e = flat_e[order]
    counts = jnp.zeros((n_experts,), jnp.int32).at[flat_e].add(1)
    padded = (counts + MOE_BLOCK - 1) // MOE_BLOCK * MOE_BLOCK
    ends = jnp.cumsum(padded)
    pad_start = ends - padded
    start = jnp.cumsum(counts) - counts
    dest = pad_start[sorted_e] + jnp.arange(s, dtype=jnp.int32) - start[sorted_e]
    n_blocks = -(-(s + n_experts * (MOE_BLOCK - 1)) // MOE_BLOCK)
    rows = n_blocks * MOE_BLOCK
    src_tok = jnp.zeros((rows,), jnp.int32).at[dest].set(order // TOP_K)
    dst_slot = jnp.full((rows,), s, jnp.int32).at[dest].set((order % TOP_K) * t + order // TOP_K)
    block_e = jnp.minimum(jnp.searchsorted(ends, jnp.arange(n_blocks, dtype=jnp.int32) * MOE_BLOCK, side='right'),
                          n_experts - 1).astype(jnp.int32)
    nused = (ends[-1:] // MOE_BLOCK).astype(jnp.int32)
    return block_e, src_tok, dst_slot, nused


def _combine_ple_kernel(x1_ref, ys0_ref, ys1_ref, rw_ref, p_ref, wp_ref, g_ref, wpg_ref, y_ref):
    rw = rw_ref[...]
    moe = rw[:, 0:1] * ys0_ref[...] + rw[:, 1:2] * ys1_ref[...]
    x2 = x1_ref[...] + moe
    hn = x2 * lax.rsqrt(jnp.mean(x2 * x2, axis=-1, keepdims=True) + EPS) * g_ref[...]
    gate = _sigmoid(_dot(hn.astype(BF16), wpg_ref[...]))
    y_ref[...] = x2 + _dot(p_ref[...], wp_ref[...]) * gate


def _combine_ple(x1, y_slots, route_w, p, w_ple, gain, w_ple_gate):
    t, d = x1.shape
    tm = _row_tile(t, 320)
    nt = t // tm
    row = lambda w: pl.BlockSpec((tm, w), lambda i: (i, 0))
    return pl.pallas_call(
        _combine_ple_kernel,
        out_shape=jax.ShapeDtypeStruct((t, d), F32),
        grid=(nt,),
        in_specs=[row(d), row(d), pl.BlockSpec((tm, d), lambda i: (nt + i, 0)), row(LANES), row(p.shape[1]),
                  pl.BlockSpec(w_ple.shape, lambda i: (0, 0)),
                  pl.BlockSpec((1, d), lambda i: (0, 0)),
                  pl.BlockSpec(w_ple_gate.shape, lambda i: (0, 0))],
        out_specs=row(d),
        compiler_params=_cparams("parallel"),
        name="combine_ple",
    )(x1, y_slots, y_slots, route_w, p, w_ple, gain, w_ple_gate)


def _pad_rows(a, rows):
    return jnp.pad(a, ((0, 0), (0, rows - a.shape[1]), (0, 0)))


def _layer(x_all, p_all, nb_p, seq_p, nb_s, seq_s, past_len, state_conv, state_delta, cache_k, cache_v,
           page_table, lam_init, lw):
    tp = nb_p * seq_p
    ts = nb_s * seq_s
    d = x_all.shape[1]
    heads_d = lw['a_log'].shape[0]
    dk = lw['delta_norm'].shape[0]
    hk = heads_d * dk
    conv_dim = 3 * hk
    comp = lw['q_norm'].shape[0]
    dv = 2 * comp
    width = lw['w_ob'].shape[0]
    heads_a = width // dv
    n_groups = lw['w_rg'].shape[1]
    n_experts = lw['w_re'].shape[1]
    assert seq_s <= 8 and conv_dim % 1024 == 0 and hk == 1024 and width == 1024 and d % 1024 == 0
    assert 2 * heads_d <= LANES and n_groups + n_experts <= LANES and lw['conv_w'].shape[0] == 4

    w_in = lw['w_in']
    ba0 = conv_dim + hk
    q0 = ba0 + 2 * heads_d
    gate0 = q0 + 3 * width
    w_main = jnp.concatenate([w_in[:, :ba0], w_in[:, gate0:], w_in[:, q0:gate0]], axis=1).astype(BF16)
    w_ba = jnp.pad(w_in[:, ba0:q0], ((0, 0), (0, LANES - 2 * heads_d))).astype(BF16)
    z_cb = conv_dim // hk
    gate_cb = (conv_dim + hk) // d
    q_cb = (conv_dim + hk + 2 * d) // width
    assert (conv_dim + hk) % d == 0 and (conv_dim + hk + 2 * d) % width == 0

    proj, ba = _inproj(x_all, lw['norm_attn'].reshape(1, d), w_main, w_ba)

    alog_row = jnp.pad(lw['a_log'], (heads_d, LANES - 2 * heads_d)).reshape(1, LANES)
    dtb_row = jnp.pad(lw['dt_bias'], (heads_d, LANES - 2 * heads_d)).reshape(1, LANES)
    dn = lw['delta_norm'].reshape(1, dk)
    chunk_p = math.gcd(seq_p, DELTA_CHUNK)
    oa_p, s_p = _delta_branch(proj, ba, 0, nb_p, seq_p,
                              jnp.zeros((nb_p, 8, conv_dim), F32), jnp.zeros((nb_p, heads_d, dk, dk), F32),
                              lw['conv_w'], alog_row, dtb_row, dn,
                              chunk=chunk_p, valid=chunk_p, heads=heads_d, dk=dk, z_col_block=z_cb)
    qkv_s = proj[tp:, :conv_dim].reshape(nb_s, seq_s, conv_dim)
    proj_s8 = _pad_rows(proj[tp:].reshape(nb_s, seq_s, -1), 8).reshape(nb_s * 8, -1)
    ba_s8 = _pad_rows(ba[tp:].reshape(nb_s, seq_s, LANES), 8).reshape(nb_s * 8, LANES)
    tail_s = jnp.pad(state_conv, ((0, 0), (8 - state_conv.shape[1], 0), (0, 0)))
    oa_s8, s_s = _delta_branch(proj_s8, ba_s8, 0, nb_s, 8, tail_s, state_delta,
                               lw['conv_w'], alog_row, dtb_row, dn,
                               chunk=8, valid=seq_s, heads=heads_d, dk=dk, z_col_block=z_cb)
    oa_s = oa_s8.reshape(nb_s, 8, hk)[:, :seq_s].reshape(ts, hk)
    o_a = jnp.concatenate([oa_p, oa_s], axis=0)
    conv_p = jnp.stack([proj[(b + 1) * seq_p - 3:(b + 1) * seq_p, :conv_dim] for b in range(nb_p)])
    conv_s = jnp.concatenate([state_conv, qkv_s], axis=1)[:, -3:, :]

    half = comp // 2
    inv_freq = ROPE_THETA ** (-jnp.arange(half, dtype=F32) / half)
    pos = jnp.concatenate([jnp.tile(jnp.arange(seq_p, dtype=jnp.int32), nb_p),
                           jnp.tile(past_len + jnp.arange(seq_s, dtype=jnp.int32), nb_s)]).astype(F32)
    ang = pos[:, None] * inv_freq[None, :]
    cos_t = jnp.tile(jnp.cos(ang), (1, LANES // half))
    sin_h = jnp.sin(ang)
    sin_t = jnp.tile(jnp.concatenate([-sin_h, sin_h], axis=1), (1, LANES // comp))
    q_bf, k_bf, v_bf, k_f = _qkprep(proj, cos_t, sin_t,
                                    jnp.tile(lw['q_norm'], LANES // comp).reshape(1, LANES),
                                    jnp.tile(lw['k_norm'], LANES // comp).reshape(1, LANES),
                                    q_col_block=q_cb, width=width, comp=comp)
    lam_vecs = jnp.stack([lw['lam_q1'], lw['lam_k1'], lw['lam_q2'], lw['lam_k2']])
    subln = lw['subln'].reshape(1, dv)
    ob_p = _attn_prompt(q_bf, k_bf, v_bf, lam_vecs, subln, nb=nb_p, seq=seq_p, heads=heads_a, comp=comp,
                        lam_init=lam_init)
    v_f_s = proj[tp:, (q_cb + 2) * width:(q_cb + 3) * width]
    q8 = _pad_rows(q_bf[tp:].astype(F32).reshape(nb_s, seq_s, width), 8)
    k8 = _pad_rows(k_f[tp:].reshape(nb_s, seq_s, width), 8)
    v8 = _pad_rows(v_f_s.reshape(nb_s, seq_s, width), 8)
    n_phys, page = cache_k.shape[0], cache_k.shape[1]
    cache_kt = jnp.transpose(cache_k, (0, 2, 3, 4, 1)).reshape(n_phys, width, page)
    cache_v2 = cache_v.reshape(n_phys, page * heads_a, dv)
    ob_s8 = _attn_sample(q8, k8, v8, cache_kt, cache_v2, page_table, lam_vecs, subln,
                         new_len=seq_s, heads=heads_a, comp=comp, lam_init=lam_init)
    o_b = jnp.concatenate([ob_p, ob_s8[:, :seq_s].reshape(ts, width)], axis=0)

    merged = _merge(o_a, o_b, proj, lw['w_oa'].astype(BF16), lw['w_ob'].astype(BF16), gate_col_block=gate_cb)
    w_route = jnp.pad(jnp.concatenate([lw['w_rg'], lw['w_re']], axis=1),
                      ((0, 0), (0, LANES - n_groups - n_experts))).astype(BF16)
    b_route = jnp.pad(jnp.concatenate([lw['b_rg'], lw['b_re']]), (0, LANES - n_groups - n_experts)).reshape(1, LANES)
    x1, h2, route_w, route_e = _outproj_route(x_all, merged, lw['w_out'].astype(BF16),
                                              lw['norm_ffn'].reshape(1, d), w_route, b_route,
                                              n_groups=n_groups, per_group=n_experts // n_groups)

    block_e, src_tok, dst_slot, nused = _expert_plan(route_e[:, :TOP_K], n_experts)
    y_slots = _experts(h2, block_e, src_tok, dst_slot, nused, lw['exp_gate'], lw['exp_up'], lw['exp_down'])

    y = _combine_ple(x1, y_slots, route_w, p_all.astype(BF16),
                     lw['w_ple'].astype(BF16), lw['norm_ple'].reshape(1, d), lw['w_ple_gate'].astype(BF16))

    k_p = k_f[:tp].reshape(nb_p, seq_p, heads_a, 2, comp)
    v_p = proj[:tp, (q_cb + 2) * width:(q_cb + 3) * width].reshape(nb_p, seq_p, heads_a, dv)
    k_s = k_f[tp:].reshape(nb_s, seq_s, heads_a, 2, comp)
    v_s = v_f_s.reshape(nb_s, seq_s, heads_a, dv)
    return y, k_p, v_p, k_s, v_s, conv_p, conv_s, s_p, s_s


def kernel(x_prompt, x_sample, cache_k, cache_v, state_conv, state_delta, page_table, p_prompt, p_sample,
           norm_attn, w_in, conv_w, a_log, dt_bias, delta_norm, q_norm, k_norm, lam_q1, lam_k1, lam_q2, lam_k2,
           subln, w_oa, w_ob, w_out, norm_ffn, w_rg, b_rg, w_re, b_re, exp_gate, exp_up, exp_down,
           norm_ple, w_ple, w_ple_gate):
    nb_p, seq_p, d = x_prompt.shape
    nb_s, seq_s, _ = x_sample.shape
    tp = nb_p * seq_p
    depth = w_in.shape[0]
    past_len = page_table.shape[1] * cache_k.shape[2]
    x_all = jnp.concatenate([x_prompt.reshape(tp, d), x_sample.reshape(nb_s * seq_s, d)], axis=0)
    outs = [[] for _ in range(8)]
    for i in range(depth):
        lw = dict(norm_attn=norm_attn[i], w_in=w_in[i], conv_w=conv_w[i], a_log=a_log[i], dt_bias=dt_bias[i],
                  delta_norm=delta_norm[i], q_norm=q_norm[i], k_norm=k_norm[i], lam_q1=lam_q1[i], lam_k1=lam_k1[i],
                  lam_q2=lam_q2[i], lam_k2=lam_k2[i], subln=subln[i], w_oa=w_oa[i], w_ob=w_ob[i], w_out=w_out[i],
                  norm_ffn=norm_ffn[i], w_rg=w_rg[i], b_rg=b_rg[i], w_re=w_re[i], b_re=b_re[i],
                  exp_gate=exp_gate[i], exp_up=exp_up[i], exp_down=exp_down[i],
                  norm_ple=norm_ple[i], w_ple=w_ple[i], w_ple_gate=w_ple_gate[i])
        lam_init = 0.8 - 0.6 * math.exp(-0.3 * i)
        p_all = jnp.concatenate([p_prompt[i].reshape(tp, -1), p_sample[i].reshape(nb_s * seq_s, -1)], axis=0)
        res = _layer(x_all, p_all, nb_p, seq_p, nb_s, seq_s, past_len, state_conv[i], state_delta[i],
                     cache_k[i], cache_v[i], page_table, lam_init, lw)
        x_all = res[0]
        for lst, val in zip(outs, res[1:]):
            lst.append(val)
    y_p = x_all[:tp].reshape(nb_p, seq_p, d)
    y_s = x_all[tp:].reshape(nb_s, seq_s, d)
    return (y_p, y_s) + tuple(jnp.stack(lst) for lst in outs)
```

```python
import functools
import math

import jax
import jax.numpy as jnp
from jax import lax
from jax.experimental import pallas as pl
from jax.experimental.pallas import tpu as pltpu

F32 = jnp.float32
BF16 = jnp.bfloat16
HIGHEST = lax.Precision.HIGHEST

LANES = 128
DELTA_CHUNK = 64
ROPE_THETA = 10000.0
TOP_K = 2
MOE_BLOCK = 128
EPS = 1e-6
VMEM_LIMIT = 48 * 1024 * 1024


def _cparams(*sem):
    return pltpu.CompilerParams(dimension_semantics=sem, vmem_limit_bytes=VMEM_LIMIT)


def _row_tile(n, target, align=16):
    best = None
    for t in range(align, min(n, target) + 1, align):
        if n % t == 0:
            best = t
    assert best is not None, (n, target, align)
    return best


def _nt_dot(a, b, precision=None):
    return lax.dot_general(a, b, (((1,), (1,)), ((), ())), precision=precision,
                           preferred_element_type=F32)


def _tn_dot(a, b, precision=None):
    return lax.dot_general(a, b, (((0,), (0,)), ((), ())), precision=precision,
                           preferred_element_type=F32)


def _dot(a, b, precision=None):
    return jnp.dot(a, b, precision=precision, preferred_element_type=F32)


def _split_bf16(a):
    hi = a.astype(BF16)
    return hi, (a - hi.astype(F32)).astype(BF16)


def _dot_split(a, b):
    return _dot(a[0], b[0]) + (_dot(a[0], b[1]) + _dot(a[1], b[0]))


def _sigmoid(x):
    return 1.0 / (1.0 + jnp.exp(-x))


def _silu(x):
    return x * _sigmoid(x)


def _iota(shape, dim):
    return lax.broadcasted_iota(jnp.int32, shape, dim)


def _inproj_kernel(x_ref, g_ref, w_ref, wba_ref, p_ref, ba_ref, h_scr):
    @pl.when(pl.program_id(1) == 0)
    def _():
        x = x_ref[...]
        h = x * lax.rsqrt(jnp.mean(x * x, axis=-1, keepdims=True) + EPS) * g_ref[...]
        h_scr[...] = h.astype(BF16)
        ba_ref[...] = _dot(h_scr[...], wba_ref[...])

    p_ref[...] = _dot(h_scr[...], w_ref[...])


def _inproj(x, gain, w_main, w_ba):
    t, d = x.shape
    n = w_main.shape[1]
    tm = _row_tile(t, 640)
    tn = 1024
    assert n % tn == 0
    return pl.pallas_call(
        _inproj_kernel,
        out_shape=(jax.ShapeDtypeStruct((t, n), F32), jax.ShapeDtypeStruct((t, LANES), F32)),
        grid=(t // tm, n // tn),
        in_specs=[pl.BlockSpec((tm, d), lambda i, j: (i, 0)),
                  pl.BlockSpec((1, d), lambda i, j: (0, 0)),
                  pl.BlockSpec((d, tn), lambda i, j: (0, j)),
                  pl.BlockSpec((d, LANES), lambda i, j: (0, 0))],
        out_specs=(pl.BlockSpec((tm, tn), lambda i, j: (i, j)),
                   pl.BlockSpec((tm, LANES), lambda i, j: (i, 0))),
        scratch_shapes=[pltpu.VMEM((tm, d), BF16)],
        compiler_params=_cparams("parallel", "arbitrary"),
        name="inproj",
    )(x, gain, w_main, w_ba)


def _delta_kernel(qkv_ref, z_ref, ba_ref, tail0_ref, s0_ref, cw_ref, alog_ref, dtb_ref, dn_ref,
                  o_ref, sfin_ref, ext_scr, s_scr, *, chunk, valid, heads, dk):
    n = pl.program_id(1)
    c = chunk
    hk = heads * dk

    @pl.when(n == 0)
    def _():
        ext_scr[0:8, :] = tail0_ref[0]
        s_scr[...] = s0_ref[0]

    ext_scr[8:8 + c, :] = qkv_ref[...]

    row = _iota((c, 1), 0)
    rowmask = (row < valid).astype(F32) if valid < c else None

    ba = ba_ref[...]
    beta_all = _sigmoid(ba)
    xg = ba + dtb_ref[...]
    softplus = jnp.maximum(xg, 0.0) + jnp.log1p(jnp.exp(-jnp.abs(xg)))
    g_all = -jnp.exp(alog_ref[...]) * softplus
    if rowmask is not None:
        beta_all = beta_all * rowmask
        g_all = g_all * rowmask
    r_i = _iota((c, c), 0)
    c_i = _iota((c, c), 1)
    incl = r_i >= c_i
    strict = r_i > c_i
    gc_all = _dot(incl.astype(F32), g_all, precision=HIGHEST)
    gc_t = jnp.transpose(gc_all)
    eye = (r_i == c_i).astype(F32)

    def conv(col0):
        sl = slice(col0, col0 + dk)
        acc = ext_scr[8:8 + c, sl] * cw_ref[3:4, sl]
        acc = acc + ext_scr[7:7 + c, sl] * cw_ref[2:3, sl]
        acc = acc + ext_scr[6:6 + c, sl] * cw_ref[1:2, sl]
        acc = acc + ext_scr[5:5 + c, sl] * cw_ref[0:1, sl]
        return _silu(acc)

    hs = range(heads)
    q, k, v, beta, gc, egc, g_last, decay = [], [], [], [], [], [], [], []
    for h in hs:
        qh = conv(h * dk)
        kh = conv(hk + h * dk)
        vh = conv(2 * hk + h * dk)
        qh = qh * lax.rsqrt(jnp.sum(qh * qh, axis=-1, keepdims=True) + EPS) * (dk ** -0.5)
        kh = kh * lax.rsqrt(jnp.sum(kh * kh, axis=-1, keepdims=True) + EPS)
        if rowmask is not None:
            qh, kh, vh = qh * rowmask, kh * rowmask, vh * rowmask
        q.append(qh)
        k.append(kh)
        v.append(vh)
        beta.append(beta_all[:, h:h + 1])
        gc.append(gc_all[:, heads + h:heads + h + 1])
        egc.append(jnp.exp(gc[h]))
        g_last.append(gc_all[c - 1:c, heads + h:heads + h + 1])
        gc_row = gc_t[heads + h:heads + h + 1, :]
        decay.append(jnp.where(incl, jnp.exp(jnp.where(incl, gc[h] - gc_row, 0.0)), 0.0))
    kb = [k[h] * beta[h] for h in hs]
    kk = [_nt_dot(kb[h], k[h]) for h in hs]
    qk = [_nt_dot(q[h], k[h]) for h in hs]
    apow = [jnp.where(strict, kk[h] * decay[h], 0.0) for h in hs]
    qk = [qk[h] * decay[h] for h in hs]
    tinv = [eye - apow[h] for h in hs]
    span = 2
    asp = [_split_bf16(apow[h]) for h in hs]
    while span < c:
        asp = [_split_bf16(_dot_split(asp[h], asp[h])) for h in hs]
        tinv = [tinv[h] + _dot_split(_split_bf16(tinv[h]), asp[h]) for h in hs]
        span *= 2
    u = [_dot(tinv[h], v[h] * beta[h]) for h in hs]
    w = [_dot(tinv[h], kb[h] * egc[h]) for h in hs]
    s = [s_scr[h] for h in hs]
    ws = [_dot(w[h], s[h]) for h in hs]
    qs = [_dot(q[h] * egc[h], s[h]) for h in hs]
    v_new = [u[h] - ws[h] for h in hs]
    o = [qs[h] + _dot(qk[h], v_new[h]) for h in hs]
    ds = [_tn_dot(k[h] * jnp.exp(g_last[h] - gc[h]), v_new[h]) for h in hs]
    for h in hs:
        s_scr[h] = s[h] * jnp.exp(g_last[h]) + ds[h]
        on = o[h] * lax.rsqrt(jnp.mean(o[h] * o[h], axis=-1, keepdims=True) + EPS) * dn_ref[...]
        zh = z_ref[:, h * dk:(h + 1) * dk]
        o_ref[:, h * dk:(h + 1) * dk] = (on * _silu(zh)).astype(o_ref.dtype)

    ext_scr[0:8, :] = ext_scr[c:c + 8, :]

    @pl.when(n == pl.num_programs(1) - 1)
    def _():
        sfin_ref[0] = s_scr[...]


def _delta_branch(proj, ba, row0, nb, seq, tail0, s0, conv_w, alog_row, dtb_row, delta_norm, *,
                  chunk, valid, heads, dk, z_col_block):
    hk = heads * dk
    nchunks = seq // chunk
    assert seq % chunk == 0 and row0 % chunk == 0
    r0 = row0 // chunk
    kern = functools.partial(_delta_kernel, chunk=chunk, valid=valid, heads=heads, dk=dk)
    return pl.pallas_call(
        kern,
        out_shape=(jax.ShapeDtypeStruct((nb * seq, hk), BF16),
                   jax.ShapeDtypeStruct((nb, heads, dk, dk), F32)),
        grid=(nb, nchunks),
        in_specs=[pl.BlockSpec((chunk, 3 * hk), lambda b, n: (r0 + b * nchunks + n, 0)),
                  pl.BlockSpec((chunk, hk), lambda b, n: (r0 + b * nchunks + n, z_col_block)),
                  pl.BlockSpec((chunk, LANES), lambda b, n: (r0 + b * nchunks + n, 0)),
                  pl.BlockSpec((1, 8, 3 * hk), lambda b, n: (b, 0, 0)),
                  pl.BlockSpec((1, heads, dk, dk), lambda b, n: (b, 0, 0, 0)),
                  pl.BlockSpec((4, 3 * hk), lambda b, n: (0, 0)),
                  pl.BlockSpec((1, LANES), lambda b, n: (0, 0)),
                  pl.BlockSpec((1, LANES), lambda b, n: (0, 0)),
                  pl.BlockSpec((1, dk), lambda b, n: (0, 0))],
        out_specs=(pl.BlockSpec((chunk, hk), lambda b, n: (b * nchunks + n, 0)),
                   pl.BlockSpec((1, heads, dk, dk), lambda b, n: (b, 0, 0, 0))),
        scratch_shapes=[pltpu.VMEM((8 + chunk, 3 * hk), F32),
                        pltpu.VMEM((heads, dk, dk), F32)],
        compiler_params=_cparams("parallel", "arbitrary"),
        name="delta_c%d" % chunk,
    )(proj, proj, ba, tail0, s0, conv_w, alog_row, dtb_row, delta_norm)


def _qkprep_kernel(q_ref, k_ref, v_ref, cos_ref, sin_ref, qg_ref, kg_ref, *out_refs, comp, scale, for_prompt):
    tm = q_ref.shape[0]
    r_i = _iota((LANES, LANES), 0)
    c_i = _iota((LANES, LANES), 1)
    group = (r_i // comp == c_i // comp).astype(BF16)
    lane = _iota((tm, LANES), 1)
    first_half = (lane % comp) < (comp // 2)
    cos = cos_ref[...]
    sin = sin_ref[...]

    def norm_rope(x, gain):
        sq = x * x
        hi = sq.astype(BF16)
        lo = (sq - hi.astype(F32)).astype(BF16)
        ms = (_dot(hi, group) + _dot(lo, group)) * (1.0 / comp)
        y = x * lax.rsqrt(ms + EPS) * gain
        swapped = jnp.where(first_half, pltpu.roll(y, LANES - comp // 2, 1), pltpu.roll(y, comp // 2, 1))
        return y * cos + swapped * sin

    for j in range(q_ref.shape[1] // LANES):
        sl = slice(j * LANES, (j + 1) * LANES)
        qr = norm_rope(q_ref[:, sl], qg_ref[...]) * scale
        kr = norm_rope(k_ref[:, sl], kg_ref[...])
        if for_prompt:
            qo_ref, kbo_ref, vo_ref, kt_ref = out_refs
            qo_ref[:, sl] = qr.astype(BF16)
            kbo_ref[:, sl] = kr.astype(BF16)
            kt_ref[0, sl, :] = jnp.transpose(kr)
        else:
            qo_ref, kfo_ref = out_refs
            qo_ref[:, sl] = qr
            kfo_ref[:, sl] = kr
    if for_prompt:
        out_refs[2][...] = v_ref[...].astype(BF16)


def _qkprep(proj, row0, nb, seq, cos_t, sin_t, q_gain, k_gain, *, q_col_block, width, comp, for_prompt):
    tm = _row_tile(seq, 512, align=LANES) if for_prompt else seq
    assert row0 % tm == 0 and seq % tm == 0
    r0, per_seq = row0 // tm, seq // tm
    kern = functools.partial(_qkprep_kernel, comp=comp, scale=comp ** -0.5, for_prompt=for_prompt)
    col = lambda cb: pl.BlockSpec((tm, width), lambda i: (r0 + i, cb))
    small = pl.BlockSpec((tm, LANES), lambda i: (i % per_seq, 0))
    gain = pl.BlockSpec((1, LANES), lambda i: (0, 0))
    out = pl.BlockSpec((tm, width), lambda i: (i, 0))
    rows = nb * seq
    if for_prompt:
        out_shape = (jax.ShapeDtypeStruct((rows, width), BF16), jax.ShapeDtypeStruct((rows, width), BF16),
                     jax.ShapeDtypeStruct((rows, width), BF16), jax.ShapeDtypeStruct((nb, width, seq), F32))
        out_specs = (out, out, out, pl.BlockSpec((1, width, tm), lambda i: (i // per_seq, 0, i % per_seq)))
    else:
        out_shape = (jax.ShapeDtypeStruct((rows, width), F32), jax.ShapeDtypeStruct((rows, width), F32))
        out_specs = (out, out)
    return pl.pallas_call(
        kern,
        out_shape=out_shape,
        grid=(rows // tm,),
        in_specs=[col(q_col_block), col(q_col_block + 1), col(q_col_block + 2), small, small, gain, gain],
        out_specs=out_specs,
        compiler_params=_cparams("parallel"),
        name="qkprep_prompt" if for_prompt else "qkprep_sample",
    )(proj, proj, proj, cos_t, sin_t, q_gain, k_gain)


def _lambda_value(lam_ref, lam_init):
    lv = lam_ref[...]
    s1 = jnp.sum(lv[0:1] * lv[1:2], axis=-1, keepdims=True)
    s2 = jnp.sum(lv[2:3] * lv[3:4], axis=-1, keepdims=True)
    return jnp.exp(s1) - jnp.exp(s2) + lam_init


def _attn_prompt_kernel(q_ref, k_ref, v_ref, lam_ref, sub_ref, o_ref, *, tq, tk, comp, heads_per_step, lam_init):
    i = pl.program_id(2)
    dv = 2 * comp
    hr = range(heads_per_step)
    lane = _iota((tq, dv), 1)
    qs = []
    for hh in hr:
        q = q_ref[:, hh * dv:(hh + 1) * dv]
        zero = jnp.zeros_like(q)
        qs.append(jnp.concatenate([jnp.where(lane < comp, q, zero), jnp.where(lane >= comp, q, zero)], axis=0))

    def update(carry, start, mask):
        kt = [k_ref[pl.ds(start, tk), hh * dv:(hh + 1) * dv] for hh in hr]
        vt = [v_ref[pl.ds(start, tk), hh * dv:(hh + 1) * dv] for hh in hr]
        s = [_nt_dot(qs[hh], kt[hh]) for hh in hr]
        if mask is not None:
            s = [jnp.where(mask, sn, -jnp.inf) for sn in s]
        m_new = [jnp.maximum(carry[hh][0], jnp.max(s[hh], axis=-1, keepdims=True)) for hh in hr]
        p = [jnp.exp(s[hh] - m_new[hh]) for hh in hr]
        corr = [jnp.exp(carry[hh][0] - m_new[hh]) for hh in hr]
        l = [carry[hh][1] * corr[hh] + jnp.sum(p[hh], axis=-1, keepdims=True) for hh in hr]
        pv = [_dot(p[hh].astype(BF16), vt[hh]) for hh in hr]
        return tuple((m_new[hh], l[hh], carry[hh][2] * corr[hh] + pv[hh]) for hh in hr)

    def body(j, carry):
        return update(carry, pl.multiple_of(j * tk, tk), None)

    init = tuple((jnp.full((2 * tq, 1), -jnp.inf, F32), jnp.zeros((2 * tq, 1), F32),
                  jnp.zeros((2 * tq, dv), F32)) for _ in hr)
    n_full = (i * tq) // tk
    carry = lax.fori_loop(0, n_full, body, init)
    qpos = i * tq + _iota((2 * tq, tk), 0) % tq
    kpos = n_full * tk + _iota((2 * tq, tk), 1)
    carry = update(carry, pl.multiple_of(n_full * tk, tk), qpos >= kpos)
    lam = _lambda_value(lam_ref, lam_init)
    for hh in hr:
        _, l, a = carry[hh]
        n = a / l
        o = n[0:tq] - lam * n[tq:2 * tq]
        on = o * lax.rsqrt(jnp.mean(o * o, axis=-1, keepdims=True) + EPS) * sub_ref[...]
        o_ref[:, hh * dv:(hh + 1) * dv] = (on * (1.0 - lam_init)).astype(o_ref.dtype)


def _attn_prompt(q, k, v, lam_vecs, subln, *, nb, seq, heads, comp, lam_init):
    dv = 2 * comp
    tq = _row_tile(seq, 256)
    tk = _row_tile(seq, 256)
    assert tk % tq == 0
    nq = seq // tq
    hps = 2
    assert heads % hps == 0
    kern = functools.partial(_attn_prompt_kernel, tq=tq, tk=tk, comp=comp, heads_per_step=hps, lam_init=lam_init)
    return pl.pallas_call(
        kern,
        out_shape=jax.ShapeDtypeStruct((nb * seq, heads * dv), BF16),
        grid=(nb, heads // hps, nq),
        in_specs=[pl.BlockSpec((tq, hps * dv), lambda b, h, i: (b * nq + i, h)),
                  pl.BlockSpec((seq, hps * dv), lambda b, h, i: (b, h)),
                  pl.BlockSpec((seq, hps * dv), lambda b, h, i: (b, h)),
                  pl.BlockSpec((4, comp), lambda b, h, i: (0, 0)),
                  pl.BlockSpec((1, dv), lambda b, h, i: (0, 0))],
        out_specs=pl.BlockSpec((tq, hps * dv), lambda b, h, i: (b * nq + i, h)),
        compiler_params=_cparams("parallel", "parallel", "arbitrary"),
        name="attn_prompt",
    )(q, k, v, lam_vecs, subln)


def _attn_sample_kernel(pt_ref, q_ref, kn_ref, vn_ref, lam_ref, sub_ref, *rest,
                        pages_per_step, new_len, heads, comp, lam_init):
    pp = pages_per_step
    kt_refs = rest[:pp]
    v_refs = rest[pp:2 * pp]
    o_ref = rest[2 * pp]
    qbd_scr, m_scr, l_scr, acc_scr = rest[2 * pp + 1:]
    p_id = pl.program_id(1)
    rph = 2 * new_len
    rows = heads * rph
    dv = 2 * comp
    width = heads * dv
    page = kt_refs[0].shape[2]

    @pl.when(p_id == 0)
    def _():
        q = q_ref[0][0:new_len, :]
        lane_grp = _iota((new_len, width), 1) // comp
        for h in range(heads):
            for c in range(2):
                r0 = h * rph + c * new_len
                qbd_scr[r0:r0 + new_len, :] = jnp.where(lane_grp == 2 * h + c, q, 0.0)
        m_scr[...] = jnp.full(m_scr.shape, -jnp.inf, F32)
        l_scr[...] = jnp.zeros(l_scr.shape, F32)
        acc_scr[...] = jnp.zeros(acc_scr.shape, F32)

    qbd = qbd_scr[...]

    def online(s_list, v_of):
        m = m_scr[...]
        m_new = m
        for s in s_list:
            m_new = jnp.maximum(m_new, jnp.max(s, axis=-1, keepdims=True))
        corr = jnp.exp(m - m_new)
        p_list = [jnp.exp(s - m_new) for s in s_list]
        l = l_scr[...] * corr
        for p in p_list:
            l = l + jnp.sum(p, axis=-1, keepdims=True)
        m_scr[...] = m_new
        l_scr[...] = l
        for h in range(heads):
            hs = slice(h * rph, (h + 1) * rph)
            acc = acc_scr[hs, :] * corr[hs, :]
            for j, p in enumerate(p_list):
                acc = acc + _dot(p[hs, :], v_of(j, h))
            acc_scr[hs, :] = acc

    online([_dot(qbd, kt_refs[r][0]) for r in range(pp)],
           lambda j, h: v_refs[j][0, pl.ds(h, page, stride=heads), :])

    @pl.when(p_id == pl.num_programs(1) - 1)
    def _():
        s = _nt_dot(qbd, kn_ref[0])
        tok = _iota((rows, 8), 0) % new_len
        key = _iota((rows, 8), 1)
        s = jnp.where((tok >= key) & (key < new_len), s, -jnp.inf)
        online([s], lambda j, h: vn_ref[0][:, h * dv:(h + 1) * dv])
        lam = _lambda_value(lam_ref, lam_init)
        nrm = acc_scr[...] / l_scr[...]
        for h in range(heads):
            r0 = h * rph
            oh = nrm[r0:r0 + new_len, :] - lam * nrm[r0 + new_len:r0 + rph, :]
            on = oh * lax.rsqrt(jnp.mean(oh * oh, axis=-1, keepdims=True) + EPS) * sub_ref[...]
            o_ref[0, 0:new_len, h * dv:(h + 1) * dv] = (on * (1.0 - lam_init)).astype(o_ref.dtype)
        if new_len < 8:
            o_ref[0, new_len:8, :] = jnp.zeros((8 - new_len, width), o_ref.dtype)


def _attn_sample(q8, k8, v8, cache_kt, cache_v2, page_table, lam_vecs, subln, *, new_len, heads, comp, lam_init):
    nb, n_pages = page_table.shape
    width, page = cache_kt.shape[1], cache_kt.shape[2]
    dv = 2 * comp
    pp = 8
    assert n_pages % pp == 0 and 2 * new_len == 8
    rows = heads * 2 * new_len
    kern = functools.partial(_attn_sample_kernel, pages_per_step=pp, new_len=new_len, heads=heads,
                             comp=comp, lam_init=lam_init)
    per_req = pl.BlockSpec((1, 8, width), lambda b, p, pt: (b, 0, 0))

    def page_spec(r, shape):
        return pl.BlockSpec((1,) + shape, lambda b, p, pt: (pt[b * n_pages + p * pp + r], 0, 0))

    grid_spec = pltpu.PrefetchScalarGridSpec(
        num_scalar_prefetch=1,
        grid=(nb, n_pages // pp),
        in_specs=[per_req, per_req, per_req,
                  pl.BlockSpec((4, comp), lambda b, p, pt: (0, 0)),
                  pl.BlockSpec((1, dv), lambda b, p, pt: (0, 0))]
                 + [page_spec(r, (width, page)) for r in range(pp)]
                 + [page_spec(r, (page * heads, dv)) for r in range(pp)],
        out_specs=pl.BlockSpec((1, 8, width), lambda b, p, pt: (b, 0, 0)),
        scratch_shapes=[pltpu.VMEM((rows, width), F32), pltpu.VMEM((rows, 1), F32),
                        pltpu.VMEM((rows, 1), F32), pltpu.VMEM((rows, dv), F32)],
    )
    return pl.pallas_call(
        kern,
        out_shape=jax.ShapeDtypeStruct((nb, 8, width), BF16),
        grid_spec=grid_spec,
        compiler_params=_cparams("parallel", "arbitrary"),
        name="attn_sample",
    )(page_table.reshape(-1), q8, k8, v8, lam_vecs, subln, *([cache_kt] * pp), *([cache_v2] * pp))


def _merge_kernel(oa_ref, ob_ref, ga_ref, gb_ref, woa_ref, wob_ref, m_ref):
    ua = _dot(oa_ref[...], woa_ref[...])
    ub = _dot(ob_ref[...], wob_ref[...])
    m_ref[...] = (_sigmoid(ga_ref[...]) * ua + _sigmoid(gb_ref[...]) * ub).astype(m_ref.dtype)


def _merge(o_a, o_b, proj, w_oa, w_ob, *, gate_col_block):
    t, zd = o_a.shape
    d = w_oa.shape[1]
    tm = _row_tile(t, 320)
    return pl.pallas_call(
        _merge_kernel,
        out_shape=jax.ShapeDtypeStruct((t, d), BF16),
        grid=(t // tm,),
        in_specs=[pl.BlockSpec((tm, zd), lambda i: (i, 0)),
                  pl.BlockSpec((tm, o_b.shape[1]), lambda i: (i, 0)),
                  pl.BlockSpec((tm, d), lambda i: (i, gate_col_block)),
                  pl.BlockSpec((tm, d), lambda i: (i, gate_col_block + 1)),
                  pl.BlockSpec(w_oa.shape, lambda i: (0, 0)),
                  pl.BlockSpec(w_ob.shape, lambda i: (0, 0))],
        out_specs=pl.BlockSpec((tm, d), lambda i: (i, 0)),
        compiler_params=_cparams("parallel"),
        name="merge",
    )(o_a, o_b, proj, proj, w_oa, w_ob)


def _outproj_route_kernel(x_ref, m_ref, wo_ref, g_ref, wr_ref, br_ref,
                          x1_ref, h2_ref, rw_ref, re_ref, cnt_ref, *, n_groups, per_group):
    x1 = x_ref[...] + _dot(m_ref[...], wo_ref[...])
    x1_ref[...] = x1
    h2 = x1 * lax.rsqrt(jnp.mean(x1 * x1, axis=-1, keepdims=True) + EPS) * g_ref[...]
    h2_ref[...] = h2
    logits = _dot(h2.astype(BF16), wr_ref[...]) + br_ref[...]
    tm = logits.shape[0]
    lane = _iota((tm, LANES), 1).astype(F32)
    big = float(LANES)

    def masked_softmax(mask):
        lm = jnp.where(mask, logits, -jnp.inf)
        e = jnp.exp(lm - jnp.max(lm, axis=-1, keepdims=True))
        return e / jnp.sum(e, axis=-1, keepdims=True)

    def top1(p, mask):
        v = jnp.max(jnp.where(mask, p, -1.0), axis=-1, keepdims=True)
        idx = jnp.min(jnp.where(mask & (p == v), lane, big), axis=-1, keepdims=True)
        return v, idx

    gmask = lane < n_groups
    pg = masked_softmax(gmask)
    g_w, g_top = top1(pg, gmask)
    lo = n_groups + g_top * per_group
    emask = (lane >= lo) & (lane < lo + per_group)
    pe = masked_softmax(emask)
    v1, i1 = top1(pe, emask)
    emask2 = emask & (lane != i1)
    v2, i2 = top1(pe, emask2)
    denom = v1 + v2
    w1 = g_w * v1 / denom
    w2 = g_w * v2 / denom
    rw_ref[...] = jnp.where(lane == 0.0, w1, jnp.where(lane == 1.0, w2, 0.0))

    @pl.when(pl.program_id(0) == 0)
    def _():
        cnt_ref[...] = jnp.zeros(cnt_ref.shape, F32)

    sel1 = lane == i1
    sel2 = lane == i2
    picked = jnp.where(sel1 | sel2, 1.0, 0.0)
    before = (_iota((tm, tm), 0) > _iota((tm, tm), 1)).astype(BF16)
    prior = _dot(before, picked.astype(BF16)) + cnt_ref[...]
    r1 = jnp.sum(jnp.where(sel1, prior, 0.0), axis=-1, keepdims=True)
    r2 = jnp.sum(jnp.where(sel2, prior, 0.0), axis=-1, keepdims=True)
    cnt_ref[...] = cnt_ref[...] + jnp.sum(picked, axis=0, keepdims=True)
    re_ref[...] = jnp.where(lane == 0.0, i1 - n_groups,
                            jnp.where(lane == 1.0, i2 - n_groups,
                                      jnp.where(lane == 2.0, r1, jnp.where(lane == 3.0, r2, 0.0)))).astype(jnp.int32)


def _outproj_route(x, merged, w_out, gain, w_route, b_route, *, n_groups, per_group):
    t, d = x.shape
    tm = _row_tile(t, 320)
    kern = functools.partial(_outproj_route_kernel, n_groups=n_groups, per_group=per_group)
    row = lambda w: pl.BlockSpec((tm, w), lambda i: (i, 0))
    return pl.pallas_call(
        kern,
        out_shape=(jax.ShapeDtypeStruct((t, d), F32), jax.ShapeDtypeStruct((t, d), F32),
                   jax.ShapeDtypeStruct((t, LANES), F32), jax.ShapeDtypeStruct((t, LANES), jnp.int32),
                   jax.ShapeDtypeStruct((1, LANES), F32)),
        grid=(t // tm,),
        in_specs=[row(d), row(d),
                  pl.BlockSpec(w_out.shape, lambda i: (0, 0)),
                  pl.BlockSpec((1, d), lambda i: (0, 0)),
                  pl.BlockSpec((d, LANES), lambda i: (0, 0)),
                  pl.BlockSpec((1, LANES), lambda i: (0, 0))],
        out_specs=(row(d), row(d), row(LANES), row(LANES), pl.BlockSpec((1, LANES), lambda i: (0, 0))),
        compiler_params=_cparams("arbitrary"),
        name="outproj_route",
    )(x, merged, w_out, gain, w_route, b_route)


def _experts_kernel(be_ref, nused_ref, src_ref, src_next_ref, dst_ref, h_hbm, wg_ref, wu_ref, wd_ref, y_hbm,
                    xbuf, ybuf, gsem, ssem):
    g = pl.program_id(0)
    nused = nused_ref[0]
    blk = MOE_BLOCK

    def start_gather(idx_ref, slot):
        def body(r, carry):
            tok = idx_ref[0, 0, r]
            pltpu.make_async_copy(h_hbm.at[pl.ds(tok, 1), :], xbuf.at[slot, pl.ds(r, 1), :], gsem.at[slot]).start()
            return carry
        lax.fori_loop(0, blk, body, 0, unroll=8)

    def start_scatter(slot):
        def body(r, carry):
            dst = dst_ref[0, 0, r]
            pltpu.make_async_copy(ybuf.at[slot, pl.ds(r, 1), :], y_hbm.at[pl.ds(dst, 1), :], ssem.at[slot]).start()
            return carry
        lax.fori_loop(0, blk, body, 0, unroll=8)

    def wait_gather(slot):
        pltpu.make_async_copy(h_hbm.at[pl.ds(0, blk), :], xbuf.at[slot], gsem.at[slot]).wait()

    def wait_scatter(slot):
        pltpu.make_async_copy(ybuf.at[slot], y_hbm.at[pl.ds(0, blk), :], ssem.at[slot]).wait()

    @pl.when(g < nused)
    def _():
        slot = g % 2

        @pl.when(g == 0)
        def _():
            ybuf[1] = jnp.zeros(ybuf.shape[1:], F32)
            fill = pltpu.make_async_copy(ybuf.at[1], y_hbm.at[pl.ds(y_hbm.shape[0] - blk, blk), :], ssem.at[1])
            fill.start()
            fill.wait()
            start_gather(src_ref, slot)

        @pl.when(g + 1 < nused)
        def _():
            start_gather(src_next_ref, 1 - slot)

        wait_gather(slot)
        x = xbuf[slot]
        hdn = _silu(_dot(x, wg_ref[0])) * _dot(x, wu_ref[0])
        ybuf[slot] = _dot(hdn, wd_ref[0])

        @pl.when(g > 0)
        def _():
            wait_scatter(1 - slot)

        start_scatter(slot)

        @pl.when(g == nused - 1)
        def _():
            wait_scatter(slot)


def _experts(h2, block_e, src_tok, dst_slot, nused, w_gate, w_up, w_down):
    t, d = h2.shape
    n_blocks = block_e.shape[0]
    n_slots = t * TOP_K
    ff = w_gate.shape[2]
    kern = _experts_kernel
    src3 = src_tok.reshape(n_blocks, 1, MOE_BLOCK)
    dst3 = dst_slot.reshape(n_blocks, 1, MOE_BLOCK)
    idx_spec = lambda f: pl.BlockSpec((1, 1, MOE_BLOCK), lambda g, be, nu: (f(g), 0, 0), memory_space=pltpu.SMEM)
    grid_spec = pltpu.PrefetchScalarGridSpec(
        num_scalar_prefetch=2,
        grid=(n_blocks,),
        in_specs=[idx_spec(lambda g: g),
                  idx_spec(lambda g: jnp.minimum(g + 1, n_blocks - 1)),
                  idx_spec(lambda g: g),
                  pl.BlockSpec(memory_space=pl.ANY),
                  pl.BlockSpec((1, d, ff), lambda g, be, nu: (be[g], 0, 0)),
                  pl.BlockSpec((1, d, ff), lambda g, be, nu: (be[g], 0, 0)),
                  pl.BlockSpec((1, ff, d), lambda g, be, nu: (be[g], 0, 0))],
        out_specs=pl.BlockSpec(memory_space=pl.ANY),
        scratch_shapes=[pltpu.VMEM((2, MOE_BLOCK, d), F32), pltpu.VMEM((2, MOE_BLOCK, d), F32),
                        pltpu.SemaphoreType.DMA((2,)), pltpu.SemaphoreType.DMA((2,))],
    )
    return pl.pallas_call(
        kern,
        out_shape=jax.ShapeDtypeStruct((n_slots + MOE_BLOCK, d), F32),
        grid_spec=grid_spec,
        compiler_params=_cparams("arbitrary"),
        name="experts",
    )(block_e, nused, src3, src3, dst3, h2, w_gate, w_up, w_down)


def _expert_plan(eidx, rank, counts):
    t = eidx.shape[0]
    n_experts = counts.shape[0]
    s = t * TOP_K
    padded = (counts + MOE_BLOCK - 1) // MOE_BLOCK * MOE_BLOCK
    ends = jnp.cumsum(padded)
    pad_start = ends - padded
    dest = (pad_start[eidx] + rank).reshape(-1)
    n_blocks = -(-(s + n_experts * (MOE_BLOCK - 1)) // MOE_BLOCK)
    rows = n_blocks * MOE_BLOCK
    slot_of_row = jnp.full((rows,), -1, jnp.int32).at[dest].set(jnp.arange(s, dtype=jnp.int32))
    valid = slot_of_row >= 0
    tok = slot_of_row // TOP_K
    src_tok = jnp.where(valid, tok, 0)
    dst_slot = jnp.where(valid, (slot_of_row % TOP_K) * t + tok, s + jnp.arange(rows, dtype=jnp.int32) % MOE_BLOCK)
    block_start = jnp.arange(n_blocks, dtype=jnp.int32) * MOE_BLOCK
    block_e = jnp.minimum(jnp.sum((ends[None, :] <= block_start[:, None]).astype(jnp.int32), axis=1), n_experts - 1)
    nused = (ends[-1:] // MOE_BLOCK).astype(jnp.int32)
    return block_e, src_tok, dst_slot, nused


def _combine_ple_kernel(x1_ref, ys0_ref, ys1_ref, rw_ref, p_ref, wp_ref, g_ref, wpg_ref, y_ref):
    rw = rw_ref[...]
    moe = rw[:, 0:1] * ys0_ref[...] + rw[:, 1:2] * ys1_ref[...]
    x2 = x1_ref[...] + moe
    hn = x2 * lax.rsqrt(jnp.mean(x2 * x2, axis=-1, keepdims=True) + EPS) * g_ref[...]
    gate = _sigmoid(_dot(hn.astype(BF16), wpg_ref[...]))
    y_ref[...] = x2 + _dot(p_ref[...], wp_ref[...]) * gate


def _combine_ple(x1, y_slots, route_w, p, w_ple, gain, w_ple_gate):
    t, d = x1.shape
    tm = _row_tile(t, 320)
    nt = t // tm
    row = lambda w: pl.BlockSpec((tm, w), lambda i: (i, 0))
    return pl.pallas_call(
        _combine_ple_kernel,
        out_shape=jax.ShapeDtypeStruct((t, d), F32),
        grid=(nt,),
        in_specs=[row(d), row(d), pl.BlockSpec((tm, d), lambda i: (nt + i, 0)), row(LANES), row(p.shape[1]),
                  pl.BlockSpec(w_ple.shape, lambda i: (0, 0)),
                  pl.BlockSpec((1, d), lambda i: (0, 0)),
                  pl.BlockSpec(w_ple_gate.shape, lambda i: (0, 0))],
        out_specs=row(d),
        compiler_params=_cparams("parallel"),
        name="combine_ple",
    )(x1, y_slots, y_slots, route_w, p, w_ple, gain, w_ple_gate)


def _pad_rows(a, rows):
    return jnp.pad(a, ((0, 0), (0, rows - a.shape[1]), (0, 0)))


def _layer(x_all, p_all, nb_p, seq_p, nb_s, seq_s, past_len, state_conv, state_delta, cache_k, cache_v,
           page_table, lam_init, lw):
    tp = nb_p * seq_p
    ts = nb_s * seq_s
    d = x_all.shape[1]
    heads_d = lw['a_log'].shape[0]
    dk = lw['delta_norm'].shape[0]
    hk = heads_d * dk
    conv_dim = 3 * hk
    comp = lw['q_norm'].shape[0]
    dv = 2 * comp
    width = lw['w_ob'].shape[0]
    heads_a = width // dv
    n_groups = lw['w_rg'].shape[1]
    n_experts = lw['w_re'].shape[1]
    assert seq_s <= 8 and conv_dim % 1024 == 0 and hk == 1024 and width == 1024 and d % 1024 == 0
    assert 2 * heads_d <= LANES and n_groups + n_experts <= LANES and lw['conv_w'].shape[0] == 4

    w_in = lw['w_in']
    ba0 = conv_dim + hk
    q0 = ba0 + 2 * heads_d
    gate0 = q0 + 3 * width
    w_main = jnp.concatenate([w_in[:, :ba0], w_in[:, gate0:], w_in[:, q0:gate0]], axis=1).astype(BF16)
    w_ba = jnp.pad(w_in[:, ba0:q0], ((0, 0), (0, LANES - 2 * heads_d))).astype(BF16)
    z_cb = conv_dim // hk
    gate_cb = (conv_dim + hk) // d
    q_cb = (conv_dim + hk + 2 * d) // width
    assert (conv_dim + hk) % d == 0 and (conv_dim + hk + 2 * d) % width == 0

    proj, ba = _inproj(x_all, lw['norm_attn'].reshape(1, d), w_main, w_ba)

    alog_row = jnp.pad(lw['a_log'], (heads_d, LANES - 2 * heads_d)).reshape(1, LANES)
    dtb_row = jnp.pad(lw['dt_bias'], (heads_d, LANES - 2 * heads_d)).reshape(1, LANES)
    dn = lw['delta_norm'].reshape(1, dk)
    chunk_p = math.gcd(seq_p, DELTA_CHUNK)
    oa_p, s_p = _delta_branch(proj, ba, 0, nb_p, seq_p,
                              jnp.zeros((nb_p, 8, conv_dim), F32), jnp.zeros((nb_p, heads_d, dk, dk), F32),
                              lw['conv_w'], alog_row, dtb_row, dn,
                              chunk=chunk_p, valid=chunk_p, heads=heads_d, dk=dk, z_col_block=z_cb)
    qkv_s = proj[tp:, :conv_dim].reshape(nb_s, seq_s, conv_dim)
    proj_s8 = _pad_rows(proj[tp:].reshape(nb_s, seq_s, -1), 8).reshape(nb_s * 8, -1)
    ba_s8 = _pad_rows(ba[tp:].reshape(nb_s, seq_s, LANES), 8).reshape(nb_s * 8, LANES)
    tail_s = jnp.pad(state_conv, ((0, 0), (8 - state_conv.shape[1], 0), (0, 0)))
    oa_s8, s_s = _delta_branch(proj_s8, ba_s8, 0, nb_s, 8, tail_s, state_delta,
                               lw['conv_w'], alog_row, dtb_row, dn,
                               chunk=8, valid=seq_s, heads=heads_d, dk=dk, z_col_block=z_cb)
    oa_s = oa_s8.reshape(nb_s, 8, hk)[:, :seq_s].reshape(ts, hk)
    o_a = jnp.concatenate([oa_p, oa_s], axis=0)
    conv_p = jnp.stack([proj[(b + 1) * seq_p - 3:(b + 1) * seq_p, :conv_dim] for b in range(nb_p)])
    conv_s = jnp.concatenate([state_conv, qkv_s], axis=1)[:, -3:, :]

    half = comp // 2
    inv_freq = ROPE_THETA ** (-jnp.arange(half, dtype=F32) / half)
    def rope_tables(pos):
        ang = pos.astype(F32)[:, None] * inv_freq[None, :]
        sin_h = jnp.sin(ang)
        return (jnp.tile(jnp.cos(ang), (1, LANES // half)),
                jnp.tile(jnp.concatenate([-sin_h, sin_h], axis=1), (1, LANES // comp)))

    q_gain = jnp.tile(lw['q_norm'], LANES // comp).reshape(1, LANES)
    k_gain = jnp.tile(lw['k_norm'], LANES // comp).reshape(1, LANES)
    q_bf, k_bf, v_bf, k_t = _qkprep(proj, 0, nb_p, seq_p, *rope_tables(jnp.arange(seq_p, dtype=jnp.int32)),
                                    q_gain, k_gain, q_col_block=q_cb, width=width, comp=comp, for_prompt=True)
    pos_s = jnp.tile(past_len + jnp.arange(seq_s, dtype=jnp.int32), nb_s)
    q_fs, k_fs = _qkprep(proj, tp, 1, ts, *rope_tables(pos_s), q_gain, k_gain,
                         q_col_block=q_cb, width=width, comp=comp, for_prompt=False)
    lam_vecs = jnp.stack([lw['lam_q1'], lw['lam_k1'], lw['lam_q2'], lw['lam_k2']])
    subln = lw['subln'].reshape(1, dv)
    ob_p = _attn_prompt(q_bf, k_bf, v_bf, lam_vecs, subln, nb=nb_p, seq=seq_p, heads=heads_a, comp=comp,
                        lam_init=lam_init)
    v_f_s = proj[tp:, (q_cb + 2) * width:(q_cb + 3) * width]
    q8 = _pad_rows(q_fs.reshape(nb_s, seq_s, width), 8)
    k8 = _pad_rows(k_fs.reshape(nb_s, seq_s, width), 8)
    v8 = _pad_rows(v_f_s.reshape(nb_s, seq_s, width), 8)
    n_phys, page = cache_k.shape[0], cache_k.shape[1]
    cache_kt = jnp.transpose(cache_k, (0, 2, 3, 4, 1)).reshape(n_phys, width, page)
    cache_v2 = cache_v.reshape(n_phys, page * heads_a, dv)
    ob_s8 = _attn_sample(q8, k8, v8, cache_kt, cache_v2, page_table, lam_vecs, subln,
                         new_len=seq_s, heads=heads_a, comp=comp, lam_init=lam_init)
    o_b = jnp.concatenate([ob_p, ob_s8[:, :seq_s].reshape(ts, width)], axis=0)

    merged = _merge(o_a, o_b, proj, lw['w_oa'].astype(BF16), lw['w_ob'].astype(BF16), gate_col_block=gate_cb)
    w_route = jnp.pad(jnp.concatenate([lw['w_rg'], lw['w_re']], axis=1),
                      ((0, 0), (0, LANES - n_groups - n_experts))).astype(BF16)
    b_route = jnp.pad(jnp.concatenate([lw['b_rg'], lw['b_re']]), (0, LANES - n_groups - n_experts)).reshape(1, LANES)
    x1, h2, route_w, route_i, route_cnt = _outproj_route(x_all, merged, lw['w_out'].astype(BF16),
                                                         lw['norm_ffn'].reshape(1, d), w_route, b_route,
                                                         n_groups=n_groups, per_group=n_experts // n_groups)

    counts = route_cnt[0, n_groups:n_groups + n_experts].astype(jnp.int32)
    block_e, src_tok, dst_slot, nused = _expert_plan(route_i[:, :TOP_K], route_i[:, TOP_K:2 * TOP_K], counts)
    y_slots = _experts(h2, block_e, src_tok, dst_slot, nused, lw['exp_gate'], lw['exp_up'], lw['exp_down'])

    y = _combine_ple(x1, y_slots, route_w, p_all.astype(BF16),
                     lw['w_ple'].astype(BF16), lw['norm_ple'].reshape(1, d), lw['w_ple_gate'].astype(BF16))

    k_p = jnp.transpose(k_t.reshape(nb_p, heads_a, 2, comp, seq_p), (0, 4, 1, 2, 3))
    v_p = proj[:tp, (q_cb + 2) * width:(q_cb + 3) * width].reshape(nb_p, seq_p, heads_a, dv)
    k_s = k_fs.reshape(nb_s, seq_s, heads_a, 2, comp)
    v_s = v_f_s.reshape(nb_s, seq_s, heads_a, dv)
    return y, k_p, v_p, k_s, v_s, conv_p, conv_s, s_p, s_s


def kernel(x_prompt, x_sample, cache_k, cache_v, state_conv, state_delta, page_table, p_prompt, p_sample,
           norm_attn, w_in, conv_w, a_log, dt_bias, delta_norm, q_norm, k_norm, lam_q1, lam_k1, lam_q2, lam_k2,
           subln, w_oa, w_ob, w_out, norm_ffn, w_rg, b_rg, w_re, b_re, exp_gate, exp_up, exp_down,
           norm_ple, w_ple, w_ple_gate):
    nb_p, seq_p, d = x_prompt.shape
    nb_s, seq_s, _ = x_sample.shape
    tp = nb_p * seq_p
    depth = w_in.shape[0]
    past_len = page_table.shape[1] * cache_k.shape[2]
    x_all = jnp.concatenate([x_prompt.reshape(tp, d), x_sample.reshape(nb_s * seq_s, d)], axis=0)
    outs = [[] for _ in range(8)]
    for i in range(depth):
        lw = dict(norm_attn=norm_attn[i], w_in=w_in[i], conv_w=conv_w[i], a_log=a_log[i], dt_bias=dt_bias[i],
                  delta_norm=delta_norm[i], q_norm=q_norm[i], k_norm=k_norm[i], lam_q1=lam_q1[i], lam_k1=lam_k1[i],
                  lam_q2=lam_q2[i], lam_k2=lam_k2[i], subln=subln[i], w_oa=w_oa[i], w_ob=w_ob[i], w_out=w_out[i],
                  norm_ffn=norm_ffn[i], w_rg=w_rg[i], b_rg=b_rg[i], w_re=w_re[i], b_re=b_re[i],
                  exp_gate=exp_gate[i], exp_up=exp_up[i], exp_down=exp_down[i],
                  norm_ple=norm_ple[i], w_ple=w_ple[i], w_ple_gate=w_ple_gate[i])
        lam_init = 0.8 - 0.6 * math.exp(-0.3 * i)
        p_all = jnp.concatenate([p_prompt[i].reshape(tp, -1), p_sample[i].reshape(nb_s * seq_s, -1)], axis=0)
        res = _layer(x_all, p_all, nb_p, seq_p, nb_s, seq_s, past_len, state_conv[i], state_delta[i],
                     cache_k[i], cache_v[i], page_table, lam_init, lw)
        x_all = res[0]
        for lst, val in zip(outs, res[1:]):
            lst.append(val)
    y_p = x_all[:tp].reshape(nb_p, seq_p, d)
    y_s = x_all[tp:].reshape(nb_s, seq_s, d)
    return (y_p, y_s) + tuple(jnp.stack(lst) for lst in outs)
```

```python
import functools
import math

import jax
import jax.numpy as jnp
from jax import lax
from jax.experimental import pallas as pl
from jax.experimental.pallas import tpu as pltpu

F32 = jnp.float32
BF16 = jnp.bfloat16
HIGHEST = lax.Precision.HIGHEST

LANES = 128
DELTA_CHUNK = 64
ROPE_THETA = 10000.0
TOP_K = 2
MOE_BLOCK = 128
EPS = 1e-6
VMEM_LIMIT = 48 * 1024 * 1024


def _cparams(*sem):
    return pltpu.CompilerParams(dimension_semantics=sem, vmem_limit_bytes=VMEM_LIMIT)


def _row_tile(n, target, align=16):
    best = None
    for t in range(align, min(n, target) + 1, align):
        if n % t == 0:
            best = t
    assert best is not None, (n, target, align)
    return best


def _nt_dot(a, b, precision=None):
    return lax.dot_general(a, b, (((1,), (1,)), ((), ())), precision=precision,
                           preferred_element_type=F32)


def _tn_dot(a, b, precision=None):
    return lax.dot_general(a, b, (((0,), (0,)), ((), ())), precision=precision,
                           preferred_element_type=F32)


def _dot(a, b, precision=None):
    return jnp.dot(a, b, precision=precision, preferred_element_type=F32)


def _split_bf16(a):
    hi = a.astype(BF16)
    return hi, (a - hi.astype(F32)).astype(BF16)


def _dot_split(a, b):
    return _dot(a[0], b[0]) + (_dot(a[0], b[1]) + _dot(a[1], b[0]))


def _sigmoid(x):
    return 1.0 / (1.0 + jnp.exp(-x))


def _silu(x):
    return x * _sigmoid(x)


def _iota(shape, dim):
    return lax.broadcasted_iota(jnp.int32, shape, dim)


def _inproj_kernel(x_ref, g_ref, w_ref, wba_ref, p_ref, ba_ref, h_scr):
    @pl.when(pl.program_id(1) == 0)
    def _():
        x = x_ref[...]
        h = x * lax.rsqrt(jnp.mean(x * x, axis=-1, keepdims=True) + EPS) * g_ref[...]
        h_scr[...] = h.astype(BF16)
        ba_ref[...] = _dot(h_scr[...], wba_ref[...])

    p_ref[...] = _dot(h_scr[...], w_ref[...])


def _inproj(x, gain, w_main, w_ba):
    t, d = x.shape
    n = w_main.shape[1]
    tm = _row_tile(t, 640)
    tn = 1024
    assert n % tn == 0
    return pl.pallas_call(
        _inproj_kernel,
        out_shape=(jax.ShapeDtypeStruct((t, n), F32), jax.ShapeDtypeStruct((t, LANES), F32)),
        grid=(t // tm, n // tn),
        in_specs=[pl.BlockSpec((tm, d), lambda i, j: (i, 0)),
                  pl.BlockSpec((1, d), lambda i, j: (0, 0)),
                  pl.BlockSpec((d, tn), lambda i, j: (0, j)),
                  pl.BlockSpec((d, LANES), lambda i, j: (0, 0))],
        out_specs=(pl.BlockSpec((tm, tn), lambda i, j: (i, j)),
                   pl.BlockSpec((tm, LANES), lambda i, j: (i, 0))),
        scratch_shapes=[pltpu.VMEM((tm, d), BF16)],
        compiler_params=_cparams("parallel", "arbitrary"),
        name="inproj",
    )(x, gain, w_main, w_ba)


def _delta_kernel(qkv_ref, z_ref, ba_ref, tail0_ref, s0_ref, cw_ref, alog_ref, dtb_ref, dn_ref,
                  o_ref, sfin_ref, ext_scr, s_scr, *, chunk, valid, heads, dk):
    n = pl.program_id(1)
    c = chunk
    hk = heads * dk

    @pl.when(n == 0)
    def _():
        ext_scr[0:8, :] = tail0_ref[0]
        s_scr[...] = s0_ref[0]

    ext_scr[8:8 + c, :] = qkv_ref[...]

    row = _iota((c, 1), 0)
    rowmask = (row < valid).astype(F32) if valid < c else None

    ba = ba_ref[...]
    beta_all = _sigmoid(ba)
    xg = ba + dtb_ref[...]
    softplus = jnp.maximum(xg, 0.0) + jnp.log1p(jnp.exp(-jnp.abs(xg)))
    g_all = -jnp.exp(alog_ref[...]) * softplus
    if rowmask is not None:
        beta_all = beta_all * rowmask
        g_all = g_all * rowmask
    r_i = _iota((c, c), 0)
    c_i = _iota((c, c), 1)
    incl = r_i >= c_i
    strict = r_i > c_i
    gc_all = _dot(incl.astype(F32), g_all, precision=HIGHEST)
    gc_t = jnp.transpose(gc_all)
    eye = (r_i == c_i).astype(F32)

    def conv(col0):
        sl = slice(col0, col0 + dk)
        acc = ext_scr[8:8 + c, sl] * cw_ref[3:4, sl]
        acc = acc + ext_scr[7:7 + c, sl] * cw_ref[2:3, sl]
        acc = acc + ext_scr[6:6 + c, sl] * cw_ref[1:2, sl]
        acc = acc + ext_scr[5:5 + c, sl] * cw_ref[0:1, sl]
        return _silu(acc)

    hs = range(heads)
    q, k, v, beta, gc, egc, g_last, decay = [], [], [], [], [], [], [], []
    for h in hs:
        qh = conv(h * dk)
        kh = conv(hk + h * dk)
        vh = conv(2 * hk + h * dk)
        qh = qh * lax.rsqrt(jnp.sum(qh * qh, axis=-1, keepdims=True) + EPS) * (dk ** -0.5)
        kh = kh * lax.rsqrt(jnp.sum(kh * kh, axis=-1, keepdims=True) + EPS)
        if rowmask is not None:
            qh, kh, vh = qh * rowmask, kh * rowmask, vh * rowmask
        q.append(qh)
        k.append(kh)
        v.append(vh)
        beta.append(beta_all[:, h:h + 1])
        gc.append(gc_all[:, heads + h:heads + h + 1])
        egc.append(jnp.exp(gc[h]))
        g_last.append(gc_all[c - 1:c, heads + h:heads + h + 1])
        gc_row = gc_t[heads + h:heads + h + 1, :]
        decay.append(jnp.where(incl, jnp.exp(jnp.where(incl, gc[h] - gc_row, 0.0)), 0.0))
    kb = [k[h] * beta[h] for h in hs]
    kk = [_nt_dot(kb[h], k[h]) for h in hs]
    qk = [_nt_dot(q[h], k[h]) for h in hs]
    apow = [jnp.where(strict, kk[h] * decay[h], 0.0) for h in hs]
    qk = [qk[h] * decay[h] for h in hs]
    tinv = [eye - apow[h] for h in hs]
    span = 2
    asp = [_split_bf16(apow[h]) for h in hs]
    while span < c:
        asp = [_split_bf16(_dot_split(asp[h], asp[h])) for h in hs]
        tinv = [tinv[h] + _dot_split(_split_bf16(tinv[h]), asp[h]) for h in hs]
        span *= 2
    u = [_dot(tinv[h], v[h] * beta[h]) for h in hs]
    w = [_dot(tinv[h], kb[h] * egc[h]) for h in hs]
    s = [s_scr[h] for h in hs]
    ws = [_dot(w[h], s[h]) for h in hs]
    qs = [_dot(q[h] * egc[h], s[h]) for h in hs]
    v_new = [u[h] - ws[h] for h in hs]
    o = [qs[h] + _dot(qk[h], v_new[h]) for h in hs]
    ds = [_tn_dot(k[h] * jnp.exp(g_last[h] - gc[h]), v_new[h]) for h in hs]
    for h in hs:
        s_scr[h] = s[h] * jnp.exp(g_last[h]) + ds[h]
        on = o[h] * lax.rsqrt(jnp.mean(o[h] * o[h], axis=-1, keepdims=True) + EPS) * dn_ref[...]
        zh = z_ref[:, h * dk:(h + 1) * dk]
        o_ref[:, h * dk:(h + 1) * dk] = (on * _silu(zh)).astype(o_ref.dtype)

    ext_scr[0:8, :] = ext_scr[c:c + 8, :]

    @pl.when(n == pl.num_programs(1) - 1)
    def _():
        sfin_ref[0] = s_scr[...]


def _delta_branch(proj, ba, row0, nb, seq, tail0, s0, conv_w, alog_row, dtb_row, delta_norm, *,
                  chunk, valid, heads, dk, z_col_block):
    hk = heads * dk
    nchunks = seq // chunk
    assert seq % chunk == 0 and row0 % chunk == 0
    r0 = row0 // chunk
    kern = functools.partial(_delta_kernel, chunk=chunk, valid=valid, heads=heads, dk=dk)
    return pl.pallas_call(
        kern,
        out_shape=(jax.ShapeDtypeStruct((nb * seq, hk), BF16),
                   jax.ShapeDtypeStruct((nb, heads, dk, dk), F32)),
        grid=(nb, nchunks),
        in_specs=[pl.BlockSpec((chunk, 3 * hk), lambda b, n: (r0 + b * nchunks + n, 0)),
                  pl.BlockSpec((chunk, hk), lambda b, n: (r0 + b * nchunks + n, z_col_block)),
                  pl.BlockSpec((chunk, LANES), lambda b, n: (r0 + b * nchunks + n, 0)),
                  pl.BlockSpec((1, 8, 3 * hk), lambda b, n: (b, 0, 0)),
                  pl.BlockSpec((1, heads, dk, dk), lambda b, n: (b, 0, 0, 0)),
                  pl.BlockSpec((4, 3 * hk), lambda b, n: (0, 0)),
                  pl.BlockSpec((1, LANES), lambda b, n: (0, 0)),
                  pl.BlockSpec((1, LANES), lambda b, n: (0, 0)),
                  pl.BlockSpec((1, dk), lambda b, n: (0, 0))],
        out_specs=(pl.BlockSpec((chunk, hk), lambda b, n: (b * nchunks + n, 0)),
                   pl.BlockSpec((1, heads, dk, dk), lambda b, n: (b, 0, 0, 0))),
        scratch_shapes=[pltpu.VMEM((8 + chunk, 3 * hk), F32),
                        pltpu.VMEM((heads, dk, dk), F32)],
        compiler_params=_cparams("parallel", "arbitrary"),
        name="delta_c%d" % chunk,
    )(proj, proj, ba, tail0, s0, conv_w, alog_row, dtb_row, delta_norm)


def _qkprep_kernel(q_ref, k_ref, v_ref, cos_ref, sin_ref, qg_ref, kg_ref, *out_refs, comp, scale, for_prompt, tk):
    tm = q_ref.shape[0]
    r_i = _iota((LANES, LANES), 0)
    c_i = _iota((LANES, LANES), 1)
    group = (r_i // comp == c_i // comp).astype(BF16)
    lane = _iota((tm, LANES), 1)
    first_half = (lane % comp) < (comp // 2)
    cos = cos_ref[...]
    sin = sin_ref[...]

    def norm_rope(x, gain):
        sq = x * x
        hi = sq.astype(BF16)
        lo = (sq - hi.astype(F32)).astype(BF16)
        ms = (_dot(hi, group) + _dot(lo, group)) * (1.0 / comp)
        y = x * lax.rsqrt(ms + EPS) * gain
        swapped = jnp.where(first_half, pltpu.roll(y, LANES - comp // 2, 1), pltpu.roll(y, comp // 2, 1))
        return y * cos + swapped * sin

    for j in range(q_ref.shape[1] // LANES):
        sl = slice(j * LANES, (j + 1) * LANES)
        qr = norm_rope(q_ref[:, sl], qg_ref[...]) * scale
        kr = norm_rope(k_ref[:, sl], kg_ref[...])
        if for_prompt:
            qt_ref, kbo_ref, vt_ref, kt_ref = out_refs
            qt_ref[0, sl, :] = jnp.transpose(qr).astype(BF16)
            kbo_ref[:, sl] = kr.astype(BF16)
            kt_ref[0, sl, :] = jnp.transpose(kr)
            vt = jnp.transpose(v_ref[:, sl]).astype(BF16)
            for u in range(tm // tk):
                vt_ref[0, u, sl, :] = vt[:, u * tk:(u + 1) * tk]
        else:
            qo_ref, kfo_ref = out_refs
            qo_ref[:, sl] = qr
            kfo_ref[:, sl] = kr


def _qkprep(proj, row0, nb, seq, cos_t, sin_t, q_gain, k_gain, *, q_col_block, width, comp, for_prompt, tk=None):
    tm = _row_tile(seq, 512, align=LANES) if for_prompt else seq
    assert row0 % tm == 0 and seq % tm == 0 and (not for_prompt or tm % tk == 0)
    r0, per_seq = row0 // tm, seq // tm
    kern = functools.partial(_qkprep_kernel, comp=comp, scale=comp ** -0.5, for_prompt=for_prompt, tk=tk)
    col = lambda cb: pl.BlockSpec((tm, width), lambda i: (r0 + i, cb))
    small = pl.BlockSpec((tm, LANES), lambda i: (i % per_seq, 0))
    gain = pl.BlockSpec((1, LANES), lambda i: (0, 0))
    out = pl.BlockSpec((tm, width), lambda i: (i, 0))
    rows = nb * seq
    if for_prompt:
        out_t = pl.BlockSpec((1, width, tm), lambda i: (i // per_seq, 0, i % per_seq))
        out_shape = (jax.ShapeDtypeStruct((nb, width, seq), BF16), jax.ShapeDtypeStruct((rows, width), BF16),
                     jax.ShapeDtypeStruct((nb, seq // tk, width, tk), BF16),
                     jax.ShapeDtypeStruct((nb, width, seq), F32))
        out_specs = (out_t, out,
                     pl.BlockSpec((1, tm // tk, width, tk), lambda i: (i // per_seq, i % per_seq, 0, 0)), out_t)
    else:
        out_shape = (jax.ShapeDtypeStruct((rows, width), F32), jax.ShapeDtypeStruct((rows, width), F32))
        out_specs = (out, out)
    return pl.pallas_call(
        kern,
        out_shape=out_shape,
        grid=(rows // tm,),
        in_specs=[col(q_col_block), col(q_col_block + 1), col(q_col_block + 2), small, small, gain, gain],
        out_specs=out_specs,
        compiler_params=_cparams("parallel"),
        name="qkprep_prompt" if for_prompt else "qkprep_sample",
    )(proj, proj, proj, cos_t, sin_t, q_gain, k_gain)


def _lambda_value(lam_ref, lam_init):
    lv = lam_ref[...]
    s1 = jnp.sum(lv[0:1] * lv[1:2], axis=-1, keepdims=True)
    s2 = jnp.sum(lv[2:3] * lv[3:4], axis=-1, keepdims=True)
    return jnp.exp(s1) - jnp.exp(s2) + lam_init


def _attn_prompt_kernel(q_ref, k_ref, v_ref, lam_ref, sub_ref, o_ref, *, tq, tk, comp, heads_per_step, lam_init):
    i = pl.program_id(2)
    dv = 2 * comp
    hr = range(heads_per_step)
    row = _iota((dv, tq), 0)
    qs = []
    for hh in hr:
        qt = q_ref[0, hh * dv:(hh + 1) * dv, :]
        zero = jnp.zeros_like(qt)
        qs.append(jnp.concatenate([jnp.where(row < comp, qt, zero), jnp.where(row >= comp, qt, zero)], axis=1))

    def update(carry, j, mask):
        start = pl.multiple_of(j * tk, tk)
        kt = [k_ref[pl.ds(start, tk), hh * dv:(hh + 1) * dv] for hh in hr]
        vt = [v_ref[0, j, hh * dv:(hh + 1) * dv, :] for hh in hr]
        s = [_dot(kt[hh], qs[hh]) for hh in hr]
        if mask is not None:
            s = [jnp.where(mask, sn, -jnp.inf) for sn in s]
        m_new = [jnp.maximum(carry[hh][0], jnp.max(s[hh], axis=0, keepdims=True)) for hh in hr]
        p = [jnp.exp(s[hh] - m_new[hh]) for hh in hr]
        corr = [jnp.exp(carry[hh][0] - m_new[hh]) for hh in hr]
        l = [carry[hh][1] * corr[hh] + jnp.sum(p[hh], axis=0, keepdims=True) for hh in hr]
        pv = [_dot(vt[hh], p[hh].astype(BF16)) for hh in hr]
        return tuple((m_new[hh], l[hh], carry[hh][2] * corr[hh] + pv[hh]) for hh in hr)

    def body(j, carry):
        return update(carry, j, None)

    init = tuple((jnp.full((1, 2 * tq), -jnp.inf, F32), jnp.zeros((1, 2 * tq), F32),
                  jnp.zeros((dv, 2 * tq), F32)) for _ in hr)
    n_full = (i * tq) // tk
    carry = lax.fori_loop(0, n_full, body, init)
    kpos = n_full * tk + _iota((tk, 2 * tq), 0)
    qpos = i * tq + _iota((tk, 2 * tq), 1) % tq
    carry = update(carry, n_full, qpos >= kpos)
    lam = _lambda_value(lam_ref, lam_init)
    for hh in hr:
        _, l, a = carry[hh]
        n = a / l
        o = n[:, 0:tq] - lam * n[:, tq:2 * tq]
        on = o * lax.rsqrt(jnp.mean(o * o, axis=0, keepdims=True) + EPS) * sub_ref[...]
        o_ref[:, hh * dv:(hh + 1) * dv] = jnp.transpose(on * (1.0 - lam_init)).astype(o_ref.dtype)


def _attn_prompt(q_t, k, v_t, lam_vecs, subln_col, *, nb, seq, tk, heads, comp, lam_init):
    dv = 2 * comp
    tq = _row_tile(seq, 256, align=LANES)
    assert tk % tq == 0 and seq % tk == 0
    nq = seq // tq
    hps = 2
    assert heads % hps == 0
    kern = functools.partial(_attn_prompt_kernel, tq=tq, tk=tk, comp=comp, heads_per_step=hps, lam_init=lam_init)
    return pl.pallas_call(
        kern,
        out_shape=jax.ShapeDtypeStruct((nb * seq, heads * dv), BF16),
        grid=(nb, heads // hps, nq),
        in_specs=[pl.BlockSpec((1, hps * dv, tq), lambda b, h, i: (b, h, i)),
                  pl.BlockSpec((seq, hps * dv), lambda b, h, i: (b, h)),
                  pl.BlockSpec((1, seq // tk, hps * dv, tk), lambda b, h, i: (b, 0, h, 0)),
                  pl.BlockSpec((4, comp), lambda b, h, i: (0, 0)),
                  pl.BlockSpec((dv, 1), lambda b, h, i: (0, 0))],
        out_specs=pl.BlockSpec((tq, hps * dv), lambda b, h, i: (b * nq + i, h)),
        compiler_params=_cparams("parallel", "parallel", "arbitrary"),
        name="attn_prompt",
    )(q_t, k, v_t, lam_vecs, subln_col)


def _attn_sample_kernel(pt_ref, q_ref, kn_ref, vn_ref, lam_ref, sub_ref, *rest,
                        pages_per_step, new_len, heads, comp, lam_init):
    pp = pages_per_step
    kt_refs = rest[:pp]
    v_refs = rest[pp:2 * pp]
    o_ref = rest[2 * pp]
    qbd_scr, m_scr, l_scr, acc_scr = rest[2 * pp + 1:]
    p_id = pl.program_id(1)
    rph = 2 * new_len
    rows = heads * rph
    dv = 2 * comp
    width = heads * dv
    page = kt_refs[0].shape[2]

    @pl.when(p_id == 0)
    def _():
        q = q_ref[0][0:new_len, :]
        lane_grp = _iota((new_len, width), 1) // comp
        for h in range(heads):
            for c in range(2):
                r0 = h * rph + c * new_len
                qbd_scr[r0:r0 + new_len, :] = jnp.where(lane_grp == 2 * h + c, q, 0.0)
        m_scr[...] = jnp.full(m_scr.shape, -jnp.inf, F32)
        l_scr[...] = jnp.zeros(l_scr.shape, F32)
        acc_scr[...] = jnp.zeros(acc_scr.shape, F32)

    qbd = qbd_scr[...]

    def online(s_list, v_of):
        m = m_scr[...]
        m_new = m
        for s in s_list:
            m_new = jnp.maximum(m_new, jnp.max(s, axis=-1, keepdims=True))
        corr = jnp.exp(m - m_new)
        p_list = [jnp.exp(s - m_new) for s in s_list]
        l = l_scr[...] * corr
        for p in p_list:
            l = l + jnp.sum(p, axis=-1, keepdims=True)
        m_scr[...] = m_new
        l_scr[...] = l
        for h in range(heads):
            hs = slice(h * rph, (h + 1) * rph)
            acc = acc_scr[hs, :] * corr[hs, :]
            for j, p in enumerate(p_list):
                acc = acc + _dot(p[hs, :], v_of(j, h))
            acc_scr[hs, :] = acc

    online([_dot(qbd, kt_refs[r][0]) for r in range(pp)],
           lambda j, h: v_refs[j][0, pl.ds(h, page, stride=heads), :])

    @pl.when(p_id == pl.num_programs(1) - 1)
    def _():
        s = _nt_dot(qbd, kn_ref[0])
        tok = _iota((rows, 8), 0) % new_len
        key = _iota((rows, 8), 1)
        s = jnp.where((tok >= key) & (key < new_len), s, -jnp.inf)
        online([s], lambda j, h: vn_ref[0][:, h * dv:(h + 1) * dv])
        lam = _lambda_value(lam_ref, lam_init)
        nrm = acc_scr[...] / l_scr[...]
        for h in range(heads):
            r0 = h * rph
            oh = nrm[r0:r0 + new_len, :] - lam * nrm[r0 + new_len:r0 + rph, :]
            on = oh * lax.rsqrt(jnp.mean(oh * oh, axis=-1, keepdims=True) + EPS) * sub_ref[...]
            o_ref[0, 0:new_len, h * dv:(h + 1) * dv] = (on * (1.0 - lam_init)).astype(o_ref.dtype)
        if new_len < 8:
            o_ref[0, new_len:8, :] = jnp.zeros((8 - new_len, width), o_ref.dtype)


def _attn_sample(q8, k8, v8, cache_kt, cache_v2, page_table, lam_vecs, subln, *, new_len, heads, comp, lam_init):
    nb, n_pages = page_table.shape
    width, page = cache_kt.shape[1], cache_kt.shape[2]
    dv = 2 * comp
    pp = 8
    assert n_pages % pp == 0 and 2 * new_len == 8
    rows = heads * 2 * new_len
    kern = functools.partial(_attn_sample_kernel, pages_per_step=pp, new_len=new_len, heads=heads,
                             comp=comp, lam_init=lam_init)
    per_req = pl.BlockSpec((1, 8, width), lambda b, p, pt: (b, 0, 0))

    def page_spec(r, shape):
        return pl.BlockSpec((1,) + shape, lambda b, p, pt: (pt[b * n_pages + p * pp + r], 0, 0))

    grid_spec = pltpu.PrefetchScalarGridSpec(
        num_scalar_prefetch=1,
        grid=(nb, n_pages // pp),
        in_specs=[per_req, per_req, per_req,
                  pl.BlockSpec((4, comp), lambda b, p, pt: (0, 0)),
                  pl.BlockSpec((1, dv), lambda b, p, pt: (0, 0))]
                 + [page_spec(r, (width, page)) for r in range(pp)]
                 + [page_spec(r, (page * heads, dv)) for r in range(pp)],
        out_specs=pl.BlockSpec((1, 8, width), lambda b, p, pt: (b, 0, 0)),
        scratch_shapes=[pltpu.VMEM((rows, width), F32), pltpu.VMEM((rows, 1), F32),
                        pltpu.VMEM((rows, 1), F32), pltpu.VMEM((rows, dv), F32)],
    )
    return pl.pallas_call(
        kern,
        out_shape=jax.ShapeDtypeStruct((nb, 8, width), BF16),
        grid_spec=grid_spec,
        compiler_params=_cparams("parallel", "arbitrary"),
        name="attn_sample",
    )(page_table.reshape(-1), q8, k8, v8, lam_vecs, subln, *([cache_kt] * pp), *([cache_v2] * pp))


def _merge_kernel(oa_ref, ob_ref, ga_ref, gb_ref, woa_ref, wob_ref, m_ref):
    ua = _dot(oa_ref[...], woa_ref[...])
    ub = _dot(ob_ref[...], wob_ref[...])
    m_ref[...] = (_sigmoid(ga_ref[...]) * ua + _sigmoid(gb_ref[...]) * ub).astype(m_ref.dtype)


def _merge(o_a, o_b, proj, w_oa, w_ob, *, gate_col_block):
    t, zd = o_a.shape
    d = w_oa.shape[1]
    tm = _row_tile(t, 640)
    once = lambda shape: pl.BlockSpec(shape, lambda i: (0, 0), pipeline_mode=pl.Buffered(1))
    return pl.pallas_call(
        _merge_kernel,
        out_shape=jax.ShapeDtypeStruct((t, d), BF16),
        grid=(t // tm,),
        in_specs=[pl.BlockSpec((tm, zd), lambda i: (i, 0)),
                  pl.BlockSpec((tm, o_b.shape[1]), lambda i: (i, 0)),
                  pl.BlockSpec((tm, d), lambda i: (i, gate_col_block)),
                  pl.BlockSpec((tm, d), lambda i: (i, gate_col_block + 1)),
                  once(w_oa.shape), once(w_ob.shape)],
        out_specs=pl.BlockSpec((tm, d), lambda i: (i, 0)),
        compiler_params=_cparams("parallel"),
        name="merge",
    )(o_a, o_b, proj, proj, w_oa, w_ob)


def _outproj_route_kernel(x_ref, m_ref, wo_ref, g_ref, wr_ref, br_ref,
                          x1_ref, h2_ref, rw_ref, re_ref, cnt_ref, *, n_groups, per_group):
    x1 = x_ref[...] + _dot(m_ref[...], wo_ref[...])
    x1_ref[...] = x1
    h2 = x1 * lax.rsqrt(jnp.mean(x1 * x1, axis=-1, keepdims=True) + EPS) * g_ref[...]
    h2_ref[...] = h2
    logits = _dot(h2.astype(BF16), wr_ref[...]) + br_ref[...]
    tm = logits.shape[0]
    lane = _iota((tm, LANES), 1).astype(F32)
    big = float(LANES)

    def masked_softmax(mask):
        lm = jnp.where(mask, logits, -jnp.inf)
        e = jnp.exp(lm - jnp.max(lm, axis=-1, keepdims=True))
        return e / jnp.sum(e, axis=-1, keepdims=True)

    def top1(p, mask):
        v = jnp.max(jnp.where(mask, p, -1.0), axis=-1, keepdims=True)
        idx = jnp.min(jnp.where(mask & (p == v), lane, big), axis=-1, keepdims=True)
        return v, idx

    gmask = lane < n_groups
    pg = masked_softmax(gmask)
    g_w, g_top = top1(pg, gmask)
    lo = n_groups + g_top * per_group
    emask = (lane >= lo) & (lane < lo + per_group)
    pe = masked_softmax(emask)
    v1, i1 = top1(pe, emask)
    emask2 = emask & (lane != i1)
    v2, i2 = top1(pe, emask2)
    denom = v1 + v2
    w1 = g_w * v1 / denom
    w2 = g_w * v2 / denom
    rw_ref[...] = jnp.where(lane == 0.0, w1, jnp.where(lane == 1.0, w2, 0.0))

    @pl.when(pl.program_id(0) == 0)
    def _():
        cnt_ref[...] = jnp.zeros(cnt_ref.shape, F32)

    sel1 = lane == i1
    sel2 = lane == i2
    picked = jnp.where(sel1 | sel2, 1.0, 0.0)
    before = (_iota((tm, tm), 0) > _iota((tm, tm), 1)).astype(BF16)
    prior = _dot(before, picked.astype(BF16)) + cnt_ref[...]
    r1 = jnp.sum(jnp.where(sel1, prior, 0.0), axis=-1, keepdims=True)
    r2 = jnp.sum(jnp.where(sel2, prior, 0.0), axis=-1, keepdims=True)
    cnt_ref[...] = cnt_ref[...] + jnp.sum(picked, axis=0, keepdims=True)
    re_ref[...] = jnp.where(lane == 0.0, i1 - n_groups,
                            jnp.where(lane == 1.0, i2 - n_groups,
                                      jnp.where(lane == 2.0, r1, jnp.where(lane == 3.0, r2, 0.0)))).astype(jnp.int32)


def _outproj_route(x, merged, w_out, gain, w_route, b_route, *, n_groups, per_group):
    t, d = x.shape
    tm = _row_tile(t, 320)
    kern = functools.partial(_outproj_route_kernel, n_groups=n_groups, per_group=per_group)
    row = lambda w: pl.BlockSpec((tm, w), lambda i: (i, 0))
    return pl.pallas_call(
        kern,
        out_shape=(jax.ShapeDtypeStruct((t, d), F32), jax.ShapeDtypeStruct((t, d), F32),
                   jax.ShapeDtypeStruct((t, LANES), F32), jax.ShapeDtypeStruct((t, LANES), jnp.int32),
                   jax.ShapeDtypeStruct((1, LANES), F32)),
        grid=(t // tm,),
        in_specs=[row(d), row(d),
                  pl.BlockSpec(w_out.shape, lambda i: (0, 0)),
                  pl.BlockSpec((1, d), lambda i: (0, 0)),
                  pl.BlockSpec((d, LANES), lambda i: (0, 0)),
                  pl.BlockSpec((1, LANES), lambda i: (0, 0))],
        out_specs=(row(d), row(d), row(LANES), row(LANES), pl.BlockSpec((1, LANES), lambda i: (0, 0))),
        compiler_params=_cparams("arbitrary"),
        name="outproj_route",
    )(x, merged, w_out, gain, w_route, b_route)


def _experts_kernel(be_ref, nused_ref, src_ref, src_next_ref, dst_ref, h_hbm, wg_ref, wu_ref, wd_ref, y_hbm,
                    xbuf, ybuf, gsem, ssem):
    g = pl.program_id(0)
    nused = nused_ref[0]
    blk = MOE_BLOCK

    def start_gather(idx_ref, slot):
        def body(r, carry):
            tok = idx_ref[0, 0, r]
            pltpu.make_async_copy(h_hbm.at[pl.ds(tok, 1), :], xbuf.at[slot, pl.ds(r, 1), :], gsem.at[slot]).start()
            return carry
        lax.fori_loop(0, blk, body, 0, unroll=8)

    def start_scatter(slot):
        def body(r, carry):
            dst = dst_ref[0, 0, r]
            pltpu.make_async_copy(ybuf.at[slot, pl.ds(r, 1), :], y_hbm.at[pl.ds(dst, 1), :], ssem.at[slot]).start()
            return carry
        lax.fori_loop(0, blk, body, 0, unroll=8)

    def wait_gather(slot):
        pltpu.make_async_copy(h_hbm.at[pl.ds(0, blk), :], xbuf.at[slot], gsem.at[slot]).wait()

    def wait_scatter(slot):
        pltpu.make_async_copy(ybuf.at[slot], y_hbm.at[pl.ds(0, blk), :], ssem.at[slot]).wait()

    @pl.when(g < nused)
    def _():
        slot = g % 2

        @pl.when(g == 0)
        def _():
            ybuf[1] = jnp.zeros(ybuf.shape[1:], F32)
            fill = pltpu.make_async_copy(ybuf.at[1], y_hbm.at[pl.ds(y_hbm.shape[0] - blk, blk), :], ssem.at[1])
            fill.start()
            fill.wait()
            start_gather(src_ref, slot)

        @pl.when(g + 1 < nused)
        def _():
            start_gather(src_next_ref, 1 - slot)

        wait_gather(slot)
        x = xbuf[slot]
        hdn = _silu(_dot(x, wg_ref[0])) * _dot(x, wu_ref[0])
        ybuf[slot] = _dot(hdn, wd_ref[0])

        @pl.when(g > 0)
        def _():
            wait_scatter(1 - slot)

        start_scatter(slot)

        @pl.when(g == nused - 1)
        def _():
            wait_scatter(slot)


def _experts(h2, block_e, src_tok, dst_slot, nused, w_gate, w_up, w_down):
    t, d = h2.shape
    n_blocks = block_e.shape[0]
    n_slots = t * TOP_K
    ff = w_gate.shape[2]
    kern = _experts_kernel
    src3 = src_tok.reshape(n_blocks, 1, MOE_BLOCK)
    dst3 = dst_slot.reshape(n_blocks, 1, MOE_BLOCK)
    idx_spec = lambda f: pl.BlockSpec((1, 1, MOE_BLOCK), lambda g, be, nu: (f(g), 0, 0), memory_space=pltpu.SMEM)
    grid_spec = pltpu.PrefetchScalarGridSpec(
        num_scalar_prefetch=2,
        grid=(n_blocks,),
        in_specs=[idx_spec(lambda g: g),
                  idx_spec(lambda g: jnp.minimum(g + 1, n_blocks - 1)),
                  idx_spec(lambda g: g),
                  pl.BlockSpec(memory_space=pl.ANY),
                  pl.BlockSpec((1, d, ff), lambda g, be, nu: (be[g], 0, 0)),
                  pl.BlockSpec((1, d, ff), lambda g, be, nu: (be[g], 0, 0)),
                  pl.BlockSpec((1, ff, d), lambda g, be, nu: (be[g], 0, 0))],
        out_specs=pl.BlockSpec(memory_space=pl.ANY),
        scratch_shapes=[pltpu.VMEM((2, MOE_BLOCK, d), F32), pltpu.VMEM((2, MOE_BLOCK, d), F32),
                        pltpu.SemaphoreType.DMA((2,)), pltpu.SemaphoreType.DMA((2,))],
    )
    return pl.pallas_call(
        kern,
        out_shape=jax.ShapeDtypeStruct((n_slots + MOE_BLOCK, d), F32),
        grid_spec=grid_spec,
        compiler_params=_cparams("arbitrary"),
        name="experts",
    )(block_e, nused, src3, src3, dst3, h2, w_gate, w_up, w_down)


def _expert_plan(eidx, rank, counts, tp):
    t = eidx.shape[0]
    n_experts = counts.shape[0]
    s = t * TOP_K
    padded = (counts + MOE_BLOCK - 1) // MOE_BLOCK * MOE_BLOCK
    ends = jnp.cumsum(padded)
    pad_start = ends - padded
    dest = (pad_start[eidx] + rank).reshape(-1)
    n_blocks = -(-(s + n_experts * (MOE_BLOCK - 1)) // MOE_BLOCK)
    rows = n_blocks * MOE_BLOCK
    slot_of_row = jnp.full((rows,), -1, jnp.int32).at[dest].set(jnp.arange(s, dtype=jnp.int32))
    valid = slot_of_row >= 0
    tok = slot_of_row // TOP_K
    src_tok = jnp.where(valid, tok, 0)
    kk = slot_of_row % TOP_K
    dst_real = jnp.where(tok < tp, kk * tp + tok, TOP_K * tp + kk * (t - tp) + (tok - tp))
    dst_slot = jnp.where(valid, dst_real, s + jnp.arange(rows, dtype=jnp.int32) % MOE_BLOCK)
    block_start = jnp.arange(n_blocks, dtype=jnp.int32) * MOE_BLOCK
    block_e = jnp.minimum(jnp.sum((ends[None, :] <= block_start[:, None]).astype(jnp.int32), axis=1), n_experts - 1)
    nused = (ends[-1:] // MOE_BLOCK).astype(jnp.int32)
    return block_e, src_tok, dst_slot, nused


def _combine_ple_kernel(x1_ref, ys0_ref, ys1_ref, rw_ref, p_ref, wp_ref, g_ref, wpg_ref, y_ref):
    rw = rw_ref[...]
    moe = rw[:, 0:1] * ys0_ref[...] + rw[:, 1:2] * ys1_ref[...]
    x2 = x1_ref[...] + moe
    hn = x2 * lax.rsqrt(jnp.mean(x2 * x2, axis=-1, keepdims=True) + EPS) * g_ref[...]
    gate = _sigmoid(_dot(hn.astype(BF16), wpg_ref[...]))
    y_ref[...] = x2 + _dot(p_ref[...], wp_ref[...]) * gate


def _combine_ple(x1, y_slots, route_w, p, w_ple, gain, w_ple_gate, *, row0, rows, slot_rows):
    t, d = x1.shape
    tm = _row_tile(math.gcd(math.gcd(rows, row0), math.gcd(*slot_rows)), 256)
    r0, s0, s1 = row0 // tm, slot_rows[0] // tm, slot_rows[1] // tm
    row = lambda w: pl.BlockSpec((tm, w), lambda i: (r0 + i, 0))
    once = lambda shape: pl.BlockSpec(shape, lambda i: (0, 0), pipeline_mode=pl.Buffered(1))
    return pl.pallas_call(
        _combine_ple_kernel,
        out_shape=jax.ShapeDtypeStruct((rows, d), F32),
        grid=(rows // tm,),
        in_specs=[row(d), pl.BlockSpec((tm, d), lambda i: (s0 + i, 0)), pl.BlockSpec((tm, d), lambda i: (s1 + i, 0)),
                  row(LANES), row(p.shape[1]),
                  once(w_ple.shape), once((1, d)), once(w_ple_gate.shape)],
        out_specs=pl.BlockSpec((tm, d), lambda i: (i, 0)),
        compiler_params=_cparams("parallel"),
        name="combine_ple_r%d" % row0,
    )(x1, y_slots, y_slots, route_w, p, w_ple, gain, w_ple_gate)


def _pad_rows(a, rows):
    return jnp.pad(a, ((0, 0), (0, rows - a.shape[1]), (0, 0)))


def _layer(x_all, p_all, nb_p, seq_p, nb_s, seq_s, past_len, state_conv, state_delta, cache_k, cache_v,
           page_table, lam_init, lw):
    tp = nb_p * seq_p
    ts = nb_s * seq_s
    d = x_all.shape[1]
    heads_d = lw['a_log'].shape[0]
    dk = lw['delta_norm'].shape[0]
    hk = heads_d * dk
    conv_dim = 3 * hk
    comp = lw['q_norm'].shape[0]
    dv = 2 * comp
    width = lw['w_ob'].shape[0]
    heads_a = width // dv
    n_groups = lw['w_rg'].shape[1]
    n_experts = lw['w_re'].shape[1]
    assert seq_s <= 8 and conv_dim % 1024 == 0 and hk == 1024 and width == 1024 and d % 1024 == 0
    assert 2 * heads_d <= LANES and n_groups + n_experts <= LANES and lw['conv_w'].shape[0] == 4

    w_in = lw['w_in']
    ba0 = conv_dim + hk
    q0 = ba0 + 2 * heads_d
    gate0 = q0 + 3 * width
    w_main = jnp.concatenate([w_in[:, :ba0], w_in[:, gate0:], w_in[:, q0:gate0]], axis=1).astype(BF16)
    w_ba = jnp.pad(w_in[:, ba0:q0], ((0, 0), (0, LANES - 2 * heads_d))).astype(BF16)
    z_cb = conv_dim // hk
    gate_cb = (conv_dim + hk) // d
    q_cb = (conv_dim + hk + 2 * d) // width
    assert (conv_dim + hk) % d == 0 and (conv_dim + hk + 2 * d) % width == 0

    proj, ba = _inproj(x_all, lw['norm_attn'].reshape(1, d), w_main, w_ba)

    alog_row = jnp.pad(lw['a_log'], (heads_d, LANES - 2 * heads_d)).reshape(1, LANES)
    dtb_row = jnp.pad(lw['dt_bias'], (heads_d, LANES - 2 * heads_d)).reshape(1, LANES)
    dn = lw['delta_norm'].reshape(1, dk)
    chunk_p = math.gcd(seq_p, DELTA_CHUNK)
    oa_p, s_p = _delta_branch(proj, ba, 0, nb_p, seq_p,
                              jnp.zeros((nb_p, 8, conv_dim), F32), jnp.zeros((nb_p, heads_d, dk, dk), F32),
                              lw['conv_w'], alog_row, dtb_row, dn,
                              chunk=chunk_p, valid=chunk_p, heads=heads_d, dk=dk, z_col_block=z_cb)
    qkv_s = proj[tp:, :conv_dim].reshape(nb_s, seq_s, conv_dim)
    proj_s8 = _pad_rows(proj[tp:].reshape(nb_s, seq_s, -1), 8).reshape(nb_s * 8, -1)
    ba_s8 = _pad_rows(ba[tp:].reshape(nb_s, seq_s, LANES), 8).reshape(nb_s * 8, LANES)
    tail_s = jnp.pad(state_conv, ((0, 0), (8 - state_conv.shape[1], 0), (0, 0)))
    oa_s8, s_s = _delta_branch(proj_s8, ba_s8, 0, nb_s, 8, tail_s, state_delta,
                               lw['conv_w'], alog_row, dtb_row, dn,
                               chunk=8, valid=seq_s, heads=heads_d, dk=dk, z_col_block=z_cb)
    oa_s = oa_s8.reshape(nb_s, 8, hk)[:, :seq_s].reshape(ts, hk)
    o_a = jnp.concatenate([oa_p, oa_s], axis=0)
    conv_p = jnp.stack([proj[(b + 1) * seq_p - 3:(b + 1) * seq_p, :conv_dim] for b in range(nb_p)])
    conv_s = jnp.concatenate([state_conv, qkv_s], axis=1)[:, -3:, :]

    half = comp // 2
    inv_freq = ROPE_THETA ** (-jnp.arange(half, dtype=F32) / half)
    def rope_tables(pos):
        ang = pos.astype(F32)[:, None] * inv_freq[None, :]
        sin_h = jnp.sin(ang)
        return (jnp.tile(jnp.cos(ang), (1, LANES // half)),
                jnp.tile(jnp.concatenate([-sin_h, sin_h], axis=1), (1, LANES // comp)))

    q_gain = jnp.tile(lw['q_norm'], LANES // comp).reshape(1, LANES)
    k_gain = jnp.tile(lw['k_norm'], LANES // comp).reshape(1, LANES)
    tk = _row_tile(seq_p, 512, align=LANES)
    q_t, k_bf, v_t, k_t = _qkprep(proj, 0, nb_p, seq_p, *rope_tables(jnp.arange(seq_p, dtype=jnp.int32)),
                                  q_gain, k_gain, q_col_block=q_cb, width=width, comp=comp, for_prompt=True, tk=tk)
    pos_s = jnp.tile(past_len + jnp.arange(seq_s, dtype=jnp.int32), nb_s)
    q_fs, k_fs = _qkprep(proj, tp, 1, ts, *rope_tables(pos_s), q_gain, k_gain,
                         q_col_block=q_cb, width=width, comp=comp, for_prompt=False)
    lam_vecs = jnp.stack([lw['lam_q1'], lw['lam_k1'], lw['lam_q2'], lw['lam_k2']])
    subln = lw['subln'].reshape(1, dv)
    ob_p = _attn_prompt(q_t, k_bf, v_t, lam_vecs, lw['subln'].reshape(dv, 1), nb=nb_p, seq=seq_p, tk=tk,
                        heads=heads_a, comp=comp, lam_init=lam_init)
    v_f_s = proj[tp:, (q_cb + 2) * width:(q_cb + 3) * width]
    q8 = _pad_rows(q_fs.reshape(nb_s, seq_s, width), 8)
    k8 = _pad_rows(k_fs.reshape(nb_s, seq_s, width), 8)
    v8 = _pad_rows(v_f_s.reshape(nb_s, seq_s, width), 8)
    n_phys, page = cache_k.shape[0], cache_k.shape[1]
    cache_kt = jnp.transpose(cache_k, (0, 2, 3, 4, 1)).reshape(n_phys, width, page)
    cache_v2 = cache_v.reshape(n_phys, page * heads_a, dv)
    ob_s8 = _attn_sample(q8, k8, v8, cache_kt, cache_v2, page_table, lam_vecs, subln,
                         new_len=seq_s, heads=heads_a, comp=comp, lam_init=lam_init)
    o_b = jnp.concatenate([ob_p, ob_s8[:, :seq_s].reshape(ts, width)], axis=0)

    merged = _merge(o_a, o_b, proj, lw['w_oa'].astype(BF16), lw['w_ob'].astype(BF16), gate_col_block=gate_cb)
    w_route = jnp.pad(jnp.concatenate([lw['w_rg'], lw['w_re']], axis=1),
                      ((0, 0), (0, LANES - n_groups - n_experts))).astype(BF16)
    b_route = jnp.pad(jnp.concatenate([lw['b_rg'], lw['b_re']]), (0, LANES - n_groups - n_experts)).reshape(1, LANES)
    x1, h2, route_w, route_i, route_cnt = _outproj_route(x_all, merged, lw['w_out'].astype(BF16),
                                                         lw['norm_ffn'].reshape(1, d), w_route, b_route,
                                                         n_groups=n_groups, per_group=n_experts // n_groups)

    counts = route_cnt[0, n_groups:n_groups + n_experts].astype(jnp.int32)
    block_e, src_tok, dst_slot, nused = _expert_plan(route_i[:, :TOP_K], route_i[:, TOP_K:2 * TOP_K], counts, tp)
    y_slots = _experts(h2, block_e, src_tok, dst_slot, nused, lw['exp_gate'], lw['exp_up'], lw['exp_down'])

    ple_args = (x1, y_slots, route_w, p_all.astype(BF16), lw['w_ple'].astype(BF16),
                lw['norm_ple'].reshape(1, d), lw['w_ple_gate'].astype(BF16))
    y_p = _combine_ple(*ple_args, row0=0, rows=tp, slot_rows=(0, tp))
    y_s = _combine_ple(*ple_args, row0=tp, rows=ts, slot_rows=(TOP_K * tp, TOP_K * tp + ts))
    y = (y_p, y_s)

    k_p = jnp.transpose(k_t.reshape(nb_p, heads_a, 2, comp, seq_p), (0, 4, 1, 2, 3))
    v_p = proj[:tp, (q_cb + 2) * width:(q_cb + 3) * width].reshape(nb_p, seq_p, heads_a, dv)
    k_s = k_fs.reshape(nb_s, seq_s, heads_a, 2, comp)
    v_s = v_f_s.reshape(nb_s, seq_s, heads_a, dv)
    return y, k_p, v_p, k_s, v_s, conv_p, conv_s, s_p, s_s


def kernel(x_prompt, x_sample, cache_k, cache_v, state_conv, state_delta, page_table, p_prompt, p_sample,
           norm_attn, w_in, conv_w, a_log, dt_bias, delta_norm, q_norm, k_norm, lam_q1, lam_k1, lam_q2, lam_k2,
           subln, w_oa, w_ob, w_out, norm_ffn, w_rg, b_rg, w_re, b_re, exp_gate, exp_up, exp_down,
           norm_ple, w_ple, w_ple_gate):
    nb_p, seq_p, d = x_prompt.shape
    nb_s, seq_s, _ = x_sample.shape
    tp = nb_p * seq_p
    depth = w_in.shape[0]
    past_len = page_table.shape[1] * cache_k.shape[2]
    y_p, y_s = x_prompt.reshape(tp, d), x_sample.reshape(nb_s * seq_s, d)
    outs = [[] for _ in range(8)]
    for i in range(depth):
        x_all = jnp.concatenate([y_p, y_s], axis=0)
        lw = dict(norm_attn=norm_attn[i], w_in=w_in[i], conv_w=conv_w[i], a_log=a_log[i], dt_bias=dt_bias[i],
                  delta_norm=delta_norm[i], q_norm=q_norm[i], k_norm=k_norm[i], lam_q1=lam_q1[i], lam_k1=lam_k1[i],
                  lam_q2=lam_q2[i], lam_k2=lam_k2[i], subln=subln[i], w_oa=w_oa[i], w_ob=w_ob[i], w_out=w_out[i],
                  norm_ffn=norm_ffn[i], w_rg=w_rg[i], b_rg=b_rg[i], w_re=w_re[i], b_re=b_re[i],
                  exp_gate=exp_gate[i], exp_up=exp_up[i], exp_down=exp_down[i],
                  norm_ple=norm_ple[i], w_ple=w_ple[i], w_ple_gate=w_ple_gate[i])
        lam_init = 0.8 - 0.6 * math.exp(-0.3 * i)
        p_all = jnp.concatenate([p_prompt[i].reshape(tp, -1), p_sample[i].reshape(nb_s * seq_s, -1)], axis=0)
        res = _layer(x_all, p_all, nb_p, seq_p, nb_s, seq_s, past_len, state_conv[i], state_delta[i],
                     cache_k[i], cache_v[i], page_table, lam_init, lw)
        y_p, y_s = res[0]
        for lst, val in zip(outs, res[1:]):
            lst.append(val)
    return (y_p.reshape(nb_p, seq_p, d), y_s.reshape(nb_s, seq_s, d)) + tuple(jnp.stack(lst) for lst in outs)
```

```python
import functools
import math

import jax
import jax.numpy as jnp
from jax import lax
from jax.experimental import pallas as pl
from jax.experimental.pallas import tpu as pltpu

F32 = jnp.float32
BF16 = jnp.bfloat16
HIGHEST = lax.Precision.HIGHEST

LANES = 128
DELTA_CHUNK = 64
ROPE_THETA = 10000.0
TOP_K = 2
MOE_BLOCK = 128
EPS = 1e-6
VMEM_LIMIT = 48 * 1024 * 1024


def _cparams(*sem):
    return pltpu.CompilerParams(dimension_semantics=sem, vmem_limit_bytes=VMEM_LIMIT)


def _row_tile(n, target, align=16):
    best = None
    for t in range(align, min(n, target) + 1, align):
        if n % t == 0:
            best = t
    assert best is not None, (n, target, align)
    return best


def _nt_dot(a, b, precision=None):
    return lax.dot_general(a, b, (((1,), (1,)), ((), ())), precision=precision,
                           preferred_element_type=F32)


def _tn_dot(a, b, precision=None):
    return lax.dot_general(a, b, (((0,), (0,)), ((), ())), precision=precision,
                           preferred_element_type=F32)


def _dot(a, b, precision=None):
    return jnp.dot(a, b, precision=precision, preferred_element_type=F32)


def _split_bf16(a):
    hi = a.astype(BF16)
    return hi, (a - hi.astype(F32)).astype(BF16)


def _dot_split(a, b):
    return _dot(a[0], b[0]) + (_dot(a[0], b[1]) + _dot(a[1], b[0]))


def _sigmoid(x):
    return 1.0 / (1.0 + jnp.exp(-x))


def _silu(x):
    return x * _sigmoid(x)


def _iota(shape, dim):
    return lax.broadcasted_iota(jnp.int32, shape, dim)


def _inproj_kernel(x_ref, g_ref, *rest, part_tiles):
    w_refs = rest[:len(part_tiles)]
    wba_ref, p_ref, ba_ref, h_scr = rest[len(part_tiles):]
    j = pl.program_id(1)

    @pl.when(j == 0)
    def _():
        x = x_ref[...]
        h = x * lax.rsqrt(jnp.mean(x * x, axis=-1, keepdims=True) + EPS) * g_ref[...]
        h_scr[...] = h.astype(BF16)
        ba_ref[...] = _dot(h_scr[...], wba_ref[...])

    lo = 0
    for w_ref, n_tiles in zip(w_refs, part_tiles):
        @pl.when((j >= lo) & (j < lo + n_tiles))
        def _(w_ref=w_ref):
            p_ref[...] = _dot(h_scr[...], w_ref[...])
        lo += n_tiles


def _inproj(x, gain, w_parts, w_ba):
    t, d = x.shape
    tm = _row_tile(t, 640)
    tn = 1024
    part_tiles = tuple(w.shape[1] // tn for w in w_parts)
    assert all(w.shape[1] % tn == 0 for w in w_parts)
    n_tiles = sum(part_tiles)
    starts = [sum(part_tiles[:k]) for k in range(len(w_parts))]

    def w_spec(k):
        return pl.BlockSpec((d, tn), lambda i, j: (0, jnp.clip(j - starts[k], 0, part_tiles[k] - 1)))

    return pl.pallas_call(
        functools.partial(_inproj_kernel, part_tiles=part_tiles),
        out_shape=(jax.ShapeDtypeStruct((t, n_tiles * tn), F32), jax.ShapeDtypeStruct((t, LANES), F32)),
        grid=(t // tm, n_tiles),
        in_specs=[pl.BlockSpec((tm, d), lambda i, j: (i, 0)),
                  pl.BlockSpec((1, d), lambda i, j: (0, 0))]
                 + [w_spec(k) for k in range(len(w_parts))]
                 + [pl.BlockSpec((d, LANES), lambda i, j: (0, 0))],
        out_specs=(pl.BlockSpec((tm, tn), lambda i, j: (i, j)),
                   pl.BlockSpec((tm, LANES), lambda i, j: (i, 0))),
        scratch_shapes=[pltpu.VMEM((tm, d), BF16)],
        compiler_params=_cparams("parallel", "arbitrary"),
        name="inproj",
    )(x, gain, *w_parts, w_ba)


def _delta_kernel(qkv_ref, z_ref, ba_ref, tail0_ref, s0_ref, cw_ref, alog_ref, dtb_ref, dn_ref,
                  o_ref, sfin_ref, ext_scr, s_scr, *, chunk, valid, heads, dk):
    n = pl.program_id(1)
    c = chunk
    hk = heads * dk

    @pl.when(n == 0)
    def _():
        ext_scr[0:8, :] = tail0_ref[0]
        s_scr[...] = s0_ref[0]

    ext_scr[8:8 + c, :] = qkv_ref[...]

    row = _iota((c, 1), 0)
    rowmask = (row < valid).astype(F32) if valid < c else None

    ba = ba_ref[...]
    beta_all = _sigmoid(ba)
    xg = ba + dtb_ref[...]
    softplus = jnp.maximum(xg, 0.0) + jnp.log1p(jnp.exp(-jnp.abs(xg)))
    g_all = -jnp.exp(alog_ref[...]) * softplus
    if rowmask is not None:
        beta_all = beta_all * rowmask
        g_all = g_all * rowmask
    r_i = _iota((c, c), 0)
    c_i = _iota((c, c), 1)
    incl = r_i >= c_i
    strict = r_i > c_i
    gc_all = _dot(incl.astype(F32), g_all, precision=HIGHEST)
    gc_t = jnp.transpose(gc_all)
    eye = (r_i == c_i).astype(F32)

    def conv(col0):
        sl = slice(col0, col0 + dk)
        acc = ext_scr[8:8 + c, sl] * cw_ref[3:4, sl]
        acc = acc + ext_scr[7:7 + c, sl] * cw_ref[2:3, sl]
        acc = acc + ext_scr[6:6 + c, sl] * cw_ref[1:2, sl]
        acc = acc + ext_scr[5:5 + c, sl] * cw_ref[0:1, sl]
        return _silu(acc)

    hs = range(heads)
    q, k, v, beta, gc, egc, g_last, decay = [], [], [], [], [], [], [], []
    for h in hs:
        qh = conv(h * dk)
        kh = conv(hk + h * dk)
        vh = conv(2 * hk + h * dk)
        qh = qh * lax.rsqrt(jnp.sum(qh * qh, axis=-1, keepdims=True) + EPS) * (dk ** -0.5)
        kh = kh * lax.rsqrt(jnp.sum(kh * kh, axis=-1, keepdims=True) + EPS)
        if rowmask is not None:
            qh, kh, vh = qh * rowmask, kh * rowmask, vh * rowmask
        q.append(qh)
        k.append(kh)
        v.append(vh)
        beta.append(beta_all[:, h:h + 1])
        gc.append(gc_all[:, heads + h:heads + h + 1])
        egc.append(jnp.exp(gc[h]))
        g_last.append(gc_all[c - 1:c, heads + h:heads + h + 1])
        gc_row = gc_t[heads + h:heads + h + 1, :]
        decay.append(jnp.where(incl, jnp.exp(jnp.where(incl, gc[h] - gc_row, 0.0)), 0.0))
    kb = [k[h] * beta[h] for h in hs]
    kk = [_nt_dot(kb[h], k[h]) for h in hs]
    qk = [_nt_dot(q[h], k[h]) for h in hs]
    apow = [jnp.where(strict, kk[h] * decay[h], 0.0) for h in hs]
    qk = [qk[h] * decay[h] for h in hs]
    a_mat = apow
    same_block = lambda b: (r_i // b) == (c_i // b)
    b0 = min(8, c)
    diag0 = same_block(b0)
    apb = [jnp.where(diag0, a_mat[h], 0.0) for h in hs]
    tinv = [eye - apb[h] for h in hs]
    apb = [apb[h].astype(BF16) for h in hs]
    span = 2
    while span < b0:
        apb = [_dot(apb[h], apb[h]).astype(BF16) for h in hs]
        tinv = [tinv[h] + _dot(tinv[h].astype(BF16), apb[h]) for h in hs]
        span *= 2
    b = b0
    while b < c:
        ring = same_block(2 * b) & ((r_i // b) != (c_i // b))
        off = [jnp.where(ring, a_mat[h], 0.0).astype(BF16) for h in hs]
        tb = [tinv[h].astype(BF16) for h in hs]
        tl = [_dot(tb[h], off[h]).astype(BF16) for h in hs]
        tinv = [tinv[h] - _dot(tl[h], tb[h]) for h in hs]
        b *= 2
    resid = [eye - _dot_split(_split_bf16(eye + a_mat[h]), _split_bf16(tinv[h])) for h in hs]
    tinv = [tinv[h] + _dot(tinv[h].astype(BF16), resid[h].astype(BF16)) for h in hs]
    u = [_dot(tinv[h], v[h] * beta[h]) for h in hs]
    w = [_dot(tinv[h], kb[h] * egc[h]) for h in hs]
    s = [s_scr[h] for h in hs]
    ws = [_dot(w[h], s[h]) for h in hs]
    qs = [_dot(q[h] * egc[h], s[h]) for h in hs]
    v_new = [u[h] - ws[h] for h in hs]
    o = [qs[h] + _dot(qk[h], v_new[h]) for h in hs]
    ds = [_tn_dot(k[h] * jnp.exp(g_last[h] - gc[h]), v_new[h]) for h in hs]
    for h in hs:
        s_scr[h] = s[h] * jnp.exp(g_last[h]) + ds[h]
        on = o[h] * lax.rsqrt(jnp.mean(o[h] * o[h], axis=-1, keepdims=True) + EPS) * dn_ref[...]
        zh = z_ref[:, h * dk:(h + 1) * dk]
        o_ref[:, h * dk:(h + 1) * dk] = (on * _silu(zh)).astype(o_ref.dtype)

    ext_scr[0:8, :] = ext_scr[c:c + 8, :]

    @pl.when(n == pl.num_programs(1) - 1)
    def _():
        sfin_ref[0] = s_scr[...]


def _delta_branch(proj, ba, row0, nb, seq, tail0, s0, conv_w, alog_row, dtb_row, delta_norm, *,
                  chunk, valid, heads, dk, z_col_block):
    hk = heads * dk
    nchunks = seq // chunk
    assert seq % chunk == 0 and row0 % chunk == 0
    r0 = row0 // chunk
    kern = functools.partial(_delta_kernel, chunk=chunk, valid=valid, heads=heads, dk=dk)
    return pl.pallas_call(
        kern,
        out_shape=(jax.ShapeDtypeStruct((nb * seq, hk), BF16),
                   jax.ShapeDtypeStruct((nb, heads, dk, dk), F32)),
        grid=(nb, nchunks),
        in_specs=[pl.BlockSpec((chunk, 3 * hk), lambda b, n: (r0 + b * nchunks + n, 0)),
                  pl.BlockSpec((chunk, hk), lambda b, n: (r0 + b * nchunks + n, z_col_block)),
                  pl.BlockSpec((chunk, LANES), lambda b, n: (r0 + b * nchunks + n, 0)),
                  pl.BlockSpec((1, 8, 3 * hk), lambda b, n: (b, 0, 0)),
                  pl.BlockSpec((1, heads, dk, dk), lambda b, n: (b, 0, 0, 0)),
                  pl.BlockSpec((4, 3 * hk), lambda b, n: (0, 0)),
                  pl.BlockSpec((1, LANES), lambda b, n: (0, 0)),
                  pl.BlockSpec((1, LANES), lambda b, n: (0, 0)),
                  pl.BlockSpec((1, dk), lambda b, n: (0, 0))],
        out_specs=(pl.BlockSpec((chunk, hk), lambda b, n: (b * nchunks + n, 0)),
                   pl.BlockSpec((1, heads, dk, dk), lambda b, n: (b, 0, 0, 0))),
        scratch_shapes=[pltpu.VMEM((8 + chunk, 3 * hk), F32),
                        pltpu.VMEM((heads, dk, dk), F32)],
        compiler_params=_cparams("parallel", "arbitrary"),
        name="delta_c%d" % chunk,
    )(proj, proj, ba, tail0, s0, conv_w, alog_row, dtb_row, delta_norm)


def _qkprep_kernel(q_ref, k_ref, v_ref, cos_ref, sin_ref, qg_ref, kg_ref, *out_refs, comp, scale, for_prompt, tk):
    tm = q_ref.shape[0]
    r_i = _iota((LANES, LANES), 0)
    c_i = _iota((LANES, LANES), 1)
    group = (r_i // comp == c_i // comp).astype(BF16)
    lane = _iota((tm, LANES), 1)
    first_half = (lane % comp) < (comp // 2)
    cos = cos_ref[...]
    sin = sin_ref[...]

    def norm_rope(x, gain):
        sq = x * x
        hi = sq.astype(BF16)
        lo = (sq - hi.astype(F32)).astype(BF16)
        ms = (_dot(hi, group) + _dot(lo, group)) * (1.0 / comp)
        y = x * lax.rsqrt(ms + EPS) * gain
        swapped = jnp.where(first_half, pltpu.roll(y, LANES - comp // 2, 1), pltpu.roll(y, comp // 2, 1))
        return y * cos + swapped * sin

    for j in range(q_ref.shape[1] // LANES):
        sl = slice(j * LANES, (j + 1) * LANES)
        qr = norm_rope(q_ref[:, sl], qg_ref[...]) * scale
        kr = norm_rope(k_ref[:, sl], kg_ref[...])
        if for_prompt:
            qt_ref, kbo_ref, vt_ref, kt_ref = out_refs
            qt_ref[0, sl, :] = jnp.transpose(qr).astype(BF16)
            kbo_ref[:, sl] = kr.astype(BF16)
            kt_ref[0, sl, :] = jnp.transpose(kr)
            vt = jnp.transpose(v_ref[:, sl]).astype(BF16)
            for u in range(tm // tk):
                vt_ref[0, u, sl, :] = vt[:, u * tk:(u + 1) * tk]
        else:
            qo_ref, kfo_ref = out_refs
            qo_ref[:, sl] = qr
            kfo_ref[:, sl] = kr


def _qkprep(proj, row0, nb, seq, cos_t, sin_t, q_gain, k_gain, *, q_col_block, width, comp, for_prompt, tk=None):
    tm = _row_tile(seq, 512, align=LANES) if for_prompt else seq
    assert row0 % tm == 0 and seq % tm == 0 and (not for_prompt or tm % tk == 0)
    r0, per_seq = row0 // tm, seq // tm
    scale = comp ** -0.5 * (math.log2(math.e) if for_prompt else 1.0)
    kern = functools.partial(_qkprep_kernel, comp=comp, scale=scale, for_prompt=for_prompt, tk=tk)
    col = lambda cb: pl.BlockSpec((tm, width), lambda i: (r0 + i, cb))
    small = pl.BlockSpec((tm, LANES), lambda i: (i % per_seq, 0))
    gain = pl.BlockSpec((1, LANES), lambda i: (0, 0))
    out = pl.BlockSpec((tm, width), lambda i: (i, 0))
    rows = nb * seq
    if for_prompt:
        out_t = pl.BlockSpec((1, width, tm), lambda i: (i // per_seq, 0, i % per_seq))
        out_shape = (jax.ShapeDtypeStruct((nb, width, seq), BF16), jax.ShapeDtypeStruct((rows, width), BF16),
                     jax.ShapeDtypeStruct((nb, seq // tk, width, tk), BF16),
                     jax.ShapeDtypeStruct((nb, width, seq), F32))
        out_specs = (out_t, out,
                     pl.BlockSpec((1, tm // tk, width, tk), lambda i: (i // per_seq, i % per_seq, 0, 0)), out_t)
    else:
        out_shape = (jax.ShapeDtypeStruct((rows, width), F32), jax.ShapeDtypeStruct((rows, width), F32))
        out_specs = (out, out)
    return pl.pallas_call(
        kern,
        out_shape=out_shape,
        grid=(rows // tm,),
        in_specs=[col(q_col_block), col(q_col_block + 1), col(q_col_block + 2), small, small, gain, gain],
        out_specs=out_specs,
        compiler_params=_cparams("parallel"),
        name="qkprep_prompt" if for_prompt else "qkprep_sample",
    )(proj, proj, proj, cos_t, sin_t, q_gain, k_gain)


def _lambda_value(lam_ref, lam_init):
    lv = lam_ref[...]
    s1 = jnp.sum(lv[0:1] * lv[1:2], axis=-1, keepdims=True)
    s2 = jnp.sum(lv[2:3] * lv[3:4], axis=-1, keepdims=True)
    return jnp.exp(s1) - jnp.exp(s2) + lam_init


def _attn_prompt_kernel(q_ref, k_ref, v_ref, lam_ref, sub_ref, o_ref, *, tq, tk, comp, heads_per_step, lam_init):
    i = pl.program_id(2)
    dv = 2 * comp
    hr = range(heads_per_step)
    row = _iota((dv, tq), 0)
    qs = []
    for hh in hr:
        qt = q_ref[0, hh * dv:(hh + 1) * dv, :]
        zero = jnp.zeros_like(qt)
        qs.append(jnp.concatenate([jnp.where(row < comp, qt, zero), jnp.where(row >= comp, qt, zero)], axis=1))

    def update(carry, j, mask):
        start = pl.multiple_of(j * tk, tk)
        kt = [k_ref[pl.ds(start, tk), hh * dv:(hh + 1) * dv] for hh in hr]
        vt = [v_ref[0, j, hh * dv:(hh + 1) * dv, :] for hh in hr]
        s = [_dot(kt[hh], qs[hh]) for hh in hr]
        if mask is not None:
            s = [jnp.where(mask, sn, -jnp.inf) for sn in s]
        m_new = [jnp.maximum(carry[hh][0], jnp.max(s[hh], axis=0, keepdims=True)) for hh in hr]
        p = [jnp.exp2(s[hh] - m_new[hh]) for hh in hr]
        corr = [jnp.exp2(carry[hh][0] - m_new[hh]) for hh in hr]
        l = [carry[hh][1] * corr[hh] + jnp.sum(p[hh], axis=0, keepdims=True) for hh in hr]
        pv = [_dot(vt[hh], p[hh].astype(BF16)) for hh in hr]
        return tuple((m_new[hh], l[hh], carry[hh][2] * corr[hh] + pv[hh]) for hh in hr)

    def body(j, carry):
        return update(carry, j, None)

    init = tuple((jnp.full((1, 2 * tq), -jnp.inf, F32), jnp.zeros((1, 2 * tq), F32),
                  jnp.zeros((dv, 2 * tq), F32)) for _ in hr)
    n_full = (i * tq) // tk
    carry = lax.fori_loop(0, n_full, body, init)
    kpos = n_full * tk + _iota((tk, 2 * tq), 0)
    qpos = i * tq + _iota((tk, 2 * tq), 1) % tq
    carry = update(carry, n_full, qpos >= kpos)
    lam = _lambda_value(lam_ref, lam_init)
    for hh in hr:
        _, l, a = carry[hh]
        n = a / l
        o = n[:, 0:tq] - lam * n[:, tq:2 * tq]
        on = o * lax.rsqrt(jnp.mean(o * o, axis=0, keepdims=True) + EPS) * sub_ref[...]
        o_ref[:, hh * dv:(hh + 1) * dv] = jnp.transpose(on * (1.0 - lam_init)).astype(o_ref.dtype)


def _attn_prompt(q_t, k, v_t, lam_vecs, subln_col, *, nb, seq, tk, heads, comp, lam_init):
    dv = 2 * comp
    tq = _row_tile(seq, 256, align=LANES)
    assert tk % tq == 0 and seq % tk == 0
    nq = seq // tq
    hps = 2
    assert heads % hps == 0
    kern = functools.partial(_attn_prompt_kernel, tq=tq, tk=tk, comp=comp, heads_per_step=hps, lam_init=lam_init)
    return pl.pallas_call(
        kern,
        out_shape=jax.ShapeDtypeStruct((nb * seq, heads * dv), BF16),
        grid=(nb, heads // hps, nq),
        in_specs=[pl.BlockSpec((1, hps * dv, tq), lambda b, h, i: (b, h, i)),
                  pl.BlockSpec((seq, hps * dv), lambda b, h, i: (b, h)),
                  pl.BlockSpec((1, seq // tk, hps * dv, tk), lambda b, h, i: (b, 0, h, 0)),
                  pl.BlockSpec((4, comp), lambda b, h, i: (0, 0)),
                  pl.BlockSpec((dv, 1), lambda b, h, i: (0, 0))],
        out_specs=pl.BlockSpec((tq, hps * dv), lambda b, h, i: (b * nq + i, h)),
        compiler_params=_cparams("parallel", "parallel", "arbitrary"),
        name="attn_prompt",
    )(q_t, k, v_t, lam_vecs, subln_col)


def _attn_sample_kernel(pt_ref, q_ref, kn_ref, vn_ref, lam_ref, sub_ref, *rest,
                        pages_per_step, new_len, heads, comp, lam_init):
    pp = pages_per_step
    kt_refs = rest[:pp]
    v_refs = rest[pp:2 * pp]
    o_ref = rest[2 * pp]
    qbd_scr, m_scr, l_scr, acc_scr = rest[2 * pp + 1:]
    p_id = pl.program_id(1)
    rph = 2 * new_len
    rows = heads * rph
    dv = 2 * comp
    width = heads * dv
    page = kt_refs[0].shape[2]

    @pl.when(p_id == 0)
    def _():
        q = q_ref[0][0:new_len, :]
        lane_grp = _iota((new_len, width), 1) // comp
        for h in range(heads):
            for c in range(2):
                r0 = h * rph + c * new_len
                qbd_scr[r0:r0 + new_len, :] = jnp.where(lane_grp == 2 * h + c, q, 0.0)
        m_scr[...] = jnp.full(m_scr.shape, -jnp.inf, F32)
        l_scr[...] = jnp.zeros(l_scr.shape, F32)
        acc_scr[...] = jnp.zeros(acc_scr.shape, F32)

    qbd = qbd_scr[...]

    def online(s_list, v_of):
        m = m_scr[...]
        m_new = m
        for s in s_list:
            m_new = jnp.maximum(m_new, jnp.max(s, axis=-1, keepdims=True))
        corr = jnp.exp(m - m_new)
        p_list = [jnp.exp(s - m_new) for s in s_list]
        l = l_scr[...] * corr
        for p in p_list:
            l = l + jnp.sum(p, axis=-1, keepdims=True)
        m_scr[...] = m_new
        l_scr[...] = l
        for h in range(heads):
            hs = slice(h * rph, (h + 1) * rph)
            acc = acc_scr[hs, :] * corr[hs, :]
            for j, p in enumerate(p_list):
                acc = acc + _dot(p[hs, :], v_of(j, h))
            acc_scr[hs, :] = acc

    online([_dot(qbd, kt_refs[r][0]) for r in range(pp)],
           lambda j, h: v_refs[j][0, pl.ds(h, page, stride=heads), :])

    @pl.when(p_id == pl.num_programs(1) - 1)
    def _():
        s = _nt_dot(qbd, kn_ref[0])
        tok = _iota((rows, 8), 0) % new_len
        key = _iota((rows, 8), 1)
        s = jnp.where((tok >= key) & (key < new_len), s, -jnp.inf)
        online([s], lambda j, h: vn_ref[0][:, h * dv:(h + 1) * dv])
        lam = _lambda_value(lam_ref, lam_init)
        nrm = acc_scr[...] / l_scr[...]
        for h in range(heads):
            r0 = h * rph
            oh = nrm[r0:r0 + new_len, :] - lam * nrm[r0 + new_len:r0 + rph, :]
            on = oh * lax.rsqrt(jnp.mean(oh * oh, axis=-1, keepdims=True) + EPS) * sub_ref[...]
            o_ref[0, 0:new_len, h * dv:(h + 1) * dv] = (on * (1.0 - lam_init)).astype(o_ref.dtype)
        if new_len < 8:
            o_ref[0, new_len:8, :] = jnp.zeros((8 - new_len, width), o_ref.dtype)


def _attn_sample(q8, k8, v8, cache_kt, cache_v2, page_table, lam_vecs, subln, *, new_len, heads, comp, lam_init):
    nb, n_pages = page_table.shape
    width, page = cache_kt.shape[1], cache_kt.shape[2]
    dv = 2 * comp
    pp = 8
    assert n_pages % pp == 0 and 2 * new_len == 8
    rows = heads * 2 * new_len
    kern = functools.partial(_attn_sample_kernel, pages_per_step=pp, new_len=new_len, heads=heads,
                             comp=comp, lam_init=lam_init)
    per_req = pl.BlockSpec((1, 8, width), lambda b, p, pt: (b, 0, 0))

    def page_spec(r, shape):
        return pl.BlockSpec((1,) + shape, lambda b, p, pt: (pt[b * n_pages + p * pp + r], 0, 0))

    grid_spec = pltpu.PrefetchScalarGridSpec(
        num_scalar_prefetch=1,
        grid=(nb, n_pages // pp),
        in_specs=[per_req, per_req, per_req,
                  pl.BlockSpec((4, comp), lambda b, p, pt: (0, 0)),
                  pl.BlockSpec((1, dv), lambda b, p, pt: (0, 0))]
                 + [page_spec(r, (width, page)) for r in range(pp)]
                 + [page_spec(r, (page * heads, dv)) for r in range(pp)],
        out_specs=pl.BlockSpec((1, 8, width), lambda b, p, pt: (b, 0, 0)),
        scratch_shapes=[pltpu.VMEM((rows, width), F32), pltpu.VMEM((rows, 1), F32),
                        pltpu.VMEM((rows, 1), F32), pltpu.VMEM((rows, dv), F32)],
    )
    return pl.pallas_call(
        kern,
        out_shape=jax.ShapeDtypeStruct((nb, 8, width), BF16),
        grid_spec=grid_spec,
        compiler_params=_cparams("parallel", "arbitrary"),
        name="attn_sample",
    )(page_table.reshape(-1), q8, k8, v8, lam_vecs, subln, *([cache_kt] * pp), *([cache_v2] * pp))


def _merge_kernel(oa_ref, ob_ref, ga_ref, gb_ref, woa_ref, wob_ref, m_ref):
    ua = _dot(oa_ref[...], woa_ref[...])
    ub = _dot(ob_ref[...], wob_ref[...])
    m_ref[...] = (_sigmoid(ga_ref[...]) * ua + _sigmoid(gb_ref[...]) * ub).astype(m_ref.dtype)


def _merge(o_a, o_b, proj, w_oa, w_ob, *, gate_col_block):
    t, zd = o_a.shape
    d = w_oa.shape[1]
    tm = _row_tile(t, 640)
    once = lambda shape: pl.BlockSpec(shape, lambda i: (0, 0), pipeline_mode=pl.Buffered(1))
    return pl.pallas_call(
        _merge_kernel,
        out_shape=jax.ShapeDtypeStruct((t, d), BF16),
        grid=(t // tm,),
        in_specs=[pl.BlockSpec((tm, zd), lambda i: (i, 0)),
                  pl.BlockSpec((tm, o_b.shape[1]), lambda i: (i, 0)),
                  pl.BlockSpec((tm, d), lambda i: (i, gate_col_block)),
                  pl.BlockSpec((tm, d), lambda i: (i, gate_col_block + 1)),
                  once(w_oa.shape), once(w_ob.shape)],
        out_specs=pl.BlockSpec((tm, d), lambda i: (i, 0)),
        compiler_params=_cparams("parallel"),
        name="merge",
    )(o_a, o_b, proj, proj, w_oa, w_ob)


def _outproj_route_kernel(x_ref, m_ref, wo_ref, g_ref, wr_ref, br_ref,
                          x1_ref, h2_ref, rw_ref, re_ref, cnt_ref, *, n_groups, per_group):
    x1 = x_ref[...] + _dot(m_ref[...], wo_ref[...])
    x1_ref[...] = x1
    h2 = x1 * lax.rsqrt(jnp.mean(x1 * x1, axis=-1, keepdims=True) + EPS) * g_ref[...]
    nc = h2.shape[1] // LANES
    for c in range(nc):
        h2_ref[pl.ds(c, h2.shape[0], stride=nc), :] = h2[:, c * LANES:(c + 1) * LANES]
    logits = _dot(h2.astype(BF16), wr_ref[...]) + br_ref[...]
    tm = logits.shape[0]
    lane = _iota((tm, LANES), 1).astype(F32)
    big = float(LANES)

    def masked_softmax(mask):
        lm = jnp.where(mask, logits, -jnp.inf)
        e = jnp.exp(lm - jnp.max(lm, axis=-1, keepdims=True))
        return e / jnp.sum(e, axis=-1, keepdims=True)

    def top1(p, mask):
        v = jnp.max(jnp.where(mask, p, -1.0), axis=-1, keepdims=True)
        idx = jnp.min(jnp.where(mask & (p == v), lane, big), axis=-1, keepdims=True)
        return v, idx

    gmask = lane < n_groups
    pg = masked_softmax(gmask)
    g_w, g_top = top1(pg, gmask)
    lo = n_groups + g_top * per_group
    emask = (lane >= lo) & (lane < lo + per_group)
    pe = masked_softmax(emask)
    v1, i1 = top1(pe, emask)
    emask2 = emask & (lane != i1)
    v2, i2 = top1(pe, emask2)
    denom = v1 + v2
    w1 = g_w * v1 / denom
    w2 = g_w * v2 / denom
    rw_ref[...] = jnp.where(lane == 0.0, w1, jnp.where(lane == 1.0, w2, 0.0))

    @pl.when(pl.program_id(0) == 0)
    def _():
        cnt_ref[...] = jnp.zeros(cnt_ref.shape, F32)

    sel1 = lane == i1
    sel2 = lane == i2
    picked = jnp.where(sel1 | sel2, 1.0, 0.0)
    before = (_iota((tm, tm), 0) > _iota((tm, tm), 1)).astype(BF16)
    prior = _dot(before, picked.astype(BF16)) + cnt_ref[...]
    r1 = jnp.sum(jnp.where(sel1, prior, 0.0), axis=-1, keepdims=True)
    r2 = jnp.sum(jnp.where(sel2, prior, 0.0), axis=-1, keepdims=True)
    cnt_ref[...] = cnt_ref[...] + jnp.sum(picked, axis=0, keepdims=True)
    re_ref[...] = jnp.where(lane == 0.0, i1 - n_groups,
                            jnp.where(lane == 1.0, i2 - n_groups,
                                      jnp.where(lane == 2.0, r1, jnp.where(lane == 3.0, r2, 0.0)))).astype(jnp.int32)


def _outproj_route(x, merged, w_out, gain, w_route, b_route, *, n_groups, per_group):
    t, d = x.shape
    tm = _row_tile(t, 320)
    kern = functools.partial(_outproj_route_kernel, n_groups=n_groups, per_group=per_group)
    row = lambda w: pl.BlockSpec((tm, w), lambda i: (i, 0))
    return pl.pallas_call(
        kern,
        out_shape=(jax.ShapeDtypeStruct((t, d), F32), jax.ShapeDtypeStruct((t * (d // LANES), LANES), F32),
                   jax.ShapeDtypeStruct((t, LANES), F32), jax.ShapeDtypeStruct((t, LANES), jnp.int32),
                   jax.ShapeDtypeStruct((1, LANES), F32)),
        grid=(t // tm,),
        in_specs=[row(d), row(d),
                  pl.BlockSpec(w_out.shape, lambda i: (0, 0)),
                  pl.BlockSpec((1, d), lambda i: (0, 0)),
                  pl.BlockSpec((d, LANES), lambda i: (0, 0)),
                  pl.BlockSpec((1, LANES), lambda i: (0, 0))],
        out_specs=(row(d), pl.BlockSpec((tm * (d // LANES), LANES), lambda i: (i, 0)), row(LANES), row(LANES),
                   pl.BlockSpec((1, LANES), lambda i: (0, 0))),
        compiler_params=_cparams("arbitrary"),
        name="outproj_route",
    )(x, merged, w_out, gain, w_route, b_route)


def _experts_kernel(be_ref, nused_ref, src_ref, src_next_ref, dst_ref, h_hbm, wg_ref, wu_ref, wd_ref, y_hbm,
                    xbuf, ybuf, gsem, ssem):
    g = pl.program_id(0)
    nused = nused_ref[0]
    blk = MOE_BLOCK
    nc = wg_ref.shape[1] // LANES

    def start_gather(idx_ref, slot):
        def body(r2, carry):
            for u in range(2):
                r = 2 * r2 + u
                row = pl.multiple_of(idx_ref[0, 0, r], nc)
                pltpu.make_async_copy(h_hbm.at[pl.ds(row, nc), :],
                                      xbuf.at[slot, pl.ds(pl.multiple_of(r * nc, nc), nc), :],
                                      gsem.at[slot]).start(priority=u)
            return carry
        lax.fori_loop(0, blk // 2, body, 0, unroll=4)

    def start_scatter(slot):
        def body(r2, carry):
            for u in range(2):
                r = 2 * r2 + u
                row = pl.multiple_of(dst_ref[0, 0, r], nc)
                pltpu.make_async_copy(ybuf.at[slot, pl.ds(pl.multiple_of(r * nc, nc), nc), :],
                                      y_hbm.at[pl.ds(row, nc), :], ssem.at[slot]).start(priority=u)
            return carry
        lax.fori_loop(0, blk // 2, body, 0, unroll=4)

    def wait_gather(slot):
        pltpu.make_async_copy(h_hbm.at[pl.ds(0, blk * nc), :], xbuf.at[slot], gsem.at[slot]).wait()

    def wait_scatter(slot):
        pltpu.make_async_copy(ybuf.at[slot], y_hbm.at[pl.ds(0, blk * nc), :], ssem.at[slot]).wait()

    @pl.when(g < nused)
    def _():
        slot = g % 2

        @pl.when(g == 0)
        def _():
            ybuf[1] = jnp.zeros(ybuf.shape[1:], F32)
            fill = pltpu.make_async_copy(ybuf.at[1], y_hbm.at[pl.ds(y_hbm.shape[0] - blk * nc, blk * nc), :],
                                         ssem.at[1])
            fill.start()
            fill.wait()
            start_gather(src_ref, slot)

        @pl.when(g + 1 < nused)
        def _():
            start_gather(src_next_ref, 1 - slot)

        wait_gather(slot)
        x = jnp.concatenate([xbuf[slot, pl.ds(c, blk, stride=nc), :] for c in range(nc)], axis=1)
        hdn = _silu(_dot(x, wg_ref[0])) * _dot(x, wu_ref[0])
        y = _dot(hdn, wd_ref[0])
        for c in range(nc):
            ybuf[slot, pl.ds(c, blk, stride=nc), :] = y[:, c * LANES:(c + 1) * LANES]

        @pl.when(g > 0)
        def _():
            wait_scatter(1 - slot)

        start_scatter(slot)

        @pl.when(g == nused - 1)
        def _():
            wait_scatter(slot)


def _experts(h2, block_e, src_tok, dst_slot, nused, w_gate, w_up, w_down):
    d, ff = w_gate.shape[1], w_gate.shape[2]
    nc = d // LANES
    t = h2.shape[0] // nc
    n_blocks = block_e.shape[0]
    n_slots = t * TOP_K
    kern = _experts_kernel
    src3 = (src_tok * nc).reshape(n_blocks, 1, MOE_BLOCK)
    dst3 = (dst_slot * nc).reshape(n_blocks, 1, MOE_BLOCK)
    idx_spec = lambda f: pl.BlockSpec((1, 1, MOE_BLOCK), lambda g, be, nu: (f(g), 0, 0), memory_space=pltpu.SMEM)
    grid_spec = pltpu.PrefetchScalarGridSpec(
        num_scalar_prefetch=2,
        grid=(n_blocks,),
        in_specs=[idx_spec(lambda g: g),
                  idx_spec(lambda g: jnp.minimum(g + 1, n_blocks - 1)),
                  idx_spec(lambda g: g),
                  pl.BlockSpec(memory_space=pl.ANY),
                  pl.BlockSpec((1, d, ff), lambda g, be, nu: (be[g], 0, 0)),
                  pl.BlockSpec((1, d, ff), lambda g, be, nu: (be[g], 0, 0)),
                  pl.BlockSpec((1, ff, d), lambda g, be, nu: (be[g], 0, 0))],
        out_specs=pl.BlockSpec(memory_space=pl.ANY),
        scratch_shapes=[pltpu.VMEM((2, MOE_BLOCK * nc, LANES), F32), pltpu.VMEM((2, MOE_BLOCK * nc, LANES), F32),
                        pltpu.SemaphoreType.DMA((2,)), pltpu.SemaphoreType.DMA((2,))],
    )
    return pl.pallas_call(
        kern,
        out_shape=jax.ShapeDtypeStruct(((n_slots + MOE_BLOCK) * nc, LANES), F32),
        grid_spec=grid_spec,
        compiler_params=_cparams("arbitrary"),
        name="experts",
    )(block_e, nused, src3, src3, dst3, h2, w_gate, w_up, w_down)


def _expert_plan(eidx, rank, counts, tp):
    t = eidx.shape[0]
    n_experts = counts.shape[0]
    s = t * TOP_K
    padded = (counts + MOE_BLOCK - 1) // MOE_BLOCK * MOE_BLOCK
    ends = jnp.cumsum(padded)
    pad_start = ends - padded
    dest = (pad_start[eidx] + rank).reshape(-1)
    n_blocks = -(-(s + n_experts * (MOE_BLOCK - 1)) // MOE_BLOCK)
    rows = n_blocks * MOE_BLOCK
    slot_of_row = jnp.full((rows,), -1, jnp.int32).at[dest].set(jnp.arange(s, dtype=jnp.int32))
    valid = slot_of_row >= 0
    tok = slot_of_row // TOP_K
    src_tok = jnp.where(valid, tok, 0)
    kk = slot_of_row % TOP_K
    dst_real = jnp.where(tok < tp, kk * tp + tok, TOP_K * tp + kk * (t - tp) + (tok - tp))
    dst_slot = jnp.where(valid, dst_real, s + jnp.arange(rows, dtype=jnp.int32) % MOE_BLOCK)
    block_start = jnp.arange(n_blocks, dtype=jnp.int32) * MOE_BLOCK
    block_e = jnp.minimum(jnp.sum((ends[None, :] <= block_start[:, None]).astype(jnp.int32), axis=1), n_experts - 1)
    nused = (ends[-1:] // MOE_BLOCK).astype(jnp.int32)
    return block_e, src_tok, dst_slot, nused


def _combine_ple_kernel(x1_ref, ys0_ref, ys1_ref, rw_ref, p_ref, wp_ref, g_ref, wpg_ref, y_ref):
    rw = rw_ref[...]
    tm, d = x1_ref.shape
    nc = d // LANES

    def token_rows(ref):
        return jnp.concatenate([ref[pl.ds(c, tm, stride=nc), :] for c in range(nc)], axis=1)

    moe = rw[:, 0:1] * token_rows(ys0_ref) + rw[:, 1:2] * token_rows(ys1_ref)
    x2 = x1_ref[...] + moe
    hn = x2 * lax.rsqrt(jnp.mean(x2 * x2, axis=-1, keepdims=True) + EPS) * g_ref[...]
    gate = _sigmoid(_dot(hn.astype(BF16), wpg_ref[...]))
    y_ref[...] = x2 + _dot(p_ref[...], wp_ref[...]) * gate


def _combine_ple(x1, y_slots, route_w, p, w_ple, gain, w_ple_gate, *, row0, rows, slot_rows):
    t, d = x1.shape
    tm = _row_tile(math.gcd(math.gcd(rows, row0), math.gcd(*slot_rows)), 256)
    r0, s0, s1 = row0 // tm, slot_rows[0] // tm, slot_rows[1] // tm
    row = lambda w: pl.BlockSpec((tm, w), lambda i: (r0 + i, 0))
    once = lambda shape: pl.BlockSpec(shape, lambda i: (0, 0), pipeline_mode=pl.Buffered(1))
    return pl.pallas_call(
        _combine_ple_kernel,
        out_shape=jax.ShapeDtypeStruct((rows, d), F32),
        grid=(rows // tm,),
        in_specs=[row(d),
                  pl.BlockSpec((tm * (d // LANES), LANES), lambda i: (s0 + i, 0)),
                  pl.BlockSpec((tm * (d // LANES), LANES), lambda i: (s1 + i, 0)),
                  row(LANES), row(p.shape[1]),
                  once(w_ple.shape), once((1, d)), once(w_ple_gate.shape)],
        out_specs=pl.BlockSpec((tm, d), lambda i: (i, 0)),
        compiler_params=_cparams("parallel"),
        name="combine_ple_r%d" % row0,
    )(x1, y_slots, y_slots, route_w, p, w_ple, gain, w_ple_gate)


def _pad_rows(a, rows):
    return jnp.pad(a, ((0, 0), (0, rows - a.shape[1]), (0, 0)))


def _layer(x_all, p_all, nb_p, seq_p, nb_s, seq_s, past_len, state_conv, state_delta, cache_k, cache_v,
           page_table, lam_init, lw):
    tp = nb_p * seq_p
    ts = nb_s * seq_s
    d = x_all.shape[1]
    heads_d = lw['a_log'].shape[0]
    dk = lw['delta_norm'].shape[0]
    hk = heads_d * dk
    conv_dim = 3 * hk
    comp = lw['q_norm'].shape[0]
    dv = 2 * comp
    width = lw['w_ob'].shape[0]
    heads_a = width // dv
    n_groups = lw['w_rg'].shape[1]
    n_experts = lw['w_re'].shape[1]
    assert seq_s <= 8 and conv_dim % 1024 == 0 and hk == 1024 and width == 1024 and d % 1024 == 0
    assert 2 * heads_d <= LANES and n_groups + n_experts <= LANES and lw['conv_w'].shape[0] == 4

    w_in = lw['w_in']
    ba0 = conv_dim + hk
    q0 = ba0 + 2 * heads_d
    gate0 = q0 + 3 * width
    w_parts = (w_in[:, :ba0].astype(BF16), w_in[:, gate0:].astype(BF16), w_in[:, q0:gate0].astype(BF16))
    w_ba = jnp.pad(w_in[:, ba0:q0], ((0, 0), (0, LANES - 2 * heads_d))).astype(BF16)
    z_cb = conv_dim // hk
    gate_cb = (conv_dim + hk) // d
    q_cb = (conv_dim + hk + 2 * d) // width
    assert (conv_dim + hk) % d == 0 and (conv_dim + hk + 2 * d) % width == 0

    proj, ba = _inproj(x_all, lw['norm_attn'].reshape(1, d), w_parts, w_ba)

    alog_row = jnp.pad(lw['a_log'], (heads_d, LANES - 2 * heads_d)).reshape(1, LANES)
    dtb_row = jnp.pad(lw['dt_bias'], (heads_d, LANES - 2 * heads_d)).reshape(1, LANES)
    dn = lw['delta_norm'].reshape(1, dk)
    chunk_p = math.gcd(seq_p, DELTA_CHUNK)
    oa_p, s_p = _delta_branch(proj, ba, 0, nb_p, seq_p,
                              jnp.zeros((nb_p, 8, conv_dim), F32), jnp.zeros((nb_p, heads_d, dk, dk), F32),
                              lw['conv_w'], alog_row, dtb_row, dn,
                              chunk=chunk_p, valid=chunk_p, heads=heads_d, dk=dk, z_col_block=z_cb)
    qkv_s = proj[tp:, :conv_dim].reshape(nb_s, seq_s, conv_dim)
    proj_s8 = _pad_rows(proj[tp:].reshape(nb_s, seq_s, -1), 8).reshape(nb_s * 8, -1)
    ba_s8 = _pad_rows(ba[tp:].reshape(nb_s, seq_s, LANES), 8).reshape(nb_s * 8, LANES)
    tail_s = jnp.pad(state_conv, ((0, 0), (8 - state_conv.shape[1], 0), (0, 0)))
    oa_s8, s_s = _delta_branch(proj_s8, ba_s8, 0, nb_s, 8, tail_s, state_delta,
                               lw['conv_w'], alog_row, dtb_row, dn,
                               chunk=8, valid=seq_s, heads=heads_d, dk=dk, z_col_block=z_cb)
    oa_s = oa_s8.reshape(nb_s, 8, hk)[:, :seq_s].reshape(ts, hk)
    o_a = jnp.concatenate([oa_p, oa_s], axis=0)
    conv_p = jnp.stack([proj[(b + 1) * seq_p - 3:(b + 1) * seq_p, :conv_dim] for b in range(nb_p)])
    conv_s = jnp.concatenate([state_conv, qkv_s], axis=1)[:, -3:, :]

    half = comp // 2
    inv_freq = ROPE_THETA ** (-jnp.arange(half, dtype=F32) / half)
    def rope_tables(pos):
        ang = pos.astype(F32)[:, None] * inv_freq[None, :]
        sin_h = jnp.sin(ang)
        return (jnp.tile(jnp.cos(ang), (1, LANES // half)),
                jnp.tile(jnp.concatenate([-sin_h, sin_h], axis=1), (1, LANES // comp)))

    q_gain = jnp.tile(lw['q_norm'], LANES // comp).reshape(1, LANES)
    k_gain = jnp.tile(lw['k_norm'], LANES // comp).reshape(1, LANES)
    tk = _row_tile(seq_p, 512, align=LANES)
    q_t, k_bf, v_t, k_t = _qkprep(proj, 0, nb_p, seq_p, *rope_tables(jnp.arange(seq_p, dtype=jnp.int32)),
                                  q_gain, k_gain, q_col_block=q_cb, width=width, comp=comp, for_prompt=True, tk=tk)
    pos_s = jnp.tile(past_len + jnp.arange(seq_s, dtype=jnp.int32), nb_s)
    q_fs, k_fs = _qkprep(proj, tp, 1, ts, *rope_tables(pos_s), q_gain, k_gain,
                         q_col_block=q_cb, width=width, comp=comp, for_prompt=False)
    lam_vecs = jnp.stack([lw['lam_q1'], lw['lam_k1'], lw['lam_q2'], lw['lam_k2']])
    subln = lw['subln'].reshape(1, dv)
    ob_p = _attn_prompt(q_t, k_bf, v_t, lam_vecs, lw['subln'].reshape(dv, 1), nb=nb_p, seq=seq_p, tk=tk,
                        heads=heads_a, comp=comp, lam_init=lam_init)
    v_f_s = proj[tp:, (q_cb + 2) * width:(q_cb + 3) * width]
    q8 = _pad_rows(q_fs.reshape(nb_s, seq_s, width), 8)
    k8 = _pad_rows(k_fs.reshape(nb_s, seq_s, width), 8)
    v8 = _pad_rows(v_f_s.reshape(nb_s, seq_s, width), 8)
    n_phys, page = cache_k.shape[0], cache_k.shape[1]
    cache_kt = jnp.transpose(cache_k, (0, 2, 3, 4, 1)).reshape(n_phys, width, page)
    cache_v2 = cache_v.reshape(n_phys, page * heads_a, dv)
    ob_s8 = _attn_sample(q8, k8, v8, cache_kt, cache_v2, page_table, lam_vecs, subln,
                         new_len=seq_s, heads=heads_a, comp=comp, lam_init=lam_init)
    o_b = jnp.concatenate([ob_p, ob_s8[:, :seq_s].reshape(ts, width)], axis=0)

    merged = _merge(o_a, o_b, proj, lw['w_oa'].astype(BF16), lw['w_ob'].astype(BF16), gate_col_block=gate_cb)
    w_route = jnp.pad(jnp.concatenate([lw['w_rg'], lw['w_re']], axis=1),
                      ((0, 0), (0, LANES - n_groups - n_experts))).astype(BF16)
    b_route = jnp.pad(jnp.concatenate([lw['b_rg'], lw['b_re']]), (0, LANES - n_groups - n_experts)).reshape(1, LANES)
    x1, h2, route_w, route_i, route_cnt = _outproj_route(x_all, merged, lw['w_out'].astype(BF16),
                                                         lw['norm_ffn'].reshape(1, d), w_route, b_route,
                                                         n_groups=n_groups, per_group=n_experts // n_groups)

    counts = route_cnt[0, n_groups:n_groups + n_experts].astype(jnp.int32)
    block_e, src_tok, dst_slot, nused = _expert_plan(route_i[:, :TOP_K], route_i[:, TOP_K:2 * TOP_K], counts, tp)
    y_slots = _experts(h2, block_e, src_tok, dst_slot, nused, lw['exp_gate'], lw['exp_up'], lw['exp_down'])

    ple_args = (x1, y_slots, route_w, p_all.astype(BF16), lw['w_ple'].astype(BF16),
                lw['norm_ple'].reshape(1, d), lw['w_ple_gate'].astype(BF16))
    y_p = _combine_ple(*ple_args, row0=0, rows=tp, slot_rows=(0, tp))
    y_s = _combine_ple(*ple_args, row0=tp, rows=ts, slot_rows=(TOP_K * tp, TOP_K * tp + ts))
    y = (y_p, y_s)

    k_p = jnp.transpose(k_t.reshape(nb_p, heads_a, 2, comp, seq_p), (0, 4, 1, 2, 3))
    v_p = proj[:tp, (q_cb + 2) * width:(q_cb + 3) * width].reshape(nb_p, seq_p, heads_a, dv)
    k_s = k_fs.reshape(nb_s, seq_s, heads_a, 2, comp)
    v_s = v_f_s.reshape(nb_s, seq_s, heads_a, dv)
    return y, k_p, v_p, k_s, v_s, conv_p, conv_s, s_p, s_s


def kernel(x_prompt, x_sample, cache_k, cache_v, state_conv, state_delta, page_table, p_prompt, p_sample,
           norm_attn, w_in, conv_w, a_log, dt_bias, delta_norm, q_norm, k_norm, lam_q1, lam_k1, lam_q2, lam_k2,
           subln, w_oa, w_ob, w_out, norm_ffn, w_rg, b_rg, w_re, b_re, exp_gate, exp_up, exp_down,
           norm_ple, w_ple, w_ple_gate):
    nb_p, seq_p, d = x_prompt.shape
    nb_s, seq_s, _ = x_sample.shape
    tp = nb_p * seq_p
    depth = w_in.shape[0]
    past_len = page_table.shape[1] * cache_k.shape[2]
    y_p, y_s = x_prompt.reshape(tp, d), x_sample.reshape(nb_s * seq_s, d)
    outs = [[] for _ in range(8)]
    for i in range(depth):
        x_all = jnp.concatenate([y_p, y_s], axis=0)
        lw = dict(norm_attn=norm_attn[i], w_in=w_in[i], conv_w=conv_w[i], a_log=a_log[i], dt_bias=dt_bias[i],
                  delta_norm=delta_norm[i], q_norm=q_norm[i], k_norm=k_norm[i], lam_q1=lam_q1[i], lam_k1=lam_k1[i],
                  lam_q2=lam_q2[i], lam_k2=lam_k2[i], subln=subln[i], w_oa=w_oa[i], w_ob=w_ob[i], w_out=w_out[i],
                  norm_ffn=norm_ffn[i], w_rg=w_rg[i], b_rg=b_rg[i], w_re=w_re[i], b_re=b_re[i],
                  exp_gate=exp_gate[i], exp_up=exp_up[i], exp_down=exp_down[i],
                  norm_ple=norm_ple[i], w_ple=w_ple[i], w_ple_gate=w_ple_gate[i])
        lam_init = 0.8 - 0.6 * math.exp(-0.3 * i)
        p_all = jnp.concatenate([p_prompt[i].reshape(tp, -1), p_sample[i].reshape(nb_s * seq_s, -1)], axis=0)
        res = _layer(x_all, p_all, nb_p, seq_p, nb_s, seq_s, past_len, state_conv[i], state_delta[i],
                     cache_k[i], cache_v[i], page_table, lam_init, lw)
        y_p, y_s = res[0]
        for lst, val in zip(outs, res[1:]):
            lst.append(val)
    return (y_p.reshape(nb_p, seq_p, d), y_s.reshape(nb_s, seq_s, d)) + tuple(jnp.stack(lst) for lst in outs)
```

```python
import functools
import math

import jax
import jax.numpy as jnp
from jax import lax
from jax.experimental import pallas as pl
from jax.experimental.pallas import tpu as pltpu

F32 = jnp.float32
BF16 = jnp.bfloat16
HIGHEST = lax.Precision.HIGHEST

LANES = 128
DELTA_CHUNK = 64
ROPE_THETA = 10000.0
TOP_K = 2
MOE_BLOCK = 128
EPS = 1e-6
VMEM_LIMIT = 48 * 1024 * 1024


def _cparams(*sem):
    return pltpu.CompilerParams(dimension_semantics=sem, vmem_limit_bytes=VMEM_LIMIT)


def _row_tile(n, target, align=16):
    best = None
    for t in range(align, min(n, target) + 1, align):
        if n % t == 0:
            best = t
    assert best is not None, (n, target, align)
    return best


def _nt_dot(a, b, precision=None):
    return lax.dot_general(a, b, (((1,), (1,)), ((), ())), precision=precision,
                           preferred_element_type=F32)


def _tn_dot(a, b, precision=None):
    return lax.dot_general(a, b, (((0,), (0,)), ((), ())), precision=precision,
                           preferred_element_type=F32)


def _dot(a, b, precision=None):
    return jnp.dot(a, b, precision=precision, preferred_element_type=F32)


def _split_bf16(a):
    hi = a.astype(BF16)
    return hi, (a - hi.astype(F32)).astype(BF16)


def _dot_split(a, b):
    return _dot(a[0], b[0]) + (_dot(a[0], b[1]) + _dot(a[1], b[0]))


def _sigmoid(x):
    return 1.0 / (1.0 + jnp.exp(-x))


def _silu(x):
    return x * _sigmoid(x)


def _iota(shape, dim):
    return lax.broadcasted_iota(jnp.int32, shape, dim)


def _inproj_kernel(x_ref, g_ref, *rest, part_tiles):
    w_refs = rest[:len(part_tiles)]
    wba_ref, p_ref, ba_ref, h_scr = rest[len(part_tiles):]
    j = pl.program_id(1)

    @pl.when(j == 0)
    def _():
        x = x_ref[...]
        h = x * lax.rsqrt(jnp.mean(x * x, axis=-1, keepdims=True) + EPS) * g_ref[...]
        h_scr[...] = h.astype(BF16)
        ba_ref[...] = _dot(h_scr[...], wba_ref[...])

    lo = 0
    for w_ref, n_tiles in zip(w_refs, part_tiles):
        @pl.when((j >= lo) & (j < lo + n_tiles))
        def _(w_ref=w_ref):
            p_ref[...] = _dot(h_scr[...], w_ref[...])
        lo += n_tiles


def _inproj(x, gain, w_parts, w_ba):
    t, d = x.shape
    tm = _row_tile(t, 640)
    tn = 1024
    part_tiles = tuple(w.shape[1] // tn for w in w_parts)
    assert all(w.shape[1] % tn == 0 for w in w_parts)
    n_tiles = sum(part_tiles)
    starts = [sum(part_tiles[:k]) for k in range(len(w_parts))]

    def w_spec(k):
        return pl.BlockSpec((d, tn), lambda i, j: (0, jnp.clip(j - starts[k], 0, part_tiles[k] - 1)))

    return pl.pallas_call(
        functools.partial(_inproj_kernel, part_tiles=part_tiles),
        out_shape=(jax.ShapeDtypeStruct((t, n_tiles * tn), F32), jax.ShapeDtypeStruct((t, LANES), F32)),
        grid=(t // tm, n_tiles),
        in_specs=[pl.BlockSpec((tm, d), lambda i, j: (i, 0)),
                  pl.BlockSpec((1, d), lambda i, j: (0, 0))]
                 + [w_spec(k) for k in range(len(w_parts))]
                 + [pl.BlockSpec((d, LANES), lambda i, j: (0, 0))],
        out_specs=(pl.BlockSpec((tm, tn), lambda i, j: (i, j)),
                   pl.BlockSpec((tm, LANES), lambda i, j: (i, 0))),
        scratch_shapes=[pltpu.VMEM((tm, d), BF16)],
        compiler_params=_cparams("parallel", "arbitrary"),
        name="inproj",
    )(x, gain, *w_parts, w_ba)


def _delta_kernel(qkv_ref, z_ref, ba_ref, tail0_ref, s0_ref, cw_ref, alog_ref, dtb_ref, dn_ref,
                  o_ref, sfin_ref, ext_scr, s_scr, *, chunk, valid, heads, dk):
    n = pl.program_id(1)
    c = chunk
    hk = heads * dk

    @pl.when(n == 0)
    def _():
        ext_scr[0:8, :] = tail0_ref[0]
        s_scr[...] = s0_ref[0]

    ext_scr[8:8 + c, :] = qkv_ref[...]

    row = _iota((c, 1), 0)
    rowmask = (row < valid).astype(F32) if valid < c else None

    ba = ba_ref[...]
    beta_all = _sigmoid(ba)
    xg = ba + dtb_ref[...]
    softplus = jnp.maximum(xg, 0.0) + jnp.log1p(jnp.exp(-jnp.abs(xg)))
    g_all = -jnp.exp(alog_ref[...]) * softplus
    if rowmask is not None:
        beta_all = beta_all * rowmask
        g_all = g_all * rowmask
    r_i = _iota((c, c), 0)
    c_i = _iota((c, c), 1)
    incl = r_i >= c_i
    strict = r_i > c_i
    gc_all = _dot(incl.astype(F32), g_all, precision=HIGHEST)
    gc_t = jnp.transpose(gc_all)
    eye = (r_i == c_i).astype(F32)

    def conv(col0):
        sl = slice(col0, col0 + dk)
        acc = ext_scr[8:8 + c, sl] * cw_ref[3:4, sl]
        acc = acc + ext_scr[7:7 + c, sl] * cw_ref[2:3, sl]
        acc = acc + ext_scr[6:6 + c, sl] * cw_ref[1:2, sl]
        acc = acc + ext_scr[5:5 + c, sl] * cw_ref[0:1, sl]
        return _silu(acc)

    hs = range(heads)
    q, k, v, beta, gc, egc, g_last, decay = [], [], [], [], [], [], [], []
    for h in hs:
        qh = conv(h * dk)
        kh = conv(hk + h * dk)
        vh = conv(2 * hk + h * dk)
        qh = qh * lax.rsqrt(jnp.sum(qh * qh, axis=-1, keepdims=True) + EPS) * (dk ** -0.5)
        kh = kh * lax.rsqrt(jnp.sum(kh * kh, axis=-1, keepdims=True) + EPS)
        if rowmask is not None:
            qh, kh, vh = qh * rowmask, kh * rowmask, vh * rowmask
        q.append(qh)
        k.append(kh)
        v.append(vh)
        beta.append(beta_all[:, h:h + 1])
        gc.append(gc_all[:, heads + h:heads + h + 1])
        egc.append(jnp.exp(gc[h]))
        g_last.append(gc_all[c - 1:c, heads + h:heads + h + 1])
        gc_row = gc_t[heads + h:heads + h + 1, :]
        decay.append(jnp.where(incl, jnp.exp(jnp.where(incl, gc[h] - gc_row, 0.0)), 0.0))
    kb = [k[h] * beta[h] for h in hs]
    kk = [_nt_dot(kb[h], k[h]) for h in hs]
    qk = [_nt_dot(q[h], k[h]) for h in hs]
    apow = [jnp.where(strict, kk[h] * decay[h], 0.0) for h in hs]
    qk = [qk[h] * decay[h] for h in hs]
    a_mat = apow
    same_block = lambda b: (r_i // b) == (c_i // b)
    b0 = min(8, c)
    diag0 = same_block(b0)
    apb = [jnp.where(diag0, a_mat[h], 0.0) for h in hs]
    tinv = [eye - apb[h] for h in hs]
    apb = [apb[h].astype(BF16) for h in hs]
    span = 2
    while span < b0:
        apb = [_dot(apb[h], apb[h]).astype(BF16) for h in hs]
        tinv = [tinv[h] + _dot(tinv[h].astype(BF16), apb[h]) for h in hs]
        span *= 2
    b = b0
    while b < c:
        ring = same_block(2 * b) & ((r_i // b) != (c_i // b))
        off = [jnp.where(ring, a_mat[h], 0.0).astype(BF16) for h in hs]
        tb = [tinv[h].astype(BF16) for h in hs]
        tl = [_dot(tb[h], off[h]).astype(BF16) for h in hs]
        tinv = [tinv[h] - _dot(tl[h], tb[h]) for h in hs]
        b *= 2
    resid = [eye - _dot_split(_split_bf16(eye + a_mat[h]), _split_bf16(tinv[h])) for h in hs]
    tinv = [tinv[h] + _dot(tinv[h].astype(BF16), resid[h].astype(BF16)) for h in hs]
    u = [_dot(tinv[h], v[h] * beta[h]) for h in hs]
    w = [_dot(tinv[h], kb[h] * egc[h]) for h in hs]
    s = [s_scr[h] for h in hs]
    ws = [_dot(w[h], s[h]) for h in hs]
    qs = [_dot(q[h] * egc[h], s[h]) for h in hs]
    v_new = [u[h] - ws[h] for h in hs]
    o = [qs[h] + _dot(qk[h], v_new[h]) for h in hs]
    ds = [_tn_dot(k[h] * jnp.exp(g_last[h] - gc[h]), v_new[h]) for h in hs]
    for h in hs:
        s_scr[h] = s[h] * jnp.exp(g_last[h]) + ds[h]
        on = o[h] * lax.rsqrt(jnp.mean(o[h] * o[h], axis=-1, keepdims=True) + EPS) * dn_ref[...]
        zh = z_ref[:, h * dk:(h + 1) * dk]
        o_ref[:, h * dk:(h + 1) * dk] = (on * _silu(zh)).astype(o_ref.dtype)

    ext_scr[0:8, :] = ext_scr[c:c + 8, :]

    @pl.when(n == pl.num_programs(1) - 1)
    def _():
        sfin_ref[0] = s_scr[...]


def _delta_branch(proj, ba, row0, nb, seq, tail0, s0, conv_w, alog_row, dtb_row, delta_norm, *,
                  chunk, valid, heads, dk, z_col_block):
    hk = heads * dk
    nchunks = seq // chunk
    assert seq % chunk == 0 and row0 % chunk == 0
    r0 = row0 // chunk
    kern = functools.partial(_delta_kernel, chunk=chunk, valid=valid, heads=heads, dk=dk)
    return pl.pallas_call(
        kern,
        out_shape=(jax.ShapeDtypeStruct((nb * seq, hk), BF16),
                   jax.ShapeDtypeStruct((nb, heads, dk, dk), F32)),
        grid=(nb, nchunks),
        in_specs=[pl.BlockSpec((chunk, 3 * hk), lambda b, n: (r0 + b * nchunks + n, 0)),
                  pl.BlockSpec((chunk, hk), lambda b, n: (r0 + b * nchunks + n, z_col_block)),
                  pl.BlockSpec((chunk, LANES), lambda b, n: (r0 + b * nchunks + n, 0)),
                  pl.BlockSpec((1, 8, 3 * hk), lambda b, n: (b, 0, 0)),
                  pl.BlockSpec((1, heads, dk, dk), lambda b, n: (b, 0, 0, 0)),
                  pl.BlockSpec((4, 3 * hk), lambda b, n: (0, 0)),
                  pl.BlockSpec((1, LANES), lambda b, n: (0, 0)),
                  pl.BlockSpec((1, LANES), lambda b, n: (0, 0)),
                  pl.BlockSpec((1, dk), lambda b, n: (0, 0))],
        out_specs=(pl.BlockSpec((chunk, hk), lambda b, n: (b * nchunks + n, 0)),
                   pl.BlockSpec((1, heads, dk, dk), lambda b, n: (b, 0, 0, 0))),
        scratch_shapes=[pltpu.VMEM((8 + chunk, 3 * hk), F32),
                        pltpu.VMEM((heads, dk, dk), F32)],
        compiler_params=_cparams("parallel", "arbitrary"),
        name="delta_c%d" % chunk,
    )(proj, proj, ba, tail0, s0, conv_w, alog_row, dtb_row, delta_norm)


def _qkprep_kernel(q_ref, k_ref, v_ref, cos_ref, sin_ref, qg_ref, kg_ref, *out_refs, comp, scale, for_prompt, tk):
    tm = q_ref.shape[0]
    r_i = _iota((LANES, LANES), 0)
    c_i = _iota((LANES, LANES), 1)
    group = (r_i // comp == c_i // comp).astype(BF16)
    lane = _iota((tm, LANES), 1)
    first_half = (lane % comp) < (comp // 2)
    cos = cos_ref[...]
    sin = sin_ref[...]

    def norm_rope(x, gain):
        sq = x * x
        hi = sq.astype(BF16)
        lo = (sq - hi.astype(F32)).astype(BF16)
        ms = (_dot(hi, group) + _dot(lo, group)) * (1.0 / comp)
        y = x * lax.rsqrt(ms + EPS) * gain
        swapped = jnp.where(first_half, pltpu.roll(y, LANES - comp // 2, 1), pltpu.roll(y, comp // 2, 1))
        return y * cos + swapped * sin

    for j in range(q_ref.shape[1] // LANES):
        sl = slice(j * LANES, (j + 1) * LANES)
        qr = norm_rope(q_ref[:, sl], qg_ref[...]) * scale
        kr = norm_rope(k_ref[:, sl], kg_ref[...])
        if for_prompt:
            qt_ref, kbo_ref, vt_ref, kt_ref = out_refs
            qt_ref[0, sl, :] = jnp.transpose(qr).astype(BF16)
            kbo_ref[:, sl] = kr.astype(BF16)
            kt_ref[0, sl, :] = jnp.transpose(kr)
            vt = jnp.transpose(v_ref[:, sl]).astype(BF16)
            for u in range(tm // tk):
                vt_ref[0, u, sl, :] = vt[:, u * tk:(u + 1) * tk]
        else:
            qo_ref, kfo_ref = out_refs
            qo_ref[:, sl] = qr
            kfo_ref[:, sl] = kr


def _qkprep(proj, row0, nb, seq, cos_t, sin_t, q_gain, k_gain, *, q_col_block, width, comp, for_prompt, tk=None):
    tm = _row_tile(seq, 512, align=LANES) if for_prompt else seq
    assert row0 % tm == 0 and seq % tm == 0 and (not for_prompt or tm % tk == 0)
    r0, per_seq = row0 // tm, seq // tm
    scale = comp ** -0.5 * (math.log2(math.e) if for_prompt else 1.0)
    kern = functools.partial(_qkprep_kernel, comp=comp, scale=scale, for_prompt=for_prompt, tk=tk)
    col = lambda cb: pl.BlockSpec((tm, width), lambda i: (r0 + i, cb))
    small = pl.BlockSpec((tm, LANES), lambda i: (i % per_seq, 0))
    gain = pl.BlockSpec((1, LANES), lambda i: (0, 0))
    out = pl.BlockSpec((tm, width), lambda i: (i, 0))
    rows = nb * seq
    if for_prompt:
        out_t = pl.BlockSpec((1, width, tm), lambda i: (i // per_seq, 0, i % per_seq))
        out_shape = (jax.ShapeDtypeStruct((nb, width, seq), BF16), jax.ShapeDtypeStruct((rows, width), BF16),
                     jax.ShapeDtypeStruct((nb, seq // tk, width, tk), BF16),
                     jax.ShapeDtypeStruct((nb, width, seq), F32))
        out_specs = (out_t, out,
                     pl.BlockSpec((1, tm // tk, width, tk), lambda i: (i // per_seq, i % per_seq, 0, 0)), out_t)
    else:
        out_shape = (jax.ShapeDtypeStruct((rows, width), F32), jax.ShapeDtypeStruct((rows, width), F32))
        out_specs = (out, out)
    return pl.pallas_call(
        kern,
        out_shape=out_shape,
        grid=(rows // tm,),
        in_specs=[col(q_col_block), col(q_col_block + 1), col(q_col_block + 2), small, small, gain, gain],
        out_specs=out_specs,
        compiler_params=_cparams("parallel"),
        name="qkprep_prompt" if for_prompt else "qkprep_sample",
    )(proj, proj, proj, cos_t, sin_t, q_gain, k_gain)


def _lambda_value(lam_ref, lam_init):
    lv = lam_ref[...]
    s1 = jnp.sum(lv[0:1] * lv[1:2], axis=-1, keepdims=True)
    s2 = jnp.sum(lv[2:3] * lv[3:4], axis=-1, keepdims=True)
    return jnp.exp(s1) - jnp.exp(s2) + lam_init


def _attn_prompt_kernel(q_ref, k_ref, v_ref, lam_ref, sub_ref, o_ref, *, tq, tk, comp, heads_per_step, lam_init):
    i = pl.program_id(2)
    dv = 2 * comp
    hr = range(heads_per_step)
    row = _iota((dv, tq), 0)
    qs = []
    for hh in hr:
        qt = q_ref[0, hh * dv:(hh + 1) * dv, :]
        zero = jnp.zeros_like(qt)
        qs.append(jnp.concatenate([jnp.where(row < comp, qt, zero), jnp.where(row >= comp, qt, zero)], axis=1))

    def update(carry, j, mask):
        start = pl.multiple_of(j * tk, tk)
        kt = [k_ref[pl.ds(start, tk), hh * dv:(hh + 1) * dv] for hh in hr]
        vt = [v_ref[0, j, hh * dv:(hh + 1) * dv, :] for hh in hr]
        s = [_dot(kt[hh], qs[hh]) for hh in hr]
        if mask is not None:
            s = [jnp.where(mask, sn, -jnp.inf) for sn in s]
        m_new = [jnp.maximum(carry[hh][0], jnp.max(s[hh], axis=0, keepdims=True)) for hh in hr]
        p = [jnp.exp2(s[hh] - m_new[hh]) for hh in hr]
        corr = [jnp.exp2(carry[hh][0] - m_new[hh]) for hh in hr]
        l = [carry[hh][1] * corr[hh] + jnp.sum(p[hh], axis=0, keepdims=True) for hh in hr]
        pv = [_dot(vt[hh], p[hh].astype(BF16)) for hh in hr]
        return tuple((m_new[hh], l[hh], carry[hh][2] * corr[hh] + pv[hh]) for hh in hr)

    def body(j, carry):
        return update(carry, j, None)

    init = tuple((jnp.full((1, 2 * tq), -jnp.inf, F32), jnp.zeros((1, 2 * tq), F32),
                  jnp.zeros((dv, 2 * tq), F32)) for _ in hr)
    n_full = (i * tq) // tk
    carry = lax.fori_loop(0, n_full, body, init)
    kpos = n_full * tk + _iota((tk, 2 * tq), 0)
    qpos = i * tq + _iota((tk, 2 * tq), 1) % tq
    carry = update(carry, n_full, qpos >= kpos)
    lam = _lambda_value(lam_ref, lam_init)
    for hh in hr:
        _, l, a = carry[hh]
        n = a / l
        o = n[:, 0:tq] - lam * n[:, tq:2 * tq]
        on = o * lax.rsqrt(jnp.mean(o * o, axis=0, keepdims=True) + EPS) * sub_ref[...]
        o_ref[:, hh * dv:(hh + 1) * dv] = jnp.transpose(on * (1.0 - lam_init)).astype(o_ref.dtype)


def _attn_prompt(q_t, k, v_t, lam_vecs, subln_col, *, nb, seq, tk, heads, comp, lam_init):
    dv = 2 * comp
    tq = _row_tile(seq, 256, align=LANES)
    assert tk % tq == 0 and seq % tk == 0
    nq = seq // tq
    hps = 2
    assert heads % hps == 0
    kern = functools.partial(_attn_prompt_kernel, tq=tq, tk=tk, comp=comp, heads_per_step=hps, lam_init=lam_init)
    return pl.pallas_call(
        kern,
        out_shape=jax.ShapeDtypeStruct((nb * seq, heads * dv), BF16),
        grid=(nb, heads // hps, nq),
        in_specs=[pl.BlockSpec((1, hps * dv, tq), lambda b, h, i: (b, h, i)),
                  pl.BlockSpec((seq, hps * dv), lambda b, h, i: (b, h)),
                  pl.BlockSpec((1, seq // tk, hps * dv, tk), lambda b, h, i: (b, 0, h, 0)),
                  pl.BlockSpec((4, comp), lambda b, h, i: (0, 0)),
                  pl.BlockSpec((dv, 1), lambda b, h, i: (0, 0))],
        out_specs=pl.BlockSpec((tq, hps * dv), lambda b, h, i: (b * nq + i, h)),
        compiler_params=_cparams("parallel", "parallel", "arbitrary"),
        name="attn_prompt",
    )(q_t, k, v_t, lam_vecs, subln_col)


def _attn_sample_kernel(pt_ref, q_ref, kn_ref, vn_ref, lam_ref, sub_ref, *rest,
                        pages_per_step, new_len, heads, comp, lam_init):
    pp = pages_per_step
    kt_refs = rest[:pp]
    v_refs = rest[pp:2 * pp]
    o_ref = rest[2 * pp]
    qbd_scr, m_scr, l_scr, acc_scr = rest[2 * pp + 1:]
    p_id = pl.program_id(1)
    rph = 2 * new_len
    rows = heads * rph
    dv = 2 * comp
    width = heads * dv
    page = kt_refs[0].shape[2]

    @pl.when(p_id == 0)
    def _():
        q = q_ref[0][0:new_len, :]
        lane_grp = _iota((new_len, width), 1) // comp
        for h in range(heads):
            for c in range(2):
                r0 = h * rph + c * new_len
                qbd_scr[r0:r0 + new_len, :] = jnp.where(lane_grp == 2 * h + c, q, 0.0)
        m_scr[...] = jnp.full(m_scr.shape, -jnp.inf, F32)
        l_scr[...] = jnp.zeros(l_scr.shape, F32)
        acc_scr[...] = jnp.zeros(acc_scr.shape, F32)

    qbd = qbd_scr[...]

    def online(s_list, v_of):
        m = m_scr[...]
        m_new = m
        for s in s_list:
            m_new = jnp.maximum(m_new, jnp.max(s, axis=-1, keepdims=True))
        corr = jnp.exp(m - m_new)
        p_list = [jnp.exp(s - m_new) for s in s_list]
        l = l_scr[...] * corr
        for p in p_list:
            l = l + jnp.sum(p, axis=-1, keepdims=True)
        m_scr[...] = m_new
        l_scr[...] = l
        for h in range(heads):
            hs = slice(h * rph, (h + 1) * rph)
            acc = acc_scr[hs, :] * corr[hs, :]
            for j, p in enumerate(p_list):
                acc = acc + _dot(p[hs, :], v_of(j, h))
            acc_scr[hs, :] = acc

    online([_dot(qbd, kt_refs[r][0]) for r in range(pp)],
           lambda j, h: v_refs[j][0, pl.ds(h, page, stride=heads), :])

    @pl.when(p_id == pl.num_programs(1) - 1)
    def _():
        s = _nt_dot(qbd, kn_ref[0])
        tok = _iota((rows, 8), 0) % new_len
        key = _iota((rows, 8), 1)
        s = jnp.where((tok >= key) & (key < new_len), s, -jnp.inf)
        online([s], lambda j, h: vn_ref[0][:, h * dv:(h + 1) * dv])
        lam = _lambda_value(lam_ref, lam_init)
        nrm = acc_scr[...] / l_scr[...]
        for h in range(heads):
            r0 = h * rph
            oh = nrm[r0:r0 + new_len, :] - lam * nrm[r0 + new_len:r0 + rph, :]
            on = oh * lax.rsqrt(jnp.mean(oh * oh, axis=-1, keepdims=True) + EPS) * sub_ref[...]
            o_ref[0, 0:new_len, h * dv:(h + 1) * dv] = (on * (1.0 - lam_init)).astype(o_ref.dtype)
        if new_len < 8:
            o_ref[0, new_len:8, :] = jnp.zeros((8 - new_len, width), o_ref.dtype)


def _attn_sample(q8, k8, v8, cache_kt, cache_v2, page_table, lam_vecs, subln, *, new_len, heads, comp, lam_init):
    nb, n_pages = page_table.shape
    width, page = cache_kt.shape[1], cache_kt.shape[2]
    dv = 2 * comp
    pp = math.gcd(n_pages, 16)
    assert 2 * new_len == 8
    rows = heads * 2 * new_len
    kern = functools.partial(_attn_sample_kernel, pages_per_step=pp, new_len=new_len, heads=heads,
                             comp=comp, lam_init=lam_init)
    per_req = pl.BlockSpec((1, 8, width), lambda b, p, pt: (b, 0, 0))

    def page_spec(r, shape):
        return pl.BlockSpec((1,) + shape, lambda b, p, pt: (pt[b * n_pages + p * pp + r], 0, 0))

    grid_spec = pltpu.PrefetchScalarGridSpec(
        num_scalar_prefetch=1,
        grid=(nb, n_pages // pp),
        in_specs=[per_req, per_req, per_req,
                  pl.BlockSpec((4, comp), lambda b, p, pt: (0, 0)),
                  pl.BlockSpec((1, dv), lambda b, p, pt: (0, 0))]
                 + [page_spec(r, (width, page)) for r in range(pp)]
                 + [page_spec(r, (page * heads, dv)) for r in range(pp)],
        out_specs=pl.BlockSpec((1, 8, width), lambda b, p, pt: (b, 0, 0)),
        scratch_shapes=[pltpu.VMEM((rows, width), F32), pltpu.VMEM((rows, 1), F32),
                        pltpu.VMEM((rows, 1), F32), pltpu.VMEM((rows, dv), F32)],
    )
    return pl.pallas_call(
        kern,
        out_shape=jax.ShapeDtypeStruct((nb, 8, width), BF16),
        grid_spec=grid_spec,
        compiler_params=_cparams("parallel", "arbitrary"),
        name="attn_sample",
    )(page_table.reshape(-1), q8, k8, v8, lam_vecs, subln, *([cache_kt] * pp), *([cache_v2] * pp))


def _merge_kernel(oa_ref, ob_ref, ga_ref, gb_ref, woa_ref, wob_ref, m_ref):
    ua = _dot(oa_ref[...], woa_ref[...])
    ub = _dot(ob_ref[...], wob_ref[...])
    m_ref[...] = (_sigmoid(ga_ref[...]) * ua + _sigmoid(gb_ref[...]) * ub).astype(m_ref.dtype)


def _merge(o_a, o_b, proj, w_oa, w_ob, *, gate_col_block):
    t, zd = o_a.shape
    d = w_oa.shape[1]
    tm = _row_tile(t, 640)
    once = lambda shape: pl.BlockSpec(shape, lambda i: (0, 0), pipeline_mode=pl.Buffered(1))
    return pl.pallas_call(
        _merge_kernel,
        out_shape=jax.ShapeDtypeStruct((t, d), BF16),
        grid=(t // tm,),
        in_specs=[pl.BlockSpec((tm, zd), lambda i: (i, 0)),
                  pl.BlockSpec((tm, o_b.shape[1]), lambda i: (i, 0)),
                  pl.BlockSpec((tm, d), lambda i: (i, gate_col_block)),
                  pl.BlockSpec((tm, d), lambda i: (i, gate_col_block + 1)),
                  once(w_oa.shape), once(w_ob.shape)],
        out_specs=pl.BlockSpec((tm, d), lambda i: (i, 0)),
        compiler_params=_cparams("parallel"),
        name="merge",
    )(o_a, o_b, proj, proj, w_oa, w_ob)


def _outproj_route_kernel(x_ref, m_ref, wo_ref, g_ref, wr_ref, br_ref,
                          x1_ref, h2_ref, rw_ref, re_ref, cnt_ref, *, n_groups, per_group):
    x1 = x_ref[...] + _dot(m_ref[...], wo_ref[...])
    x1_ref[...] = x1
    h2 = x1 * lax.rsqrt(jnp.mean(x1 * x1, axis=-1, keepdims=True) + EPS) * g_ref[...]
    nc = h2.shape[1] // LANES
    for c in range(nc):
        h2_ref[pl.ds(c, h2.shape[0], stride=nc), :] = h2[:, c * LANES:(c + 1) * LANES]
    logits = _dot(h2.astype(BF16), wr_ref[...]) + br_ref[...]
    tm = logits.shape[0]
    lane = _iota((tm, LANES), 1).astype(F32)
    big = float(LANES)

    def masked_softmax(mask):
        lm = jnp.where(mask, logits, -jnp.inf)
        e = jnp.exp(lm - jnp.max(lm, axis=-1, keepdims=True))
        return e / jnp.sum(e, axis=-1, keepdims=True)

    def top1(p, mask):
        v = jnp.max(jnp.where(mask, p, -1.0), axis=-1, keepdims=True)
        idx = jnp.min(jnp.where(mask & (p == v), lane, big), axis=-1, keepdims=True)
        return v, idx

    gmask = lane < n_groups
    pg = masked_softmax(gmask)
    g_w, g_top = top1(pg, gmask)
    lo = n_groups + g_top * per_group
    emask = (lane >= lo) & (lane < lo + per_group)
    pe = masked_softmax(emask)
    v1, i1 = top1(pe, emask)
    emask2 = emask & (lane != i1)
    v2, i2 = top1(pe, emask2)
    denom = v1 + v2
    w1 = g_w * v1 / denom
    w2 = g_w * v2 / denom
    rw_ref[...] = jnp.where(lane == 0.0, w1, jnp.where(lane == 1.0, w2, 0.0))

    @pl.when(pl.program_id(0) == 0)
    def _():
        cnt_ref[...] = jnp.zeros(cnt_ref.shape, F32)

    sel1 = lane == i1
    sel2 = lane == i2
    picked = jnp.where(sel1 | sel2, 1.0, 0.0)
    before = (_iota((tm, tm), 0) > _iota((tm, tm), 1)).astype(BF16)
    prior = _dot(before, picked.astype(BF16)) + cnt_ref[...]
    r1 = jnp.sum(jnp.where(sel1, prior, 0.0), axis=-1, keepdims=True)
    r2 = jnp.sum(jnp.where(sel2, prior, 0.0), axis=-1, keepdims=True)
    cnt_ref[...] = cnt_ref[...] + jnp.sum(picked, axis=0, keepdims=True)
    re_ref[...] = jnp.where(lane == 0.0, i1 - n_groups,
                            jnp.where(lane == 1.0, i2 - n_groups,
                                      jnp.where(lane == 2.0, r1, jnp.where(lane == 3.0, r2, 0.0)))).astype(jnp.int32)


def _outproj_route(x, merged, w_out, gain, w_route, b_route, *, n_groups, per_group):
    t, d = x.shape
    tm = _row_tile(t, 320)
    kern = functools.partial(_outproj_route_kernel, n_groups=n_groups, per_group=per_group)
    row = lambda w: pl.BlockSpec((tm, w), lambda i: (i, 0))
    return pl.pallas_call(
        kern,
        out_shape=(jax.ShapeDtypeStruct((t, d), F32), jax.ShapeDtypeStruct((t * (d // LANES), LANES), F32),
                   jax.ShapeDtypeStruct((t, LANES), F32), jax.ShapeDtypeStruct((t, LANES), jnp.int32),
                   jax.ShapeDtypeStruct((1, LANES), F32)),
        grid=(t // tm,),
        in_specs=[row(d), row(d),
                  pl.BlockSpec(w_out.shape, lambda i: (0, 0)),
                  pl.BlockSpec((1, d), lambda i: (0, 0)),
                  pl.BlockSpec((d, LANES), lambda i: (0, 0)),
                  pl.BlockSpec((1, LANES), lambda i: (0, 0))],
        out_specs=(row(d), pl.BlockSpec((tm * (d // LANES), LANES), lambda i: (i, 0)), row(LANES), row(LANES),
                   pl.BlockSpec((1, LANES), lambda i: (0, 0))),
        compiler_params=_cparams("arbitrary"),
        name="outproj_route",
    )(x, merged, w_out, gain, w_route, b_route)


def _experts_kernel(be_ref, nused_ref, first_ref, wslot_ref, nexte_ref, src_ref, src_next_ref, dst_ref,
                    h_hbm, wg_hbm, wu_hbm, wd_hbm, y_hbm, xbuf, ybuf, wgbuf, wubuf, wdbuf, gsem, ssem, wsem):
    g = pl.program_id(0)
    nused = nused_ref[0]
    blk = MOE_BLOCK
    nc = wg_hbm.shape[1] // LANES

    def weight_copies(e, ws):
        return [pltpu.make_async_copy(hbm.at[e], buf.at[ws], wsem.at[ws])
                for hbm, buf in ((wg_hbm, wgbuf), (wu_hbm, wubuf), (wd_hbm, wdbuf))]

    def start_gather(idx_ref, slot):
        def body(r2, carry):
            for u in range(2):
                r = 2 * r2 + u
                row = pl.multiple_of(idx_ref[0, 0, r], nc)
                pltpu.make_async_copy(h_hbm.at[pl.ds(row, nc), :],
                                      xbuf.at[slot, pl.ds(pl.multiple_of(r * nc, nc), nc), :],
                                      gsem.at[slot]).start(priority=u)
            return carry
        lax.fori_loop(0, blk // 2, body, 0, unroll=4)

    def start_scatter(slot):
        def body(r2, carry):
            for u in range(2):
                r = 2 * r2 + u
                row = pl.multiple_of(dst_ref[0, 0, r], nc)
                pltpu.make_async_copy(ybuf.at[slot, pl.ds(pl.multiple_of(r * nc, nc), nc), :],
                                      y_hbm.at[pl.ds(row, nc), :], ssem.at[slot]).start(priority=u)
            return carry
        lax.fori_loop(0, blk // 2, body, 0, unroll=4)

    def wait_gather(slot):
        pltpu.make_async_copy(h_hbm.at[pl.ds(0, blk * nc), :], xbuf.at[slot], gsem.at[slot]).wait()

    def wait_scatter(slot):
        pltpu.make_async_copy(ybuf.at[slot], y_hbm.at[pl.ds(0, blk * nc), :], ssem.at[slot]).wait()

    @pl.when(g < nused)
    def _():
        slot = g % 2

        @pl.when(g == 0)
        def _():
            ybuf[1] = jnp.zeros(ybuf.shape[1:], F32)
            fill = pltpu.make_async_copy(ybuf.at[1], y_hbm.at[pl.ds(y_hbm.shape[0] - blk * nc, blk * nc), :],
                                         ssem.at[1])
            fill.start()
            fill.wait()
            start_gather(src_ref, slot)
            for cp in weight_copies(be_ref[0], wslot_ref[0]):
                cp.start()

        @pl.when(g + 1 < nused)
        def _():
            start_gather(src_next_ref, 1 - slot)

        ws = wslot_ref[g]

        @pl.when(first_ref[g] == 1)
        def _():
            @pl.when(nexte_ref[g] >= 0)
            def _():
                for cp in weight_copies(nexte_ref[g], 1 - ws):
                    cp.start()
            for cp in weight_copies(be_ref[g], ws):
                cp.wait()

        wait_gather(slot)
        x = jnp.concatenate([xbuf[slot, pl.ds(c, blk, stride=nc), :] for c in range(nc)], axis=1)
        hdn = _silu(_dot(x, wgbuf[ws])) * _dot(x, wubuf[ws])
        y = _dot(hdn, wdbuf[ws])
        for c in range(nc):
            ybuf[slot, pl.ds(c, blk, stride=nc), :] = y[:, c * LANES:(c + 1) * LANES]

        @pl.when(g > 0)
        def _():
            wait_scatter(1 - slot)

        start_scatter(slot)

        @pl.when(g == nused - 1)
        def _():
            wait_scatter(slot)


def _experts(h2, block_e, src_tok, dst_slot, nused, w_gate, w_up, w_down):
    d, ff = w_gate.shape[1], w_gate.shape[2]
    nc = d // LANES
    t = h2.shape[0] // nc
    n_blocks = block_e.shape[0]
    n_slots = t * TOP_K
    kern = _experts_kernel
    src3 = (src_tok * nc).reshape(n_blocks, 1, MOE_BLOCK)
    dst3 = (dst_slot * nc).reshape(n_blocks, 1, MOE_BLOCK)
    blk_id = jnp.arange(n_blocks, dtype=jnp.int32)
    used = blk_id < nused[0]
    first = used & ((blk_id == 0) | (block_e != jnp.roll(block_e, 1)))
    run_id = jnp.cumsum(first.astype(jnp.int32)) - 1
    wslot = (run_id % 2).astype(jnp.int32)
    run_expert = jnp.full((n_blocks + 1,), -1, jnp.int32).at[jnp.where(first, run_id, n_blocks)].set(block_e)
    next_e = run_expert[jnp.minimum(run_id + 1, n_blocks)]
    idx_spec = lambda f: pl.BlockSpec((1, 1, MOE_BLOCK), lambda g, *_: (f(g), 0, 0), memory_space=pltpu.SMEM)
    anyspec = pl.BlockSpec(memory_space=pl.ANY)
    grid_spec = pltpu.PrefetchScalarGridSpec(
        num_scalar_prefetch=5,
        grid=(n_blocks,),
        in_specs=[idx_spec(lambda g: g),
                  idx_spec(lambda g: jnp.minimum(g + 1, n_blocks - 1)),
                  idx_spec(lambda g: g),
                  anyspec, anyspec, anyspec, anyspec],
        out_specs=anyspec,
        scratch_shapes=[pltpu.VMEM((2, MOE_BLOCK * nc, LANES), F32), pltpu.VMEM((2, MOE_BLOCK * nc, LANES), F32),
                        pltpu.VMEM((2, d, ff), F32), pltpu.VMEM((2, d, ff), F32), pltpu.VMEM((2, ff, d), F32),
                        pltpu.SemaphoreType.DMA((2,)), pltpu.SemaphoreType.DMA((2,)), pltpu.SemaphoreType.DMA((2,))],
    )
    return pl.pallas_call(
        kern,
        out_shape=jax.ShapeDtypeStruct(((n_slots + MOE_BLOCK) * nc, LANES), F32),
        grid_spec=grid_spec,
        compiler_params=_cparams("arbitrary"),
        name="experts",
    )(block_e, nused, first.astype(jnp.int32), wslot, next_e, src3, src3, dst3, h2, w_gate, w_up, w_down)


def _expert_plan(eidx, rank, counts, tp):
    t = eidx.shape[0]
    n_experts = counts.shape[0]
    s = t * TOP_K
    padded = (counts + MOE_BLOCK - 1) // MOE_BLOCK * MOE_BLOCK
    ends = jnp.cumsum(padded)
    pad_start = ends - padded
    dest = (pad_start[eidx] + rank).reshape(-1)
    n_blocks = -(-(s + n_experts * (MOE_BLOCK - 1)) // MOE_BLOCK)
    rows = n_blocks * MOE_BLOCK
    slot_of_row = jnp.full((rows,), -1, jnp.int32).at[dest].set(jnp.arange(s, dtype=jnp.int32))
    valid = slot_of_row >= 0
    tok = slot_of_row // TOP_K
    src_tok = jnp.where(valid, tok, 0)
    kk = slot_of_row % TOP_K
    dst_real = jnp.where(tok < tp, kk * tp + tok, TOP_K * tp + kk * (t - tp) + (tok - tp))
    dst_slot = jnp.where(valid, dst_real, s + jnp.arange(rows, dtype=jnp.int32) % MOE_BLOCK)
    block_start = jnp.arange(n_blocks, dtype=jnp.int32) * MOE_BLOCK
    block_e = jnp.minimum(jnp.sum((ends[None, :] <= block_start[:, None]).astype(jnp.int32), axis=1), n_experts - 1)
    nused = (ends[-1:] // MOE_BLOCK).astype(jnp.int32)
    return block_e, src_tok, dst_slot, nused


def _combine_ple_kernel(x1_ref, ys0_ref, ys1_ref, rw_ref, p_ref, wp_ref, g_ref, wpg_ref, y_ref):
    rw = rw_ref[...]
    tm, d = x1_ref.shape
    nc = d // LANES

    def token_rows(ref):
        return jnp.concatenate([ref[pl.ds(c, tm, stride=nc), :] for c in range(nc)], axis=1)

    moe = rw[:, 0:1] * token_rows(ys0_ref) + rw[:, 1:2] * token_rows(ys1_ref)
    x2 = x1_ref[...] + moe
    hn = x2 * lax.rsqrt(jnp.mean(x2 * x2, axis=-1, keepdims=True) + EPS) * g_ref[...]
    gate = _sigmoid(_dot(hn.astype(BF16), wpg_ref[...]))
    y_ref[...] = x2 + _dot(p_ref[...], wp_ref[...]) * gate


def _combine_ple(x1, y_slots, route_w, p, w_ple, gain, w_ple_gate, *, row0, rows, slot_rows):
    t, d = x1.shape
    tm = _row_tile(math.gcd(math.gcd(rows, row0), math.gcd(*slot_rows)), 256)
    r0, s0, s1 = row0 // tm, slot_rows[0] // tm, slot_rows[1] // tm
    row = lambda w: pl.BlockSpec((tm, w), lambda i: (r0 + i, 0))
    once = lambda shape: pl.BlockSpec(shape, lambda i: (0, 0), pipeline_mode=pl.Buffered(1))
    return pl.pallas_call(
        _combine_ple_kernel,
        out_shape=jax.ShapeDtypeStruct((rows, d), F32),
        grid=(rows // tm,),
        in_specs=[row(d),
                  pl.BlockSpec((tm * (d // LANES), LANES), lambda i: (s0 + i, 0)),
                  pl.BlockSpec((tm * (d // LANES), LANES), lambda i: (s1 + i, 0)),
                  row(LANES), row(p.shape[1]),
                  once(w_ple.shape), once((1, d)), once(w_ple_gate.shape)],
        out_specs=pl.BlockSpec((tm, d), lambda i: (i, 0)),
        compiler_params=_cparams("parallel"),
        name="combine_ple_r%d" % row0,
    )(x1, y_slots, y_slots, route_w, p, w_ple, gain, w_ple_gate)


def _pad_rows(a, rows):
    return jnp.pad(a, ((0, 0), (0, rows - a.shape[1]), (0, 0)))


def _layer(x_all, p_all, nb_p, seq_p, nb_s, seq_s, past_len, state_conv, state_delta, cache_k, cache_v,
           page_table, lam_init, lw):
    tp = nb_p * seq_p
    ts = nb_s * seq_s
    d = x_all.shape[1]
    heads_d = lw['a_log'].shape[0]
    dk = lw['delta_norm'].shape[0]
    hk = heads_d * dk
    conv_dim = 3 * hk
    comp = lw['q_norm'].shape[0]
    dv = 2 * comp
    width = lw['w_ob'].shape[0]
    heads_a = width // dv
    n_groups = lw['w_rg'].shape[1]
    n_experts = lw['w_re'].shape[1]
    assert seq_s <= 8 and conv_dim % 1024 == 0 and hk == 1024 and width == 1024 and d % 1024 == 0
    assert 2 * heads_d <= LANES and n_groups + n_experts <= LANES and lw['conv_w'].shape[0] == 4

    w_in = lw['w_in']
    ba0 = conv_dim + hk
    q0 = ba0 + 2 * heads_d
    gate0 = q0 + 3 * width
    w_parts = (w_in[:, :ba0].astype(BF16), w_in[:, gate0:].astype(BF16), w_in[:, q0:gate0].astype(BF16))
    w_ba = jnp.pad(w_in[:, ba0:q0], ((0, 0), (0, LANES - 2 * heads_d))).astype(BF16)
    z_cb = conv_dim // hk
    gate_cb = (conv_dim + hk) // d
    q_cb = (conv_dim + hk + 2 * d) // width
    assert (conv_dim + hk) % d == 0 and (conv_dim + hk + 2 * d) % width == 0

    proj, ba = _inproj(x_all, lw['norm_attn'].reshape(1, d), w_parts, w_ba)

    alog_row = jnp.pad(lw['a_log'], (heads_d, LANES - 2 * heads_d)).reshape(1, LANES)
    dtb_row = jnp.pad(lw['dt_bias'], (heads_d, LANES - 2 * heads_d)).reshape(1, LANES)
    dn = lw['delta_norm'].reshape(1, dk)
    chunk_p = math.gcd(seq_p, DELTA_CHUNK)
    oa_p, s_p = _delta_branch(proj, ba, 0, nb_p, seq_p,
                              jnp.zeros((nb_p, 8, conv_dim), F32), jnp.zeros((nb_p, heads_d, dk, dk), F32),
                              lw['conv_w'], alog_row, dtb_row, dn,
                              chunk=chunk_p, valid=chunk_p, heads=heads_d, dk=dk, z_col_block=z_cb)
    qkv_s = proj[tp:, :conv_dim].reshape(nb_s, seq_s, conv_dim)
    proj_s8 = _pad_rows(proj[tp:].reshape(nb_s, seq_s, -1), 8).reshape(nb_s * 8, -1)
    ba_s8 = _pad_rows(ba[tp:].reshape(nb_s, seq_s, LANES), 8).reshape(nb_s * 8, LANES)
    tail_s = jnp.pad(state_conv, ((0, 0), (8 - state_conv.shape[1], 0), (0, 0)))
    oa_s8, s_s = _delta_branch(proj_s8, ba_s8, 0, nb_s, 8, tail_s, state_delta,
                               lw['conv_w'], alog_row, dtb_row, dn,
                               chunk=8, valid=seq_s, heads=heads_d, dk=dk, z_col_block=z_cb)
    oa_s = oa_s8.reshape(nb_s, 8, hk)[:, :seq_s].reshape(ts, hk)
    o_a = jnp.concatenate([oa_p, oa_s], axis=0)
    conv_p = jnp.stack([proj[(b + 1) * seq_p - 3:(b + 1) * seq_p, :conv_dim] for b in range(nb_p)])
    conv_s = jnp.concatenate([state_conv, qkv_s], axis=1)[:, -3:, :]

    half = comp // 2
    inv_freq = ROPE_THETA ** (-jnp.arange(half, dtype=F32) / half)
    def rope_tables(pos):
        ang = pos.astype(F32)[:, None] * inv_freq[None, :]
        sin_h = jnp.sin(ang)
        return (jnp.tile(jnp.cos(ang), (1, LANES // half)),
                jnp.tile(jnp.concatenate([-sin_h, sin_h], axis=1), (1, LANES // comp)))

    q_gain = jnp.tile(lw['q_norm'], LANES // comp).reshape(1, LANES)
    k_gain = jnp.tile(lw['k_norm'], LANES // comp).reshape(1, LANES)
    tk = _row_tile(seq_p, 512, align=LANES)
    q_t, k_bf, v_t, k_t = _qkprep(proj, 0, nb_p, seq_p, *rope_tables(jnp.arange(seq_p, dtype=jnp.int32)),
                                  q_gain, k_gain, q_col_block=q_cb, width=width, comp=comp, for_prompt=True, tk=tk)
    pos_s = jnp.tile(past_len + jnp.arange(seq_s, dtype=jnp.int32), nb_s)
    q_fs, k_fs = _qkprep(proj, tp, 1, ts, *rope_tables(pos_s), q_gain, k_gain,
                         q_col_block=q_cb, width=width, comp=comp, for_prompt=False)
    lam_vecs = jnp.stack([lw['lam_q1'], lw['lam_k1'], lw['lam_q2'], lw['lam_k2']])
    subln = lw['subln'].reshape(1, dv)
    ob_p = _attn_prompt(q_t, k_bf, v_t, lam_vecs, lw['subln'].reshape(dv, 1), nb=nb_p, seq=seq_p, tk=tk,
                        heads=heads_a, comp=comp, lam_init=lam_init)
    v_f_s = proj[tp:, (q_cb + 2) * width:(q_cb + 3) * width]
    q8 = _pad_rows(q_fs.reshape(nb_s, seq_s, width), 8)
    k8 = _pad_rows(k_fs.reshape(nb_s, seq_s, width), 8)
    v8 = _pad_rows(v_f_s.reshape(nb_s, seq_s, width), 8)
    n_phys, page = cache_k.shape[0], cache_k.shape[1]
    cache_kt = jnp.transpose(cache_k, (0, 2, 3, 4, 1)).reshape(n_phys, width, page)
    cache_v2 = cache_v.reshape(n_phys, page * heads_a, dv)
    ob_s8 = _attn_sample(q8, k8, v8, cache_kt, cache_v2, page_table, lam_vecs, subln,
                         new_len=seq_s, heads=heads_a, comp=comp, lam_init=lam_init)
    o_b = jnp.concatenate([ob_p, ob_s8[:, :seq_s].reshape(ts, width)], axis=0)

    merged = _merge(o_a, o_b, proj, lw['w_oa'].astype(BF16), lw['w_ob'].astype(BF16), gate_col_block=gate_cb)
    w_route = jnp.pad(jnp.concatenate([lw['w_rg'], lw['w_re']], axis=1),
                      ((0, 0), (0, LANES - n_groups - n_experts))).astype(BF16)
    b_route = jnp.pad(jnp.concatenate([lw['b_rg'], lw['b_re']]), (0, LANES - n_groups - n_experts)).reshape(1, LANES)
    x1, h2, route_w, route_i, route_cnt = _outproj_route(x_all, merged, lw['w_out'].astype(BF16),
                                                         lw['norm_ffn'].reshape(1, d), w_route, b_route,
                                                         n_groups=n_groups, per_group=n_experts // n_groups)

    counts = route_cnt[0, n_groups:n_groups + n_experts].astype(jnp.int32)
    block_e, src_tok, dst_slot, nused = _expert_plan(route_i[:, :TOP_K], route_i[:, TOP_K:2 * TOP_K], counts, tp)
    y_slots = _experts(h2, block_e, src_tok, dst_slot, nused, lw['exp_gate'], lw['exp_up'], lw['exp_down'])

    ple_args = (x1, y_slots, route_w, p_all.astype(BF16), lw['w_ple'].astype(BF16),
                lw['norm_ple'].reshape(1, d), lw['w_ple_gate'].astype(BF16))
    y_p = _combine_ple(*ple_args, row0=0, rows=tp, slot_rows=(0, tp))
    y_s = _combine_ple(*ple_args, row0=tp, rows=ts, slot_rows=(TOP_K * tp, TOP_K * tp + ts))
    y = (y_p, y_s)

    k_p = jnp.transpose(k_t.reshape(nb_p, heads_a, 2, comp, seq_p), (0, 4, 1, 2, 3))
    v_p = proj[:tp, (q_cb + 2) * width:(q_cb + 3) * width].reshape(nb_p, seq_p, heads_a, dv)
    k_s = k_fs.reshape(nb_s, seq_s, heads_a, 2, comp)
    v_s = v_f_s.reshape(nb_s, seq_s, heads_a, dv)
    return y, k_p, v_p, k_s, v_s, conv_p, conv_s, s_p, s_s


def kernel(x_prompt, x_sample, cache_k, cache_v, state_conv, state_delta, page_table, p_prompt, p_sample,
           norm_attn, w_in, conv_w, a_log, dt_bias, delta_norm, q_norm, k_norm, lam_q1, lam_k1, lam_q2, lam_k2,
           subln, w_oa, w_ob, w_out, norm_ffn, w_rg, b_rg, w_re, b_re, exp_gate, exp_up, exp_down,
           norm_ple, w_ple, w_ple_gate):
    nb_p, seq_p, d = x_prompt.shape
    nb_s, seq_s, _ = x_sample.shape
    tp = nb_p * seq_p
    depth = w_in.shape[0]
    past_len = page_table.shape[1] * cache_k.shape[2]
    y_p, y_s = x_prompt.reshape(tp, d), x_sample.reshape(nb_s * seq_s, d)
    outs = [[] for _ in range(8)]
    for i in range(depth):
        x_all = jnp.concatenate([y_p, y_s], axis=0)
        lw = dict(norm_attn=norm_attn[i], w_in=w_in[i], conv_w=conv_w[i], a_log=a_log[i], dt_bias=dt_bias[i],
                  delta_norm=delta_norm[i], q_norm=q_norm[i], k_norm=k_norm[i], lam_q1=lam_q1[i], lam_k1=lam_k1[i],
                  lam_q2=lam_q2[i], lam_k2=lam_k2[i], subln=subln[i], w_oa=w_oa[i], w_ob=w_ob[i], w_out=w_out[i],
                  norm_ffn=norm_ffn[i], w_rg=w_rg[i], b_rg=b_rg[i], w_re=w_re[i], b_re=b_re[i],
                  exp_gate=exp_gate[i], exp_up=exp_up[i], exp_down=exp_down[i],
                  norm_ple=norm_ple[i], w_ple=w_ple[i], w_ple_gate=w_ple_gate[i])
        lam_init = 0.8 - 0.6 * math.exp(-0.3 * i)
        p_all = jnp.concatenate([p_prompt[i].reshape(tp, -1), p_sample[i].reshape(nb_s * seq_s, -1)], axis=0)
        res = _layer(x_all, p_all, nb_p, seq_p, nb_s, seq_s, past_len, state_conv[i], state_delta[i],
                     cache_k[i], cache_v[i], page_table, lam_init, lw)
        y_p, y_s = res[0]
        for lst, val in zip(outs, res[1:]):
            lst.append(val)
    return (y_p.reshape(nb_p, seq_p, d), y_s.reshape(nb_s, seq_s, d)) + tuple(jnp.stack(lst) for lst in outs)
```

```python
import functools
import math

import jax
import jax.numpy as jnp
from jax import lax
from jax.experimental import pallas as pl
from jax.experimental.pallas import tpu as pltpu

F32 = jnp.float32
BF16 = jnp.bfloat16
HIGHEST = lax.Precision.HIGHEST

LANES = 128
DELTA_CHUNK = 64
ROPE_THETA = 10000.0
TOP_K = 2
MOE_BLOCK = 128
EPS = 1e-6
VMEM_LIMIT = 48 * 1024 * 1024


def _cparams(*sem):
    return pltpu.CompilerParams(dimension_semantics=sem, vmem_limit_bytes=VMEM_LIMIT)


def _row_tile(n, target, align=16):
    best = None
    for t in range(align, min(n, target) + 1, align):
        if n % t == 0:
            best = t
    assert best is not None, (n, target, align)
    return best


def _nt_dot(a, b, precision=None):
    return lax.dot_general(a, b, (((1,), (1,)), ((), ())), precision=precision,
                           preferred_element_type=F32)


def _tn_dot(a, b, precision=None):
    return lax.dot_general(a, b, (((0,), (0,)), ((), ())), precision=precision,
                           preferred_element_type=F32)


def _dot(a, b, precision=None):
    return jnp.dot(a, b, precision=precision, preferred_element_type=F32)


def _split_bf16(a):
    hi = a.astype(BF16)
    return hi, (a - hi.astype(F32)).astype(BF16)


def _dot_split(a, b):
    return _dot(a[0], b[0]) + (_dot(a[0], b[1]) + _dot(a[1], b[0]))


def _sigmoid(x):
    return 1.0 / (1.0 + jnp.exp(-x))


def _silu(x):
    return x * _sigmoid(x)


def _iota(shape, dim):
    return lax.broadcasted_iota(jnp.int32, shape, dim)


def _inproj_kernel(x_ref, g_ref, *rest, part_tiles):
    w_refs = rest[:len(part_tiles)]
    wba_ref, p_ref, ba_ref, h_scr = rest[len(part_tiles):]
    j = pl.program_id(1)

    @pl.when(j == 0)
    def _():
        x = x_ref[...]
        h = x * lax.rsqrt(jnp.mean(x * x, axis=-1, keepdims=True) + EPS) * g_ref[...]
        h_scr[...] = h.astype(BF16)
        ba_ref[...] = _dot(h_scr[...], wba_ref[...])

    lo = 0
    for w_ref, n_tiles in zip(w_refs, part_tiles):
        @pl.when((j >= lo) & (j < lo + n_tiles))
        def _(w_ref=w_ref):
            p_ref[...] = _dot(h_scr[...], w_ref[...])
        lo += n_tiles


def _inproj(x, gain, w_parts, w_ba):
    t, d = x.shape
    tm = _row_tile(t, 640)
    tn = 1024
    part_tiles = tuple(w.shape[1] // tn for w in w_parts)
    assert all(w.shape[1] % tn == 0 for w in w_parts)
    n_tiles = sum(part_tiles)
    starts = [sum(part_tiles[:k]) for k in range(len(w_parts))]

    def w_spec(k):
        return pl.BlockSpec((d, tn), lambda i, j: (0, jnp.clip(j - starts[k], 0, part_tiles[k] - 1)))

    return pl.pallas_call(
        functools.partial(_inproj_kernel, part_tiles=part_tiles),
        out_shape=(jax.ShapeDtypeStruct((t, n_tiles * tn), F32), jax.ShapeDtypeStruct((t, LANES), F32)),
        grid=(t // tm, n_tiles),
        in_specs=[pl.BlockSpec((tm, d), lambda i, j: (i, 0)),
                  pl.BlockSpec((1, d), lambda i, j: (0, 0))]
                 + [w_spec(k) for k in range(len(w_parts))]
                 + [pl.BlockSpec((d, LANES), lambda i, j: (0, 0))],
        out_specs=(pl.BlockSpec((tm, tn), lambda i, j: (i, j)),
                   pl.BlockSpec((tm, LANES), lambda i, j: (i, 0))),
        scratch_shapes=[pltpu.VMEM((tm, d), BF16)],
        compiler_params=_cparams("parallel", "arbitrary"),
        name="inproj",
    )(x, gain, *w_parts, w_ba)


def _delta_kernel(qkv_ref, z_ref, ba_ref, tail0_ref, s0_ref, cw_ref, alog_ref, dtb_ref, dn_ref,
                  o_ref, sfin_ref, ext_scr, s_scr, *, chunk, valid, heads, dk):
    n = pl.program_id(1)
    c = chunk
    hk = heads * dk

    @pl.when(n == 0)
    def _():
        ext_scr[0:8, :] = tail0_ref[0]
        s_scr[...] = s0_ref[0]

    ext_scr[8:8 + c, :] = qkv_ref[...]

    row = _iota((c, 1), 0)
    rowmask = (row < valid).astype(F32) if valid < c else None

    ba = ba_ref[...]
    beta_all = _sigmoid(ba)
    xg = ba + dtb_ref[...]
    softplus = jnp.maximum(xg, 0.0) + jnp.log1p(jnp.exp(-jnp.abs(xg)))
    g_all = -jnp.exp(alog_ref[...]) * softplus
    if rowmask is not None:
        beta_all = beta_all * rowmask
        g_all = g_all * rowmask
    r_i = _iota((c, c), 0)
    c_i = _iota((c, c), 1)
    incl = r_i >= c_i
    strict = r_i > c_i
    gc_all = _dot(incl.astype(F32), g_all, precision=HIGHEST)
    gc_t = jnp.transpose(gc_all)
    eye = (r_i == c_i).astype(F32)

    def conv(col0):
        sl = slice(col0, col0 + dk)
        acc = ext_scr[8:8 + c, sl] * cw_ref[3:4, sl]
        acc = acc + ext_scr[7:7 + c, sl] * cw_ref[2:3, sl]
        acc = acc + ext_scr[6:6 + c, sl] * cw_ref[1:2, sl]
        acc = acc + ext_scr[5:5 + c, sl] * cw_ref[0:1, sl]
        return _silu(acc)

    hs = range(heads)
    q, k, v, beta, gc, egc, g_last, decay = [], [], [], [], [], [], [], []
    for h in hs:
        qh = conv(h * dk)
        kh = conv(hk + h * dk)
        vh = conv(2 * hk + h * dk)
        qh = qh * lax.rsqrt(jnp.sum(qh * qh, axis=-1, keepdims=True) + EPS) * (dk ** -0.5)
        kh = kh * lax.rsqrt(jnp.sum(kh * kh, axis=-1, keepdims=True) + EPS)
        if rowmask is not None:
            qh, kh, vh = qh * rowmask, kh * rowmask, vh * rowmask
        q.append(qh)
        k.append(kh)
        v.append(vh)
        beta.append(beta_all[:, h:h + 1])
        gc.append(gc_all[:, heads + h:heads + h + 1])
        egc.append(jnp.exp(gc[h]))
        g_last.append(gc_all[c - 1:c, heads + h:heads + h + 1])
        gc_row = gc_t[heads + h:heads + h + 1, :]
        decay.append(jnp.where(incl, jnp.exp(jnp.where(incl, gc[h] - gc_row, 0.0)), 0.0))
    kb = [k[h] * beta[h] for h in hs]
    kk = [_nt_dot(kb[h], k[h]) for h in hs]
    qk = [_nt_dot(q[h], k[h]) for h in hs]
    apow = [jnp.where(strict, kk[h] * decay[h], 0.0) for h in hs]
    qk = [qk[h] * decay[h] for h in hs]
    a_mat = apow
    same_block = lambda b: (r_i // b) == (c_i // b)
    b0 = min(8, c)
    diag0 = same_block(b0)
    apb = [jnp.where(diag0, a_mat[h], 0.0) for h in hs]
    tinv = [eye - apb[h] for h in hs]
    apb = [apb[h].astype(BF16) for h in hs]
    span = 2
    while span < b0:
        apb = [_dot(apb[h], apb[h]).astype(BF16) for h in hs]
        tinv = [tinv[h] + _dot(tinv[h].astype(BF16), apb[h]) for h in hs]
        span *= 2
    b = b0
    while b < c:
        ring = same_block(2 * b) & ((r_i // b) != (c_i // b))
        off = [jnp.where(ring, a_mat[h], 0.0).astype(BF16) for h in hs]
        tb = [tinv[h].astype(BF16) for h in hs]
        tl = [_dot(tb[h], off[h]).astype(BF16) for h in hs]
        tinv = [tinv[h] - _dot(tl[h], tb[h]) for h in hs]
        b *= 2
    resid = [eye - _dot_split(_split_bf16(eye + a_mat[h]), _split_bf16(tinv[h])) for h in hs]
    tinv = [tinv[h] + _dot(tinv[h].astype(BF16), resid[h].astype(BF16)) for h in hs]
    u = [_dot(tinv[h], v[h] * beta[h]) for h in hs]
    w = [_dot(tinv[h], kb[h] * egc[h]) for h in hs]
    s = [s_scr[h] for h in hs]
    ws = [_dot(w[h], s[h]) for h in hs]
    qs = [_dot(q[h] * egc[h], s[h]) for h in hs]
    v_new = [u[h] - ws[h] for h in hs]
    o = [qs[h] + _dot(qk[h], v_new[h]) for h in hs]
    ds = [_tn_dot(k[h] * jnp.exp(g_last[h] - gc[h]), v_new[h]) for h in hs]
    for h in hs:
        s_scr[h] = s[h] * jnp.exp(g_last[h]) + ds[h]
        on = o[h] * lax.rsqrt(jnp.mean(o[h] * o[h], axis=-1, keepdims=True) + EPS) * dn_ref[...]
        zh = z_ref[:, h * dk:(h + 1) * dk]
        o_ref[:, h * dk:(h + 1) * dk] = (on * _silu(zh)).astype(o_ref.dtype)

    ext_scr[0:8, :] = ext_scr[c:c + 8, :]

    @pl.when(n == pl.num_programs(1) - 1)
    def _():
        sfin_ref[0] = s_scr[...]


def _delta_branch(proj, ba, row0, nb, seq, tail0, s0, conv_w, alog_row, dtb_row, delta_norm, *,
                  chunk, valid, heads, dk, z_col_block):
    hk = heads * dk
    nchunks = seq // chunk
    assert seq % chunk == 0 and row0 % chunk == 0
    r0 = row0 // chunk
    kern = functools.partial(_delta_kernel, chunk=chunk, valid=valid, heads=heads, dk=dk)
    return pl.pallas_call(
        kern,
        out_shape=(jax.ShapeDtypeStruct((nb * seq, hk), BF16),
                   jax.ShapeDtypeStruct((nb, heads, dk, dk), F32)),
        grid=(nb, nchunks),
        in_specs=[pl.BlockSpec((chunk, 3 * hk), lambda b, n: (r0 + b * nchunks + n, 0)),
                  pl.BlockSpec((chunk, hk), lambda b, n: (r0 + b * nchunks + n, z_col_block)),
                  pl.BlockSpec((chunk, LANES), lambda b, n: (r0 + b * nchunks + n, 0)),
                  pl.BlockSpec((1, 8, 3 * hk), lambda b, n: (b, 0, 0)),
                  pl.BlockSpec((1, heads, dk, dk), lambda b, n: (b, 0, 0, 0)),
                  pl.BlockSpec((4, 3 * hk), lambda b, n: (0, 0)),
                  pl.BlockSpec((1, LANES), lambda b, n: (0, 0)),
                  pl.BlockSpec((1, LANES), lambda b, n: (0, 0)),
                  pl.BlockSpec((1, dk), lambda b, n: (0, 0))],
        out_specs=(pl.BlockSpec((chunk, hk), lambda b, n: (b * nchunks + n, 0)),
                   pl.BlockSpec((1, heads, dk, dk), lambda b, n: (b, 0, 0, 0))),
        scratch_shapes=[pltpu.VMEM((8 + chunk, 3 * hk), F32),
                        pltpu.VMEM((heads, dk, dk), F32)],
        compiler_params=_cparams("parallel", "arbitrary"),
        name="delta_c%d" % chunk,
    )(proj, proj, ba, tail0, s0, conv_w, alog_row, dtb_row, delta_norm)


def _qkprep_kernel(q_ref, k_ref, v_ref, cos_ref, sin_ref, qg_ref, kg_ref, *out_refs, comp, scale, for_prompt, tk):
    tm = q_ref.shape[0]
    r_i = _iota((LANES, LANES), 0)
    c_i = _iota((LANES, LANES), 1)
    group = (r_i // comp == c_i // comp).astype(BF16)
    lane = _iota((tm, LANES), 1)
    first_half = (lane % comp) < (comp // 2)
    cos = cos_ref[...]
    sin = sin_ref[...]

    def norm_rope(x, gain):
        sq = x * x
        hi = sq.astype(BF16)
        lo = (sq - hi.astype(F32)).astype(BF16)
        ms = (_dot(hi, group) + _dot(lo, group)) * (1.0 / comp)
        y = x * lax.rsqrt(ms + EPS) * gain
        swapped = jnp.where(first_half, pltpu.roll(y, LANES - comp // 2, 1), pltpu.roll(y, comp // 2, 1))
        return y * cos + swapped * sin

    for j in range(q_ref.shape[1] // LANES):
        sl = slice(j * LANES, (j + 1) * LANES)
        qr = norm_rope(q_ref[:, sl], qg_ref[...]) * scale
        kr = norm_rope(k_ref[:, sl], kg_ref[...])
        if for_prompt:
            qt_ref, kbo_ref, vt_ref, kt_ref = out_refs
            qt_ref[0, sl, :] = jnp.transpose(qr).astype(BF16)
            kbo_ref[:, sl] = kr.astype(BF16)
            kt_ref[0, sl, :] = jnp.transpose(kr)
            vt = jnp.transpose(v_ref[:, sl]).astype(BF16)
            for u in range(tm // tk):
                vt_ref[0, u, sl, :] = vt[:, u * tk:(u + 1) * tk]
        else:
            qo_ref, kfo_ref = out_refs
            qo_ref[:, sl] = qr
            kfo_ref[:, sl] = kr


def _qkprep(proj, row0, nb, seq, cos_t, sin_t, q_gain, k_gain, *, q_col_block, width, comp, for_prompt, tk=None):
    tm = _row_tile(seq, 512, align=LANES) if for_prompt else seq
    assert row0 % tm == 0 and seq % tm == 0 and (not for_prompt or tm % tk == 0)
    r0, per_seq = row0 // tm, seq // tm
    scale = comp ** -0.5 * (math.log2(math.e) if for_prompt else 1.0)
    kern = functools.partial(_qkprep_kernel, comp=comp, scale=scale, for_prompt=for_prompt, tk=tk)
    col = lambda cb: pl.BlockSpec((tm, width), lambda i: (r0 + i, cb))
    small = pl.BlockSpec((tm, LANES), lambda i: (i % per_seq, 0))
    gain = pl.BlockSpec((1, LANES), lambda i: (0, 0))
    out = pl.BlockSpec((tm, width), lambda i: (i, 0))
    rows = nb * seq
    if for_prompt:
        out_t = pl.BlockSpec((1, width, tm), lambda i: (i // per_seq, 0, i % per_seq))
        out_shape = (jax.ShapeDtypeStruct((nb, width, seq), BF16), jax.ShapeDtypeStruct((rows, width), BF16),
                     jax.ShapeDtypeStruct((nb, seq // tk, width, tk), BF16),
                     jax.ShapeDtypeStruct((nb, width, seq), F32))
        out_specs = (out_t, out,
                     pl.BlockSpec((1, tm // tk, width, tk), lambda i: (i // per_seq, i % per_seq, 0, 0)), out_t)
    else:
        out_shape = (jax.ShapeDtypeStruct((rows, width), F32), jax.ShapeDtypeStruct((rows, width), F32))
        out_specs = (out, out)
    return pl.pallas_call(
        kern,
        out_shape=out_shape,
        grid=(rows // tm,),
        in_specs=[col(q_col_block), col(q_col_block + 1), col(q_col_block + 2), small, small, gain, gain],
        out_specs=out_specs,
        compiler_params=_cparams("parallel"),
        name="qkprep_prompt" if for_prompt else "qkprep_sample",
    )(proj, proj, proj, cos_t, sin_t, q_gain, k_gain)


def _lambda_value(lam_ref, lam_init):
    lv = lam_ref[...]
    s1 = jnp.sum(lv[0:1] * lv[1:2], axis=-1, keepdims=True)
    s2 = jnp.sum(lv[2:3] * lv[3:4], axis=-1, keepdims=True)
    return jnp.exp(s1) - jnp.exp(s2) + lam_init


def _attn_prompt_kernel(q_ref, k_ref, v_ref, lam_ref, sub_ref, o_ref, *, tq, tk, comp, heads_per_step, lam_init):
    i = pl.program_id(2)
    dv = 2 * comp
    hr = range(heads_per_step)
    row = _iota((dv, tq), 0)
    qs = []
    for hh in hr:
        qt = q_ref[0, hh * dv:(hh + 1) * dv, :]
        zero = jnp.zeros_like(qt)
        qs.append(jnp.concatenate([jnp.where(row < comp, qt, zero), jnp.where(row >= comp, qt, zero)], axis=1))

    def update(carry, j, mask):
        start = pl.multiple_of(j * tk, tk)
        kt = [k_ref[pl.ds(start, tk), hh * dv:(hh + 1) * dv] for hh in hr]
        vt = [v_ref[0, j, hh * dv:(hh + 1) * dv, :] for hh in hr]
        s = [_dot(kt[hh], qs[hh]) for hh in hr]
        if mask is not None:
            s = [jnp.where(mask, sn, -jnp.inf) for sn in s]
        m_new = [jnp.maximum(carry[hh][0], jnp.max(s[hh], axis=0, keepdims=True)) for hh in hr]
        p = [jnp.exp2(s[hh] - m_new[hh]) for hh in hr]
        corr = [jnp.exp2(carry[hh][0] - m_new[hh]) for hh in hr]
        l = [carry[hh][1] * corr[hh] + jnp.sum(p[hh], axis=0, keepdims=True) for hh in hr]
        pv = [_dot(vt[hh], p[hh].astype(BF16)) for hh in hr]
        return tuple((m_new[hh], l[hh], carry[hh][2] * corr[hh] + pv[hh]) for hh in hr)

    def body(j, carry):
        return update(carry, j, None)

    init = tuple((jnp.full((1, 2 * tq), -jnp.inf, F32), jnp.zeros((1, 2 * tq), F32),
                  jnp.zeros((dv, 2 * tq), F32)) for _ in hr)
    n_full = (i * tq) // tk
    carry = lax.fori_loop(0, n_full, body, init)
    kpos = n_full * tk + _iota((tk, 2 * tq), 0)
    qpos = i * tq + _iota((tk, 2 * tq), 1) % tq
    carry = update(carry, n_full, qpos >= kpos)
    lam = _lambda_value(lam_ref, lam_init)
    for hh in hr:
        _, l, a = carry[hh]
        n = a / l
        o = n[:, 0:tq] - lam * n[:, tq:2 * tq]
        on = o * lax.rsqrt(jnp.mean(o * o, axis=0, keepdims=True) + EPS) * sub_ref[...]
        o_ref[:, hh * dv:(hh + 1) * dv] = jnp.transpose(on * (1.0 - lam_init)).astype(o_ref.dtype)


def _attn_prompt(q_t, k, v_t, lam_vecs, subln_col, *, nb, seq, tk, heads, comp, lam_init):
    dv = 2 * comp
    tq = _row_tile(seq, 256, align=LANES)
    assert tk % tq == 0 and seq % tk == 0
    nq = seq // tq
    hps = 2
    assert heads % hps == 0
    kern = functools.partial(_attn_prompt_kernel, tq=tq, tk=tk, comp=comp, heads_per_step=hps, lam_init=lam_init)
    return pl.pallas_call(
        kern,
        out_shape=jax.ShapeDtypeStruct((nb * seq, heads * dv), BF16),
        grid=(nb, heads // hps, nq),
        in_specs=[pl.BlockSpec((1, hps * dv, tq), lambda b, h, i: (b, h, i)),
                  pl.BlockSpec((seq, hps * dv), lambda b, h, i: (b, h)),
                  pl.BlockSpec((1, seq // tk, hps * dv, tk), lambda b, h, i: (b, 0, h, 0)),
                  pl.BlockSpec((4, comp), lambda b, h, i: (0, 0)),
                  pl.BlockSpec((dv, 1), lambda b, h, i: (0, 0))],
        out_specs=pl.BlockSpec((tq, hps * dv), lambda b, h, i: (b * nq + i, h)),
        compiler_params=_cparams("parallel", "parallel", "arbitrary"),
        name="attn_prompt",
    )(q_t, k, v_t, lam_vecs, subln_col)


def _attn_sample_kernel(pt_ref, q_ref, kn_ref, vn_ref, lam_ref, sub_ref, *rest,
                        pages_per_step, new_len, heads, comp, lam_init):
    pp = pages_per_step
    kt_refs = rest[:pp]
    v_refs = rest[pp:2 * pp]
    o_ref = rest[2 * pp]
    qbd_scr, m_scr, l_scr, acc_scr = rest[2 * pp + 1:]
    p_id = pl.program_id(1)
    rph = 2 * new_len
    rows = heads * rph
    dv = 2 * comp
    width = heads * dv
    page = kt_refs[0].shape[2]

    @pl.when(p_id == 0)
    def _():
        q = q_ref[0][0:new_len, :]
        lane_grp = _iota((new_len, width), 1) // comp
        for h in range(heads):
            for c in range(2):
                r0 = h * rph + c * new_len
                qbd_scr[r0:r0 + new_len, :] = jnp.where(lane_grp == 2 * h + c, q, 0.0)
        m_scr[...] = jnp.full(m_scr.shape, -jnp.inf, F32)
        l_scr[...] = jnp.zeros(l_scr.shape, F32)
        acc_scr[...] = jnp.zeros(acc_scr.shape, F32)

    qbd = qbd_scr[...]

    def online(s_list, v_of):
        m = m_scr[...]
        m_new = m
        for s in s_list:
            m_new = jnp.maximum(m_new, jnp.max(s, axis=-1, keepdims=True))
        corr = jnp.exp(m - m_new)
        p_list = [jnp.exp(s - m_new) for s in s_list]
        l = l_scr[...] * corr
        for p in p_list:
            l = l + jnp.sum(p, axis=-1, keepdims=True)
        m_scr[...] = m_new
        l_scr[...] = l
        for h in range(heads):
            hs = slice(h * rph, (h + 1) * rph)
            acc = acc_scr[hs, :] * corr[hs, :]
            for j, p in enumerate(p_list):
                acc = acc + _dot(p[hs, :], v_of(j, h))
            acc_scr[hs, :] = acc

    online([_dot(qbd, kt_refs[r][0]) for r in range(pp)],
           lambda j, h: v_refs[j][0, pl.ds(h, page, stride=heads), :])

    @pl.when(p_id == pl.num_programs(1) - 1)
    def _():
        s = _nt_dot(qbd, kn_ref[0])
        tok = _iota((rows, 8), 0) % new_len
        key = _iota((rows, 8), 1)
        s = jnp.where((tok >= key) & (key < new_len), s, -jnp.inf)
        online([s], lambda j, h: vn_ref[0][:, h * dv:(h + 1) * dv])
        lam = _lambda_value(lam_ref, lam_init)
        nrm = acc_scr[...] / l_scr[...]
        for h in range(heads):
            r0 = h * rph
            oh = nrm[r0:r0 + new_len, :] - lam * nrm[r0 + new_len:r0 + rph, :]
            on = oh * lax.rsqrt(jnp.mean(oh * oh, axis=-1, keepdims=True) + EPS) * sub_ref[...]
            o_ref[0, 0:new_len, h * dv:(h + 1) * dv] = (on * (1.0 - lam_init)).astype(o_ref.dtype)
        if new_len < 8:
            o_ref[0, new_len:8, :] = jnp.zeros((8 - new_len, width), o_ref.dtype)


def _attn_sample(q8, k8, v8, cache_kt, cache_v2, page_table, lam_vecs, subln, *, new_len, heads, comp, lam_init):
    nb, n_pages = page_table.shape
    width, page = cache_kt.shape[1], cache_kt.shape[2]
    dv = 2 * comp
    pp = math.gcd(n_pages, 16)
    assert 2 * new_len == 8
    rows = heads * 2 * new_len
    kern = functools.partial(_attn_sample_kernel, pages_per_step=pp, new_len=new_len, heads=heads,
                             comp=comp, lam_init=lam_init)
    per_req = pl.BlockSpec((1, 8, width), lambda b, p, pt: (b, 0, 0))

    def page_spec(r, shape):
        return pl.BlockSpec((1,) + shape, lambda b, p, pt: (pt[b * n_pages + p * pp + r], 0, 0))

    grid_spec = pltpu.PrefetchScalarGridSpec(
        num_scalar_prefetch=1,
        grid=(nb, n_pages // pp),
        in_specs=[per_req, per_req, per_req,
                  pl.BlockSpec((4, comp), lambda b, p, pt: (0, 0)),
                  pl.BlockSpec((1, dv), lambda b, p, pt: (0, 0))]
                 + [page_spec(r, (width, page)) for r in range(pp)]
                 + [page_spec(r, (page * heads, dv)) for r in range(pp)],
        out_specs=pl.BlockSpec((1, 8, width), lambda b, p, pt: (b, 0, 0)),
        scratch_shapes=[pltpu.VMEM((rows, width), F32), pltpu.VMEM((rows, 1), F32),
                        pltpu.VMEM((rows, 1), F32), pltpu.VMEM((rows, dv), F32)],
    )
    return pl.pallas_call(
        kern,
        out_shape=jax.ShapeDtypeStruct((nb, 8, width), BF16),
        grid_spec=grid_spec,
        compiler_params=_cparams("parallel", "arbitrary"),
        name="attn_sample",
    )(page_table.reshape(-1), q8, k8, v8, lam_vecs, subln, *([cache_kt] * pp), *([cache_v2] * pp))


def _merge_kernel(oa_ref, ob_ref, ga_ref, gb_ref, woa_ref, wob_ref, m_ref):
    ua = _dot(oa_ref[...], woa_ref[...])
    ub = _dot(ob_ref[...], wob_ref[...])
    m_ref[...] = (_sigmoid(ga_ref[...]) * ua + _sigmoid(gb_ref[...]) * ub).astype(m_ref.dtype)


def _merge(o_a, o_b, proj, w_oa, w_ob, *, gate_col_block):
    t, zd = o_a.shape
    d = w_oa.shape[1]
    tm = _row_tile(t, 640)
    once = lambda shape: pl.BlockSpec(shape, lambda i: (0, 0), pipeline_mode=pl.Buffered(1))
    return pl.pallas_call(
        _merge_kernel,
        out_shape=jax.ShapeDtypeStruct((t, d), BF16),
        grid=(t // tm,),
        in_specs=[pl.BlockSpec((tm, zd), lambda i: (i, 0)),
                  pl.BlockSpec((tm, o_b.shape[1]), lambda i: (i, 0)),
                  pl.BlockSpec((tm, d), lambda i: (i, gate_col_block)),
                  pl.BlockSpec((tm, d), lambda i: (i, gate_col_block + 1)),
                  once(w_oa.shape), once(w_ob.shape)],
        out_specs=pl.BlockSpec((tm, d), lambda i: (i, 0)),
        compiler_params=_cparams("parallel"),
        name="merge",
    )(o_a, o_b, proj, proj, w_oa, w_ob)


def _outproj_route_kernel(x_ref, m_ref, wo_ref, g_ref, wr_ref, br_ref,
                          x1_ref, h2_ref, rw_ref, re_ref, cnt_ref, *, n_groups, per_group, parts):
    @pl.when(pl.program_id(0) == 0)
    def _():
        cnt_ref[...] = jnp.zeros(cnt_ref.shape, F32)

    tm = x_ref.shape[0] // parts
    nc = x_ref.shape[1] // LANES
    rows = [slice(k * tm, (k + 1) * tm) for k in range(parts)]
    x1s = [x_ref[sl, :] + _dot(m_ref[sl, :], wo_ref[...]) for sl in rows]
    lane = _iota((tm, LANES), 1).astype(F32)
    big = float(LANES)
    before = (_iota((tm, tm), 0) > _iota((tm, tm), 1)).astype(BF16)

    for k, sl in enumerate(rows):
        x1 = x1s[k]
        x1_ref[sl, :] = x1
        h2 = x1 * lax.rsqrt(jnp.mean(x1 * x1, axis=-1, keepdims=True) + EPS) * g_ref[...]
        for c in range(nc):
            h2_ref[pl.ds(k * tm * nc + c, tm, stride=nc), :] = h2[:, c * LANES:(c + 1) * LANES]
        logits = _dot(h2.astype(BF16), wr_ref[...]) + br_ref[...]

        def masked_softmax(mask):
            lm = jnp.where(mask, logits, -jnp.inf)
            e = jnp.exp(lm - jnp.max(lm, axis=-1, keepdims=True))
            return e / jnp.sum(e, axis=-1, keepdims=True)

        def top1(p, mask):
            v = jnp.max(jnp.where(mask, p, -1.0), axis=-1, keepdims=True)
            idx = jnp.min(jnp.where(mask & (p == v), lane, big), axis=-1, keepdims=True)
            return v, idx

        gmask = lane < n_groups
        pg = masked_softmax(gmask)
        g_w, g_top = top1(pg, gmask)
        lo = n_groups + g_top * per_group
        emask = (lane >= lo) & (lane < lo + per_group)
        pe = masked_softmax(emask)
        v1, i1 = top1(pe, emask)
        emask2 = emask & (lane != i1)
        v2, i2 = top1(pe, emask2)
        denom = v1 + v2
        w1 = g_w * v1 / denom
        w2 = g_w * v2 / denom
        rw_ref[sl, :] = jnp.where(lane == 0.0, w1, jnp.where(lane == 1.0, w2, 0.0))

        sel1 = lane == i1
        sel2 = lane == i2
        picked = jnp.where(sel1 | sel2, 1.0, 0.0)
        prior = _dot(before, picked.astype(BF16)) + cnt_ref[...]
        r1 = jnp.sum(jnp.where(sel1, prior, 0.0), axis=-1, keepdims=True)
        r2 = jnp.sum(jnp.where(sel2, prior, 0.0), axis=-1, keepdims=True)
        cnt_ref[...] = cnt_ref[...] + jnp.sum(picked, axis=0, keepdims=True)
        re_ref[sl, :] = jnp.where(lane == 0.0, i1 - n_groups,
                                  jnp.where(lane == 1.0, i2 - n_groups,
                                            jnp.where(lane == 2.0, r1,
                                                      jnp.where(lane == 3.0, r2, 0.0)))).astype(jnp.int32)


def _outproj_route(x, merged, w_out, gain, w_route, b_route, *, n_groups, per_group):
    t, d = x.shape
    parts = 2
    tm = _row_tile(t, 416, align=16 * parts)
    kern = functools.partial(_outproj_route_kernel, n_groups=n_groups, per_group=per_group, parts=parts)
    row = lambda w: pl.BlockSpec((tm, w), lambda i: (i, 0))
    once = lambda shape: pl.BlockSpec(shape, lambda i: (0, 0), pipeline_mode=pl.Buffered(1))
    return pl.pallas_call(
        kern,
        out_shape=(jax.ShapeDtypeStruct((t, d), F32), jax.ShapeDtypeStruct((t * (d // LANES), LANES), F32),
                   jax.ShapeDtypeStruct((t, LANES), F32), jax.ShapeDtypeStruct((t, LANES), jnp.int32),
                   jax.ShapeDtypeStruct((1, LANES), F32)),
        grid=(t // tm,),
        in_specs=[row(d), row(d), once(w_out.shape), once((1, d)), once((d, LANES)), once((1, LANES))],
        out_specs=(row(d), pl.BlockSpec((tm * (d // LANES), LANES), lambda i: (i, 0)), row(LANES), row(LANES),
                   pl.BlockSpec((1, LANES), lambda i: (0, 0))),
        compiler_params=_cparams("arbitrary"),
        name="outproj_route",
    )(x, merged, w_out, gain, w_route, b_route)


def _experts_kernel(be_ref, nused_ref, first_ref, wslot_ref, nexte_ref, src_ref, src_next_ref, dst_prev_ref, dst_ref,
                    h_hbm, wg_hbm, wu_hbm, wd_hbm, y_hbm, xbuf, ybuf, wgbuf, wubuf, wdbuf, gsem, ssem, wsem):
    g = pl.program_id(0)
    nused = nused_ref[0]
    blk = MOE_BLOCK
    nc = wg_hbm.shape[1] // LANES

    def weight_copies(e, ws):
        return [pltpu.make_async_copy(hbm.at[e], buf.at[ws], wsem.at[ws])
                for hbm, buf in ((wg_hbm, wgbuf), (wu_hbm, wubuf), (wd_hbm, wdbuf))]

    def start_gather(idx_ref, slot):
        for r in range(blk):
            row = pl.multiple_of(idx_ref[0, 0, r], nc)
            pltpu.make_async_copy(h_hbm.at[pl.ds(row, nc), :], xbuf.at[slot, pl.ds(r * nc, nc), :],
                                  gsem.at[slot]).start(priority=r % 2)

    def start_scatter(idx_ref, slot):
        for r in range(blk):
            row = pl.multiple_of(idx_ref[0, 0, r], nc)
            pltpu.make_async_copy(ybuf.at[slot, pl.ds(r * nc, nc), :], y_hbm.at[pl.ds(row, nc), :],
                                  ssem.at[slot]).start(priority=r % 2)

    def wait_gather(slot):
        pltpu.make_async_copy(h_hbm.at[pl.ds(0, blk * nc), :], xbuf.at[slot], gsem.at[slot]).wait()

    def wait_scatter(slot):
        pltpu.make_async_copy(ybuf.at[slot], y_hbm.at[pl.ds(0, blk * nc), :], ssem.at[slot]).wait()

    @pl.when(g < nused)
    def _():
        slot = g % 2

        @pl.when(g == 0)
        def _():
            ybuf[1] = jnp.zeros(ybuf.shape[1:], F32)
            start_gather(src_ref, slot)
            for cp in weight_copies(be_ref[0], wslot_ref[0]):
                cp.start()

        ws = wslot_ref[g]

        @pl.when(first_ref[g] == 1)
        def _():
            @pl.when(nexte_ref[g] >= 0)
            def _():
                for cp in weight_copies(nexte_ref[g], 1 - ws):
                    cp.start()
            for cp in weight_copies(be_ref[g], ws):
                cp.wait()

        wait_gather(slot)
        start_gather(src_next_ref, 1 - slot)
        start_scatter(dst_prev_ref, 1 - slot)
        x = jnp.concatenate([xbuf[slot, pl.ds(c, blk, stride=nc), :] for c in range(nc)], axis=1)
        hdn = _silu(_dot(x, wgbuf[ws])) * _dot(x, wubuf[ws])
        y = _dot(hdn, wdbuf[ws])

        @pl.when(g > 0)
        def _():
            wait_scatter(slot)

        for c in range(nc):
            ybuf[slot, pl.ds(c, blk, stride=nc), :] = y[:, c * LANES:(c + 1) * LANES]

        @pl.when(g == nused - 1)
        def _():
            start_scatter(dst_ref, slot)
            wait_scatter(1 - slot)
            wait_scatter(slot)
            wait_gather(1 - slot)


def _experts(h2, block_e, src_tok, dst_slot, nused, w_gate, w_up, w_down):
    d, ff = w_gate.shape[1], w_gate.shape[2]
    nc = d // LANES
    t = h2.shape[0] // nc
    n_blocks = block_e.shape[0]
    n_slots = t * TOP_K
    kern = _experts_kernel
    src3 = (src_tok * nc).reshape(n_blocks, 1, MOE_BLOCK)
    spare = (n_slots + jnp.arange(MOE_BLOCK, dtype=jnp.int32)) * nc
    dst3 = jnp.concatenate([spare, dst_slot * nc]).reshape(n_blocks + 1, 1, MOE_BLOCK)
    blk_id = jnp.arange(n_blocks, dtype=jnp.int32)
    used = blk_id < nused[0]
    first = used & ((blk_id == 0) | (block_e != jnp.roll(block_e, 1)))
    run_id = jnp.cumsum(first.astype(jnp.int32)) - 1
    wslot = (run_id % 2).astype(jnp.int32)
    run_expert = jnp.full((n_blocks + 1,), -1, jnp.int32).at[jnp.where(first, run_id, n_blocks)].set(block_e)
    next_e = run_expert[jnp.minimum(run_id + 1, n_blocks)]
    idx_spec = lambda f: pl.BlockSpec((1, 1, MOE_BLOCK), lambda g, *_: (f(g), 0, 0), memory_space=pltpu.SMEM)
    anyspec = pl.BlockSpec(memory_space=pl.ANY)
    grid_spec = pltpu.PrefetchScalarGridSpec(
        num_scalar_prefetch=5,
        grid=(n_blocks,),
        in_specs=[idx_spec(lambda g: g),
                  idx_spec(lambda g: jnp.minimum(g + 1, n_blocks - 1)),
                  idx_spec(lambda g: g),
                  idx_spec(lambda g: g + 1),
                  anyspec, anyspec, anyspec, anyspec],
        out_specs=anyspec,
        scratch_shapes=[pltpu.VMEM((2, MOE_BLOCK * nc, LANES), F32), pltpu.VMEM((2, MOE_BLOCK * nc, LANES), F32),
                        pltpu.VMEM((2, d, ff), F32), pltpu.VMEM((2, d, ff), F32), pltpu.VMEM((2, ff, d), F32),
                        pltpu.SemaphoreType.DMA((2,)), pltpu.SemaphoreType.DMA((2,)), pltpu.SemaphoreType.DMA((2,))],
    )
    return pl.pallas_call(
        kern,
        out_shape=jax.ShapeDtypeStruct(((n_slots + MOE_BLOCK) * nc, LANES), F32),
        grid_spec=grid_spec,
        compiler_params=_cparams("arbitrary"),
        name="experts",
    )(block_e, nused, first.astype(jnp.int32), wslot, next_e, src3, src3, dst3, dst3, h2, w_gate, w_up, w_down)


def _expert_plan(eidx, rank, counts, tp):
    t = eidx.shape[0]
    n_experts = counts.shape[0]
    s = t * TOP_K
    padded = (counts + MOE_BLOCK - 1) // MOE_BLOCK * MOE_BLOCK
    ends = jnp.cumsum(padded)
    pad_start = ends - padded
    dest = (pad_start[eidx] + rank).reshape(-1)
    n_blocks = -(-(s + n_experts * (MOE_BLOCK - 1)) // MOE_BLOCK)
    rows = n_blocks * MOE_BLOCK
    slot_of_row = jnp.full((rows,), -1, jnp.int32).at[dest].set(jnp.arange(s, dtype=jnp.int32))
    valid = slot_of_row >= 0
    tok = slot_of_row // TOP_K
    src_tok = jnp.where(valid, tok, 0)
    kk = slot_of_row % TOP_K
    dst_real = jnp.where(tok < tp, kk * tp + tok, TOP_K * tp + kk * (t - tp) + (tok - tp))
    dst_slot = jnp.where(valid, dst_real, s + jnp.arange(rows, dtype=jnp.int32) % MOE_BLOCK)
    block_start = jnp.arange(n_blocks, dtype=jnp.int32) * MOE_BLOCK
    block_e = jnp.minimum(jnp.sum((ends[None, :] <= block_start[:, None]).astype(jnp.int32), axis=1), n_experts - 1)
    nused = (ends[-1:] // MOE_BLOCK).astype(jnp.int32)
    return block_e, src_tok, dst_slot, nused


def _combine_ple_kernel(x1_ref, ys0_ref, ys1_ref, rw_ref, p_ref, wp_ref, g_ref, wpg_ref, y_ref):
    d = x1_ref.shape[1]
    nc = d // LANES
    parts = 2 if x1_ref.shape[0] % 32 == 0 else 1
    tm = x1_ref.shape[0] // parts

    def token_rows(ref, k):
        return jnp.concatenate([ref[pl.ds(k * tm * nc + c, tm, stride=nc), :] for c in range(nc)], axis=1)

    x2s, hns = [], []
    for k in range(parts):
        sl = slice(k * tm, (k + 1) * tm)
        rw = rw_ref[sl, :]
        x2 = x1_ref[sl, :] + (rw[:, 0:1] * token_rows(ys0_ref, k) + rw[:, 1:2] * token_rows(ys1_ref, k))
        x2s.append(x2)
        hns.append((x2 * lax.rsqrt(jnp.mean(x2 * x2, axis=-1, keepdims=True) + EPS) * g_ref[...]).astype(BF16))
    for k in range(parts):
        sl = slice(k * tm, (k + 1) * tm)
        gate = _sigmoid(_dot(hns[k], wpg_ref[...]))
        y_ref[sl, :] = x2s[k] + _dot(p_ref[sl, :], wp_ref[...]) * gate


def _combine_ple(x1, y_slots, route_w, p, w_ple, gain, w_ple_gate, *, row0, rows, slot_rows):
    t, d = x1.shape
    tm = _row_tile(math.gcd(math.gcd(rows, row0), math.gcd(*slot_rows)), 256)
    r0, s0, s1 = row0 // tm, slot_rows[0] // tm, slot_rows[1] // tm
    row = lambda w: pl.BlockSpec((tm, w), lambda i: (r0 + i, 0))
    once = lambda shape: pl.BlockSpec(shape, lambda i: (0, 0), pipeline_mode=pl.Buffered(1))
    return pl.pallas_call(
        _combine_ple_kernel,
        out_shape=jax.ShapeDtypeStruct((rows, d), F32),
        grid=(rows // tm,),
        in_specs=[row(d),
                  pl.BlockSpec((tm * (d // LANES), LANES), lambda i: (s0 + i, 0)),
                  pl.BlockSpec((tm * (d // LANES), LANES), lambda i: (s1 + i, 0)),
                  row(LANES), row(p.shape[1]),
                  once(w_ple.shape), once((1, d)), once(w_ple_gate.shape)],
        out_specs=pl.BlockSpec((tm, d), lambda i: (i, 0)),
        compiler_params=_cparams("parallel"),
        name="combine_ple_r%d" % row0,
    )(x1, y_slots, y_slots, route_w, p, w_ple, gain, w_ple_gate)


def _pad_rows(a, rows):
    return jnp.pad(a, ((0, 0), (0, rows - a.shape[1]), (0, 0)))


def _layer(x_all, p_all, nb_p, seq_p, nb_s, seq_s, past_len, state_conv, state_delta, cache_k, cache_v,
           page_table, lam_init, lw):
    tp = nb_p * seq_p
    ts = nb_s * seq_s
    d = x_all.shape[1]
    heads_d = lw['a_log'].shape[0]
    dk = lw['delta_norm'].shape[0]
    hk = heads_d * dk
    conv_dim = 3 * hk
    comp = lw['q_norm'].shape[0]
    dv = 2 * comp
    width = lw['w_ob'].shape[0]
    heads_a = width // dv
    n_groups = lw['w_rg'].shape[1]
    n_experts = lw['w_re'].shape[1]
    assert seq_s <= 8 and conv_dim % 1024 == 0 and hk == 1024 and width == 1024 and d % 1024 == 0
    assert 2 * heads_d <= LANES and n_groups + n_experts <= LANES and lw['conv_w'].shape[0] == 4

    w_in = lw['w_in']
    ba0 = conv_dim + hk
    q0 = ba0 + 2 * heads_d
    gate0 = q0 + 3 * width
    w_parts = (w_in[:, :ba0].astype(BF16), w_in[:, gate0:].astype(BF16), w_in[:, q0:gate0].astype(BF16))
    w_ba = jnp.pad(w_in[:, ba0:q0], ((0, 0), (0, LANES - 2 * heads_d))).astype(BF16)
    z_cb = conv_dim // hk
    gate_cb = (conv_dim + hk) // d
    q_cb = (conv_dim + hk + 2 * d) // width
    assert (conv_dim + hk) % d == 0 and (conv_dim + hk + 2 * d) % width == 0

    proj, ba = _inproj(x_all, lw['norm_attn'].reshape(1, d), w_parts, w_ba)

    alog_row = jnp.pad(lw['a_log'], (heads_d, LANES - 2 * heads_d)).reshape(1, LANES)
    dtb_row = jnp.pad(lw['dt_bias'], (heads_d, LANES - 2 * heads_d)).reshape(1, LANES)
    dn = lw['delta_norm'].reshape(1, dk)
    chunk_p = math.gcd(seq_p, DELTA_CHUNK)
    oa_p, s_p = _delta_branch(proj, ba, 0, nb_p, seq_p,
                              jnp.zeros((nb_p, 8, conv_dim), F32), jnp.zeros((nb_p, heads_d, dk, dk), F32),
                              lw['conv_w'], alog_row, dtb_row, dn,
                              chunk=chunk_p, valid=chunk_p, heads=heads_d, dk=dk, z_col_block=z_cb)
    qkv_s = proj[tp:, :conv_dim].reshape(nb_s, seq_s, conv_dim)
    proj_s8 = _pad_rows(proj[tp:].reshape(nb_s, seq_s, -1), 8).reshape(nb_s * 8, -1)
    ba_s8 = _pad_rows(ba[tp:].reshape(nb_s, seq_s, LANES), 8).reshape(nb_s * 8, LANES)
    tail_s = jnp.pad(state_conv, ((0, 0), (8 - state_conv.shape[1], 0), (0, 0)))
    oa_s8, s_s = _delta_branch(proj_s8, ba_s8, 0, nb_s, 8, tail_s, state_delta,
                               lw['conv_w'], alog_row, dtb_row, dn,
                               chunk=8, valid=seq_s, heads=heads_d, dk=dk, z_col_block=z_cb)
    oa_s = oa_s8.reshape(nb_s, 8, hk)[:, :seq_s].reshape(ts, hk)
    o_a = jnp.concatenate([oa_p, oa_s], axis=0)
    conv_p = jnp.stack([proj[(b + 1) * seq_p - 3:(b + 1) * seq_p, :conv_dim] for b in range(nb_p)])
    conv_s = jnp.concatenate([state_conv, qkv_s], axis=1)[:, -3:, :]

    half = comp // 2
    inv_freq = ROPE_THETA ** (-jnp.arange(half, dtype=F32) / half)
    def rope_tables(pos):
        ang = pos.astype(F32)[:, None] * inv_freq[None, :]
        sin_h = jnp.sin(ang)
        return (jnp.tile(jnp.cos(ang), (1, LANES // half)),
                jnp.tile(jnp.concatenate([-sin_h, sin_h], axis=1), (1, LANES // comp)))

    q_gain = jnp.tile(lw['q_norm'], LANES // comp).reshape(1, LANES)
    k_gain = jnp.tile(lw['k_norm'], LANES // comp).reshape(1, LANES)
    tk = _row_tile(seq_p, 512, align=LANES)
    q_t, k_bf, v_t, k_t = _qkprep(proj, 0, nb_p, seq_p, *rope_tables(jnp.arange(seq_p, dtype=jnp.int32)),
                                  q_gain, k_gain, q_col_block=q_cb, width=width, comp=comp, for_prompt=True, tk=tk)
    pos_s = jnp.tile(past_len + jnp.arange(seq_s, dtype=jnp.int32), nb_s)
    q_fs, k_fs = _qkprep(proj, tp, 1, ts, *rope_tables(pos_s), q_gain, k_gain,
                         q_col_block=q_cb, width=width, comp=comp, for_prompt=False)
    lam_vecs = jnp.stack([lw['lam_q1'], lw['lam_k1'], lw['lam_q2'], lw['lam_k2']])
    subln = lw['subln'].reshape(1, dv)
    ob_p = _attn_prompt(q_t, k_bf, v_t, lam_vecs, lw['subln'].reshape(dv, 1), nb=nb_p, seq=seq_p, tk=tk,
                        heads=heads_a, comp=comp, lam_init=lam_init)
    v_f_s = proj[tp:, (q_cb + 2) * width:(q_cb + 3) * width]
    q8 = _pad_rows(q_fs.reshape(nb_s, seq_s, width), 8)
    k8 = _pad_rows(k_fs.reshape(nb_s, seq_s, width), 8)
    v8 = _pad_rows(v_f_s.reshape(nb_s, seq_s, width), 8)
    n_phys, page = cache_k.shape[0], cache_k.shape[1]
    cache_kt = jnp.transpose(cache_k, (0, 2, 3, 4, 1)).reshape(n_phys, width, page)
    cache_v2 = cache_v.reshape(n_phys, page * heads_a, dv)
    ob_s8 = _attn_sample(q8, k8, v8, cache_kt, cache_v2, page_table, lam_vecs, subln,
                         new_len=seq_s, heads=heads_a, comp=comp, lam_init=lam_init)
    o_b = jnp.concatenate([ob_p, ob_s8[:, :seq_s].reshape(ts, width)], axis=0)

    merged = _merge(o_a, o_b, proj, lw['w_oa'].astype(BF16), lw['w_ob'].astype(BF16), gate_col_block=gate_cb)
    w_route = jnp.pad(jnp.concatenate([lw['w_rg'], lw['w_re']], axis=1),
                      ((0, 0), (0, LANES - n_groups - n_experts))).astype(BF16)
    b_route = jnp.pad(jnp.concatenate([lw['b_rg'], lw['b_re']]), (0, LANES - n_groups - n_experts)).reshape(1, LANES)
    x1, h2, route_w, route_i, route_cnt = _outproj_route(x_all, merged, lw['w_out'].astype(BF16),
                                                         lw['norm_ffn'].reshape(1, d), w_route, b_route,
                                                         n_groups=n_groups, per_group=n_experts // n_groups)

    counts = route_cnt[0, n_groups:n_groups + n_experts].astype(jnp.int32)
    block_e, src_tok, dst_slot, nused = _expert_plan(route_i[:, :TOP_K], route_i[:, TOP_K:2 * TOP_K], counts, tp)
    y_slots = _experts(h2, block_e, src_tok, dst_slot, nused, lw['exp_gate'], lw['exp_up'], lw['exp_down'])

    ple_args = (x1, y_slots, route_w, p_all.astype(BF16), lw['w_ple'].astype(BF16),
                lw['norm_ple'].reshape(1, d), lw['w_ple_gate'].astype(BF16))
    y_p = _combine_ple(*ple_args, row0=0, rows=tp, slot_rows=(0, tp))
    y_s = _combine_ple(*ple_args, row0=tp, rows=ts, slot_rows=(TOP_K * tp, TOP_K * tp + ts))
    y = (y_p, y_s)

    k_p = jnp.transpose(k_t.reshape(nb_p, heads_a, 2, comp, seq_p), (0, 4, 1, 2, 3))
    v_p = proj[:tp, (q_cb + 2) * width:(q_cb + 3) * width].reshape(nb_p, seq_p, heads_a, dv)
    k_s = k_fs.reshape(nb_s, seq_s, heads_a, 2, comp)
    v_s = v_f_s.reshape(nb_s, seq_s, heads_a, dv)
    return y, k_p, v_p, k_s, v_s, conv_p, conv_s, s_p, s_s


def kernel(x_prompt, x_sample, cache_k, cache_v, state_conv, state_delta, page_table, p_prompt, p_sample,
           norm_attn, w_in, conv_w, a_log, dt_bias, delta_norm, q_norm, k_norm, lam_q1, lam_k1, lam_q2, lam_k2,
           subln, w_oa, w_ob, w_out, norm_ffn, w_rg, b_rg, w_re, b_re, exp_gate, exp_up, exp_down,
           norm_ple, w_ple, w_ple_gate):
    nb_p, seq_p, d = x_prompt.shape
    nb_s, seq_s, _ = x_sample.shape
    tp = nb_p * seq_p
    depth = w_in.shape[0]
    past_len = page_table.shape[1] * cache_k.shape[2]
    y_p, y_s = x_prompt.reshape(tp, d), x_sample.reshape(nb_s * seq_s, d)
    outs = [[] for _ in range(8)]
    for i in range(depth):
        x_all = jnp.concatenate([y_p, y_s], axis=0)
        lw = dict(norm_attn=norm_attn[i], w_in=w_in[i], conv_w=conv_w[i], a_log=a_log[i], dt_bias=dt_bias[i],
                  delta_norm=delta_norm[i], q_norm=q_norm[i], k_norm=k_norm[i], lam_q1=lam_q1[i], lam_k1=lam_k1[i],
                  lam_q2=lam_q2[i], lam_k2=lam_k2[i], subln=subln[i], w_oa=w_oa[i], w_ob=w_ob[i], w_out=w_out[i],
                  norm_ffn=norm_ffn[i], w_rg=w_rg[i], b_rg=b_rg[i], w_re=w_re[i], b_re=b_re[i],
                  exp_gate=exp_gate[i], exp_up=exp_up[i], exp_down=exp_down[i],
                  norm_ple=norm_ple[i], w_ple=w_ple[i], w_ple_gate=w_ple_gate[i])
        lam_init = 0.8 - 0.6 * math.exp(-0.3 * i)
        p_all = jnp.concatenate([p_prompt[i].reshape(tp, -1), p_sample[i].reshape(nb_s * seq_s, -1)], axis=0)
        res = _layer(x_all, p_all, nb_p, seq_p, nb_s, seq_s, past_len, state_conv[i], state_delta[i],
                     cache_k[i], cache_v[i], page_table, lam_init, lw)
        y_p, y_s = res[0]
        for lst, val in zip(outs, res[1:]):
            lst.append(val)
    return (y_p.reshape(nb_p, seq_p, d), y_s.reshape(nb_s, seq_s, d)) + tuple(jnp.stack(lst) for lst in outs)
```

```python
import functools
import math

import jax
import jax.numpy as jnp
from jax import lax
from jax.experimental import pallas as pl
from jax.experimental.pallas import tpu as pltpu

F32 = jnp.float32
BF16 = jnp.bfloat16
HIGHEST = lax.Precision.HIGHEST

LANES = 128
DELTA_CHUNK = 64
ROPE_THETA = 10000.0
TOP_K = 2
MOE_BLOCK = 128
EPS = 1e-6
VMEM_LIMIT = 48 * 1024 * 1024


def _cparams(*sem):
    return pltpu.CompilerParams(dimension_semantics=sem, vmem_limit_bytes=VMEM_LIMIT)


def _row_tile(n, target, align=16):
    best = None
    for t in range(align, min(n, target) + 1, align):
        if n % t == 0:
            best = t
    assert best is not None, (n, target, align)
    return best


def _nt_dot(a, b, precision=None):
    return lax.dot_general(a, b, (((1,), (1,)), ((), ())), precision=precision,
                           preferred_element_type=F32)


def _tn_dot(a, b, precision=None):
    return lax.dot_general(a, b, (((0,), (0,)), ((), ())), precision=precision,
                           preferred_element_type=F32)


def _dot(a, b, precision=None):
    return jnp.dot(a, b, precision=precision, preferred_element_type=F32)


def _split_bf16(a):
    hi = a.astype(BF16)
    return hi, (a - hi.astype(F32)).astype(BF16)


def _dot_split(a, b):
    return _dot(a[0], b[0]) + (_dot(a[0], b[1]) + _dot(a[1], b[0]))


def _sigmoid(x):
    return 1.0 / (1.0 + jnp.exp(-x))


def _silu(x):
    return x * _sigmoid(x)


def _iota(shape, dim):
    return lax.broadcasted_iota(jnp.int32, shape, dim)


def _inproj_kernel(x_ref, g_ref, *rest, part_tiles):
    w_refs = rest[:len(part_tiles)]
    wba_ref, p_ref, ba_ref, h_scr = rest[len(part_tiles):]
    j = pl.program_id(1)

    @pl.when(j == 0)
    def _():
        x = x_ref[...]
        h = x * lax.rsqrt(jnp.mean(x * x, axis=-1, keepdims=True) + EPS) * g_ref[...]
        h_scr[...] = h.astype(BF16)
        ba_ref[...] = _dot(h_scr[...], wba_ref[...])

    lo = 0
    for w_ref, n_tiles in zip(w_refs, part_tiles):
        @pl.when((j >= lo) & (j < lo + n_tiles))
        def _(w_ref=w_ref):
            p_ref[...] = _dot(h_scr[...], w_ref[...])
        lo += n_tiles


def _inproj(x, gain, w_parts, w_ba):
    t, d = x.shape
    tm = _row_tile(t, 640)
    tn = 1024
    part_tiles = tuple(w.shape[1] // tn for w in w_parts)
    assert all(w.shape[1] % tn == 0 for w in w_parts)
    n_tiles = sum(part_tiles)
    starts = [sum(part_tiles[:k]) for k in range(len(w_parts))]

    def w_spec(k):
        return pl.BlockSpec((d, tn), lambda i, j: (0, jnp.clip(j - starts[k], 0, part_tiles[k] - 1)))

    return pl.pallas_call(
        functools.partial(_inproj_kernel, part_tiles=part_tiles),
        out_shape=(jax.ShapeDtypeStruct((t, n_tiles * tn), F32), jax.ShapeDtypeStruct((t, LANES), F32)),
        grid=(t // tm, n_tiles),
        in_specs=[pl.BlockSpec((tm, d), lambda i, j: (i, 0)),
                  pl.BlockSpec((1, d), lambda i, j: (0, 0))]
                 + [w_spec(k) for k in range(len(w_parts))]
                 + [pl.BlockSpec((d, LANES), lambda i, j: (0, 0))],
        out_specs=(pl.BlockSpec((tm, tn), lambda i, j: (i, j)),
                   pl.BlockSpec((tm, LANES), lambda i, j: (i, 0))),
        scratch_shapes=[pltpu.VMEM((tm, d), BF16)],
        compiler_params=_cparams("parallel", "arbitrary"),
        name="inproj",
    )(x, gain, *w_parts, w_ba)


def _delta_kernel(qkv_ref, z_ref, ba_ref, tail0_ref, s0_ref, cw_ref, alog_ref, dtb_ref, dn_ref,
                  o_ref, sfin_ref, ext_scr, s_scr, *, chunk, valid, heads, dk):
    n = pl.program_id(1)
    c = chunk
    hk = heads * dk

    @pl.when(n == 0)
    def _():
        ext_scr[0:8, :] = tail0_ref[0]
        s_scr[...] = s0_ref[0]

    ext_scr[8:8 + c, :] = qkv_ref[...]

    row = _iota((c, 1), 0)
    rowmask = (row < valid).astype(F32) if valid < c else None

    ba = ba_ref[...]
    beta_all = _sigmoid(ba)
    xg = ba + dtb_ref[...]
    softplus = jnp.maximum(xg, 0.0) + jnp.log1p(jnp.exp(-jnp.abs(xg)))
    g_all = -jnp.exp(alog_ref[...]) * softplus
    if rowmask is not None:
        beta_all = beta_all * rowmask
        g_all = g_all * rowmask
    r_i = _iota((c, c), 0)
    c_i = _iota((c, c), 1)
    incl = r_i >= c_i
    strict = r_i > c_i
    gc_all = _dot(incl.astype(F32), g_all, precision=HIGHEST)
    gc_t = jnp.transpose(gc_all)
    eye = (r_i == c_i).astype(F32)

    def conv(col0):
        sl = slice(col0, col0 + dk)
        acc = ext_scr[8:8 + c, sl] * cw_ref[3:4, sl]
        acc = acc + ext_scr[7:7 + c, sl] * cw_ref[2:3, sl]
        acc = acc + ext_scr[6:6 + c, sl] * cw_ref[1:2, sl]
        acc = acc + ext_scr[5:5 + c, sl] * cw_ref[0:1, sl]
        return _silu(acc)

    hs = range(heads)
    q, k, v, beta, gc, egc, g_last, decay = [], [], [], [], [], [], [], []
    for h in hs:
        qh = conv(h * dk)
        kh = conv(hk + h * dk)
        vh = conv(2 * hk + h * dk)
        qh = qh * lax.rsqrt(jnp.sum(qh * qh, axis=-1, keepdims=True) + EPS) * (dk ** -0.5)
        kh = kh * lax.rsqrt(jnp.sum(kh * kh, axis=-1, keepdims=True) + EPS)
        if rowmask is not None:
            qh, kh, vh = qh * rowmask, kh * rowmask, vh * rowmask
        q.append(qh)
        k.append(kh)
        v.append(vh)
        beta.append(beta_all[:, h:h + 1])
        gc.append(gc_all[:, heads + h:heads + h + 1])
        egc.append(jnp.exp(gc[h]))
        g_last.append(gc_all[c - 1:c, heads + h:heads + h + 1])
        gc_row = gc_t[heads + h:heads + h + 1, :]
        decay.append(jnp.where(incl, jnp.exp(jnp.where(incl, gc[h] - gc_row, 0.0)), 0.0))
    kb = [k[h] * beta[h] for h in hs]
    kk = [_nt_dot(kb[h], k[h]) for h in hs]
    qk = [_nt_dot(q[h], k[h]) for h in hs]
    apow = [jnp.where(strict, kk[h] * decay[h], 0.0) for h in hs]
    qk = [qk[h] * decay[h] for h in hs]
    a_mat = apow
    same_block = lambda b: (r_i // b) == (c_i // b)
    b0 = min(8, c)
    diag0 = same_block(b0)
    apb = [jnp.where(diag0, a_mat[h], 0.0) for h in hs]
    tinv = [eye - apb[h] for h in hs]
    apb = [apb[h].astype(BF16) for h in hs]
    span = 2
    while span < b0:
        apb = [_dot(apb[h], apb[h]).astype(BF16) for h in hs]
        tinv = [tinv[h] + _dot(tinv[h].astype(BF16), apb[h]) for h in hs]
        span *= 2
    b = b0
    while b < c:
        ring = same_block(2 * b) & ((r_i // b) != (c_i // b))
        off = [jnp.where(ring, a_mat[h], 0.0).astype(BF16) for h in hs]
        tb = [tinv[h].astype(BF16) for h in hs]
        tl = [_dot(tb[h], off[h]).astype(BF16) for h in hs]
        tinv = [tinv[h] - _dot(tl[h], tb[h]) for h in hs]
        b *= 2
    resid = [eye - _dot_split(_split_bf16(eye + a_mat[h]), _split_bf16(tinv[h])) for h in hs]
    tinv = [tinv[h] + _dot(tinv[h].astype(BF16), resid[h].astype(BF16)) for h in hs]
    u = [_dot(tinv[h], v[h] * beta[h]) for h in hs]
    w = [_dot(tinv[h], kb[h] * egc[h]) for h in hs]
    s = [s_scr[h] for h in hs]
    ws = [_dot(w[h], s[h]) for h in hs]
    qs = [_dot(q[h] * egc[h], s[h]) for h in hs]
    v_new = [u[h] - ws[h] for h in hs]
    o = [qs[h] + _dot(qk[h], v_new[h]) for h in hs]
    ds = [_tn_dot(k[h] * jnp.exp(g_last[h] - gc[h]), v_new[h]) for h in hs]
    for h in hs:
        s_scr[h] = s[h] * jnp.exp(g_last[h]) + ds[h]
        on = o[h] * lax.rsqrt(jnp.mean(o[h] * o[h], axis=-1, keepdims=True) + EPS) * dn_ref[...]
        zh = z_ref[:, h * dk:(h + 1) * dk]
        o_ref[:, h * dk:(h + 1) * dk] = (on * _silu(zh)).astype(o_ref.dtype)

    ext_scr[0:8, :] = ext_scr[c:c + 8, :]

    @pl.when(n == pl.num_programs(1) - 1)
    def _():
        sfin_ref[0] = s_scr[...]


def _delta_branch(proj, ba, row0, nb, seq, tail0, s0, conv_w, alog_row, dtb_row, delta_norm, *,
                  chunk, valid, heads, dk, z_col_block):
    hk = heads * dk
    nchunks = seq // chunk
    assert seq % chunk == 0 and row0 % chunk == 0
    r0 = row0 // chunk
    kern = functools.partial(_delta_kernel, chunk=chunk, valid=valid, heads=heads, dk=dk)
    return pl.pallas_call(
        kern,
        out_shape=(jax.ShapeDtypeStruct((nb * seq, hk), BF16),
                   jax.ShapeDtypeStruct((nb, heads, dk, dk), F32)),
        grid=(nb, nchunks),
        in_specs=[pl.BlockSpec((chunk, 3 * hk), lambda b, n: (r0 + b * nchunks + n, 0)),
                  pl.BlockSpec((chunk, hk), lambda b, n: (r0 + b * nchunks + n, z_col_block)),
                  pl.BlockSpec((chunk, LANES), lambda b, n: (r0 + b * nchunks + n, 0)),
                  pl.BlockSpec((1, 8, 3 * hk), lambda b, n: (b, 0, 0)),
                  pl.BlockSpec((1, heads, dk, dk), lambda b, n: (b, 0, 0, 0)),
                  pl.BlockSpec((4, 3 * hk), lambda b, n: (0, 0)),
                  pl.BlockSpec((1, LANES), lambda b, n: (0, 0)),
                  pl.BlockSpec((1, LANES), lambda b, n: (0, 0)),
                  pl.BlockSpec((1, dk), lambda b, n: (0, 0))],
        out_specs=(pl.BlockSpec((chunk, hk), lambda b, n: (b * nchunks + n, 0)),
                   pl.BlockSpec((1, heads, dk, dk), lambda b, n: (b, 0, 0, 0))),
        scratch_shapes=[pltpu.VMEM((8 + chunk, 3 * hk), F32),
                        pltpu.VMEM((heads, dk, dk), F32)],
        compiler_params=_cparams("parallel", "arbitrary"),
        name="delta_c%d" % chunk,
    )(proj, proj, ba, tail0, s0, conv_w, alog_row, dtb_row, delta_norm)


def _qkprep_kernel(q_ref, k_ref, v_ref, cos_ref, sin_ref, qg_ref, kg_ref, *out_refs, comp, scale, for_prompt, tk):
    tm = q_ref.shape[0]
    r_i = _iota((LANES, LANES), 0)
    c_i = _iota((LANES, LANES), 1)
    group = (r_i // comp == c_i // comp).astype(BF16)
    lane = _iota((tm, LANES), 1)
    first_half = (lane % comp) < (comp // 2)
    cos = cos_ref[...]
    sin = sin_ref[...]

    def norm_rope(x, gain):
        sq = x * x
        hi = sq.astype(BF16)
        lo = (sq - hi.astype(F32)).astype(BF16)
        ms = (_dot(hi, group) + _dot(lo, group)) * (1.0 / comp)
        y = x * lax.rsqrt(ms + EPS) * gain
        swapped = jnp.where(first_half, pltpu.roll(y, LANES - comp // 2, 1), pltpu.roll(y, comp // 2, 1))
        return y * cos + swapped * sin

    for j in range(q_ref.shape[1] // LANES):
        sl = slice(j * LANES, (j + 1) * LANES)
        qr = norm_rope(q_ref[:, sl], qg_ref[...]) * scale
        kr = norm_rope(k_ref[:, sl], kg_ref[...])
        if for_prompt:
            qt_ref, kbo_ref, vt_ref, kt_ref = out_refs
            qt_ref[0, sl, :] = jnp.transpose(qr).astype(BF16)
            kbo_ref[:, sl] = kr.astype(BF16)
            kt_ref[0, sl, :] = jnp.transpose(kr)
            vt = jnp.transpose(v_ref[:, sl]).astype(BF16)
            for u in range(tm // tk):
                vt_ref[0, u, sl, :] = vt[:, u * tk:(u + 1) * tk]
        else:
            qo_ref, kfo_ref = out_refs
            qo_ref[:, sl] = qr
            kfo_ref[:, sl] = kr


def _qkprep(proj, row0, nb, seq, cos_t, sin_t, q_gain, k_gain, *, q_col_block, width, comp, for_prompt, tk=None):
    tm = _row_tile(seq, 512, align=LANES) if for_prompt else seq
    assert row0 % tm == 0 and seq % tm == 0 and (not for_prompt or tm % tk == 0)
    r0, per_seq = row0 // tm, seq // tm
    scale = comp ** -0.5 * (math.log2(math.e) if for_prompt else 1.0)
    kern = functools.partial(_qkprep_kernel, comp=comp, scale=scale, for_prompt=for_prompt, tk=tk)
    col = lambda cb: pl.BlockSpec((tm, width), lambda i: (r0 + i, cb))
    small = pl.BlockSpec((tm, LANES), lambda i: (i % per_seq, 0))
    gain = pl.BlockSpec((1, LANES), lambda i: (0, 0))
    out = pl.BlockSpec((tm, width), lambda i: (i, 0))
    rows = nb * seq
    if for_prompt:
        out_t = pl.BlockSpec((1, width, tm), lambda i: (i // per_seq, 0, i % per_seq))
        out_shape = (jax.ShapeDtypeStruct((nb, width, seq), BF16), jax.ShapeDtypeStruct((rows, width), BF16),
                     jax.ShapeDtypeStruct((nb, seq // tk, width, tk), BF16),
                     jax.ShapeDtypeStruct((nb, width, seq), F32))
        out_specs = (out_t, out,
                     pl.BlockSpec((1, tm // tk, width, tk), lambda i: (i // per_seq, i % per_seq, 0, 0)), out_t)
    else:
        out_shape = (jax.ShapeDtypeStruct((rows, width), F32), jax.ShapeDtypeStruct((rows, width), F32))
        out_specs = (out, out)
    return pl.pallas_call(
        kern,
        out_shape=out_shape,
        grid=(rows // tm,),
        in_specs=[col(q_col_block), col(q_col_block + 1), col(q_col_block + 2), small, small, gain, gain],
        out_specs=out_specs,
        compiler_params=_cparams("parallel"),
        name="qkprep_prompt" if for_prompt else "qkprep_sample",
    )(proj, proj, proj, cos_t, sin_t, q_gain, k_gain)


def _lambda_value(lam_ref, lam_init):
    lv = lam_ref[...]
    s1 = jnp.sum(lv[0:1] * lv[1:2], axis=-1, keepdims=True)
    s2 = jnp.sum(lv[2:3] * lv[3:4], axis=-1, keepdims=True)
    return jnp.exp(s1) - jnp.exp(s2) + lam_init


def _attn_prompt_kernel(q_ref, k_ref, v_ref, lam_ref, sub_ref, o_ref, *, tq, tk, comp, heads_per_step, lam_init):
    i = pl.program_id(2)
    dv = 2 * comp
    hr = range(heads_per_step)
    row = _iota((dv, tq), 0)
    qs = []
    for hh in hr:
        qt = q_ref[0, hh * dv:(hh + 1) * dv, :]
        zero = jnp.zeros_like(qt)
        qs.append(jnp.concatenate([jnp.where(row < comp, qt, zero), jnp.where(row >= comp, qt, zero)], axis=1))

    def update(carry, j, mask):
        start = pl.multiple_of(j * tk, tk)
        kt = [k_ref[pl.ds(start, tk), hh * dv:(hh + 1) * dv] for hh in hr]
        vt = [v_ref[0, j, hh * dv:(hh + 1) * dv, :] for hh in hr]
        s = [_dot(kt[hh], qs[hh]) for hh in hr]
        if mask is not None:
            s = [jnp.where(mask, sn, -jnp.inf) for sn in s]
        m_new = [jnp.maximum(carry[hh][0], jnp.max(s[hh], axis=0, keepdims=True)) for hh in hr]
        p = [jnp.exp2(s[hh] - m_new[hh]) for hh in hr]
        corr = [jnp.exp2(carry[hh][0] - m_new[hh]) for hh in hr]
        l = [carry[hh][1] * corr[hh] + jnp.sum(p[hh], axis=0, keepdims=True) for hh in hr]
        pv = [_dot(vt[hh], p[hh].astype(BF16)) for hh in hr]
        return tuple((m_new[hh], l[hh], carry[hh][2] * corr[hh] + pv[hh]) for hh in hr)

    def body(j, carry):
        return update(carry, j, None)

    init = tuple((jnp.full((1, 2 * tq), -jnp.inf, F32), jnp.zeros((1, 2 * tq), F32),
                  jnp.zeros((dv, 2 * tq), F32)) for _ in hr)
    n_full = (i * tq) // tk
    carry = lax.fori_loop(0, n_full, body, init)
    kpos = n_full * tk + _iota((tk, 2 * tq), 0)
    qpos = i * tq + _iota((tk, 2 * tq), 1) % tq
    carry = update(carry, n_full, qpos >= kpos)
    lam = _lambda_value(lam_ref, lam_init)
    for hh in hr:
        _, l, a = carry[hh]
        n = a / l
        o = n[:, 0:tq] - lam * n[:, tq:2 * tq]
        on = o * lax.rsqrt(jnp.mean(o * o, axis=0, keepdims=True) + EPS) * sub_ref[...]
        o_ref[:, hh * dv:(hh + 1) * dv] = jnp.transpose(on * (1.0 - lam_init)).astype(o_ref.dtype)


def _attn_prompt(q_t, k, v_t, lam_vecs, subln_col, *, nb, seq, tk, heads, comp, lam_init):
    dv = 2 * comp
    tq = _row_tile(seq, 256, align=LANES)
    assert tk % tq == 0 and seq % tk == 0
    nq = seq // tq
    hps = 2
    assert heads % hps == 0
    kern = functools.partial(_attn_prompt_kernel, tq=tq, tk=tk, comp=comp, heads_per_step=hps, lam_init=lam_init)
    return pl.pallas_call(
        kern,
        out_shape=jax.ShapeDtypeStruct((nb * seq, heads * dv), BF16),
        grid=(nb, heads // hps, nq),
        in_specs=[pl.BlockSpec((1, hps * dv, tq), lambda b, h, i: (b, h, i)),
                  pl.BlockSpec((seq, hps * dv), lambda b, h, i: (b, h)),
                  pl.BlockSpec((1, seq // tk, hps * dv, tk), lambda b, h, i: (b, 0, h, 0)),
                  pl.BlockSpec((4, comp), lambda b, h, i: (0, 0)),
                  pl.BlockSpec((dv, 1), lambda b, h, i: (0, 0))],
        out_specs=pl.BlockSpec((tq, hps * dv), lambda b, h, i: (b * nq + i, h)),
        compiler_params=_cparams("parallel", "parallel", "arbitrary"),
        name="attn_prompt",
    )(q_t, k, v_t, lam_vecs, subln_col)


def _attn_sample_kernel(pt_ref, q_ref, kn_ref, vn_ref, lam_ref, sub_ref, *rest,
                        pages_per_step, new_len, heads, comp, lam_init):
    pp = pages_per_step
    kt_refs = rest[:pp]
    v_refs = rest[pp:2 * pp]
    o_ref = rest[2 * pp]
    qbd_scr, m_scr, l_scr, acc_scr = rest[2 * pp + 1:]
    p_id = pl.program_id(1)
    rph = 2 * new_len
    rows = heads * rph
    dv = 2 * comp
    width = heads * dv
    page = kt_refs[0].shape[2]

    @pl.when(p_id == 0)
    def _():
        q = q_ref[0][0:new_len, :]
        lane_grp = _iota((new_len, width), 1) // comp
        for h in range(heads):
            for c in range(2):
                r0 = h * rph + c * new_len
                qbd_scr[r0:r0 + new_len, :] = jnp.where(lane_grp == 2 * h + c, q, 0.0)
        m_scr[...] = jnp.full(m_scr.shape, -jnp.inf, F32)
        l_scr[...] = jnp.zeros(l_scr.shape, F32)
        acc_scr[...] = jnp.zeros(acc_scr.shape, F32)

    qbd = qbd_scr[...]

    def online(s_list, v_of):
        m = m_scr[...]
        m_new = m
        for s in s_list:
            m_new = jnp.maximum(m_new, jnp.max(s, axis=-1, keepdims=True))
        corr = jnp.exp(m - m_new)
        p_list = [jnp.exp(s - m_new) for s in s_list]
        l = l_scr[...] * corr
        for p in p_list:
            l = l + jnp.sum(p, axis=-1, keepdims=True)
        m_scr[...] = m_new
        l_scr[...] = l
        for h in range(heads):
            hs = slice(h * rph, (h + 1) * rph)
            acc = acc_scr[hs, :] * corr[hs, :]
            for j, p in enumerate(p_list):
                acc = acc + _dot(p[hs, :], v_of(j, h))
            acc_scr[hs, :] = acc

    online([_dot(qbd, kt_refs[r][0]) for r in range(pp)],
           lambda j, h: v_refs[j][0, pl.ds(h, page, stride=heads), :])

    @pl.when(p_id == pl.num_programs(1) - 1)
    def _():
        s = _nt_dot(qbd, kn_ref[0])
        tok = _iota((rows, 8), 0) % new_len
        key = _iota((rows, 8), 1)
        s = jnp.where((tok >= key) & (key < new_len), s, -jnp.inf)
        online([s], lambda j, h: vn_ref[0][:, h * dv:(h + 1) * dv])
        lam = _lambda_value(lam_ref, lam_init)
        nrm = acc_scr[...] / l_scr[...]
        for h in range(heads):
            r0 = h * rph
            oh = nrm[r0:r0 + new_len, :] - lam * nrm[r0 + new_len:r0 + rph, :]
            on = oh * lax.rsqrt(jnp.mean(oh * oh, axis=-1, keepdims=True) + EPS) * sub_ref[...]
            o_ref[0, 0:new_len, h * dv:(h + 1) * dv] = (on * (1.0 - lam_init)).astype(o_ref.dtype)
        if new_len < 8:
            o_ref[0, new_len:8, :] = jnp.zeros((8 - new_len, width), o_ref.dtype)


def _attn_sample(q8, k8, v8, cache_kt, cache_v2, page_table, lam_vecs, subln, *, new_len, heads, comp, lam_init):
    nb, n_pages = page_table.shape
    width, page = cache_kt.shape[1], cache_kt.shape[2]
    dv = 2 * comp
    pp = math.gcd(n_pages, 16)
    assert 2 * new_len == 8
    rows = heads * 2 * new_len
    kern = functools.partial(_attn_sample_kernel, pages_per_step=pp, new_len=new_len, heads=heads,
                             comp=comp, lam_init=lam_init)
    per_req = pl.BlockSpec((1, 8, width), lambda b, p, pt: (b, 0, 0))

    def page_spec(r, shape):
        return pl.BlockSpec((1,) + shape, lambda b, p, pt: (pt[b * n_pages + p * pp + r], 0, 0))

    grid_spec = pltpu.PrefetchScalarGridSpec(
        num_scalar_prefetch=1,
        grid=(nb, n_pages // pp),
        in_specs=[per_req, per_req, per_req,
                  pl.BlockSpec((4, comp), lambda b, p, pt: (0, 0)),
                  pl.BlockSpec((1, dv), lambda b, p, pt: (0, 0))]
                 + [page_spec(r, (width, page)) for r in range(pp)]
                 + [page_spec(r, (page * heads, dv)) for r in range(pp)],
        out_specs=pl.BlockSpec((1, 8, width), lambda b, p, pt: (b, 0, 0)),
        scratch_shapes=[pltpu.VMEM((rows, width), F32), pltpu.VMEM((rows, 1), F32),
                        pltpu.VMEM((rows, 1), F32), pltpu.VMEM((rows, dv), F32)],
    )
    return pl.pallas_call(
        kern,
        out_shape=jax.ShapeDtypeStruct((nb, 8, width), BF16),
        grid_spec=grid_spec,
        compiler_params=_cparams("parallel", "arbitrary"),
        name="attn_sample",
    )(page_table.reshape(-1), q8, k8, v8, lam_vecs, subln, *([cache_kt] * pp), *([cache_v2] * pp))


def _merge_kernel(oa_ref, ob_ref, ga_ref, gb_ref, woa_ref, wob_ref, m_ref):
    ua = _dot(oa_ref[...], woa_ref[...])
    ub = _dot(ob_ref[...], wob_ref[...])
    m_ref[...] = (_sigmoid(ga_ref[...]) * ua + _sigmoid(gb_ref[...]) * ub).astype(m_ref.dtype)


def _merge(o_a, o_b, proj, w_oa, w_ob, *, gate_col_block):
    t, zd = o_a.shape
    d = w_oa.shape[1]
    tm = _row_tile(t, 640)
    once = lambda shape: pl.BlockSpec(shape, lambda i: (0, 0), pipeline_mode=pl.Buffered(1))
    return pl.pallas_call(
        _merge_kernel,
        out_shape=jax.ShapeDtypeStruct((t, d), BF16),
        grid=(t // tm,),
        in_specs=[pl.BlockSpec((tm, zd), lambda i: (i, 0)),
                  pl.BlockSpec((tm, o_b.shape[1]), lambda i: (i, 0)),
                  pl.BlockSpec((tm, d), lambda i: (i, gate_col_block)),
                  pl.BlockSpec((tm, d), lambda i: (i, gate_col_block + 1)),
                  once(w_oa.shape), once(w_ob.shape)],
        out_specs=pl.BlockSpec((tm, d), lambda i: (i, 0)),
        compiler_params=_cparams("parallel"),
        name="merge",
    )(o_a, o_b, proj, proj, w_oa, w_ob)


def _outproj_route_kernel(x_ref, m_ref, wo_ref, g_ref, wr_ref, br_ref,
                          x1_ref, h2_ref, rw_ref, re_ref, cnt_ref, *, n_groups, per_group, parts):
    @pl.when(pl.program_id(0) == 0)
    def _():
        cnt_ref[...] = jnp.zeros(cnt_ref.shape, F32)

    tm = x_ref.shape[0] // parts
    nc = x_ref.shape[1] // LANES
    rows = [slice(k * tm, (k + 1) * tm) for k in range(parts)]
    x1s = [x_ref[sl, :] + _dot(m_ref[sl, :], wo_ref[...]) for sl in rows]
    lane = _iota((tm, LANES), 1).astype(F32)
    big = float(LANES)
    before = (_iota((tm, tm), 0) > _iota((tm, tm), 1)).astype(BF16)

    for k, sl in enumerate(rows):
        x1 = x1s[k]
        x1_ref[sl, :] = x1
        h2 = x1 * lax.rsqrt(jnp.mean(x1 * x1, axis=-1, keepdims=True) + EPS) * g_ref[...]
        for c in range(nc):
            h2_ref[pl.ds(k * tm * nc + c, tm, stride=nc), :] = h2[:, c * LANES:(c + 1) * LANES]
        logits = _dot(h2.astype(BF16), wr_ref[...]) + br_ref[...]

        def masked_softmax(mask):
            lm = jnp.where(mask, logits, -jnp.inf)
            e = jnp.exp(lm - jnp.max(lm, axis=-1, keepdims=True))
            return e / jnp.sum(e, axis=-1, keepdims=True)

        def top1(p, mask):
            v = jnp.max(jnp.where(mask, p, -1.0), axis=-1, keepdims=True)
            idx = jnp.min(jnp.where(mask & (p == v), lane, big), axis=-1, keepdims=True)
            return v, idx

        gmask = lane < n_groups
        pg = masked_softmax(gmask)
        g_w, g_top = top1(pg, gmask)
        lo = n_groups + g_top * per_group
        emask = (lane >= lo) & (lane < lo + per_group)
        pe = masked_softmax(emask)
        v1, i1 = top1(pe, emask)
        emask2 = emask & (lane != i1)
        v2, i2 = top1(pe, emask2)
        denom = v1 + v2
        w1 = g_w * v1 / denom
        w2 = g_w * v2 / denom
        rw_ref[sl, :] = jnp.where(lane == 0.0, w1, jnp.where(lane == 1.0, w2, 0.0))

        sel1 = lane == i1
        sel2 = lane == i2
        picked = jnp.where(sel1 | sel2, 1.0, 0.0)
        prior = _dot(before, picked.astype(BF16)) + cnt_ref[...]
        r1 = jnp.sum(jnp.where(sel1, prior, 0.0), axis=-1, keepdims=True)
        r2 = jnp.sum(jnp.where(sel2, prior, 0.0), axis=-1, keepdims=True)
        cnt_ref[...] = cnt_ref[...] + jnp.sum(picked, axis=0, keepdims=True)
        re_ref[sl, :] = jnp.where(lane == 0.0, i1 - n_groups,
                                  jnp.where(lane == 1.0, i2 - n_groups,
                                            jnp.where(lane == 2.0, r1,
                                                      jnp.where(lane == 3.0, r2, 0.0)))).astype(jnp.int32)


def _outproj_route(x, merged, w_out, gain, w_route, b_route, *, n_groups, per_group):
    t, d = x.shape
    parts = 2
    tm = _row_tile(t, 416, align=16 * parts)
    kern = functools.partial(_outproj_route_kernel, n_groups=n_groups, per_group=per_group, parts=parts)
    row = lambda w: pl.BlockSpec((tm, w), lambda i: (i, 0))
    once = lambda shape: pl.BlockSpec(shape, lambda i: (0, 0), pipeline_mode=pl.Buffered(1))
    return pl.pallas_call(
        kern,
        out_shape=(jax.ShapeDtypeStruct((t, d), F32), jax.ShapeDtypeStruct((t * (d // LANES), LANES), F32),
                   jax.ShapeDtypeStruct((t, LANES), F32), jax.ShapeDtypeStruct((t, LANES), jnp.int32),
                   jax.ShapeDtypeStruct((1, LANES), F32)),
        grid=(t // tm,),
        in_specs=[row(d), row(d), once(w_out.shape), once((1, d)), once((d, LANES)), once((1, LANES))],
        out_specs=(row(d), pl.BlockSpec((tm * (d // LANES), LANES), lambda i: (i, 0)), row(LANES), row(LANES),
                   pl.BlockSpec((1, LANES), lambda i: (0, 0))),
        compiler_params=_cparams("arbitrary"),
        name="outproj_route",
    )(x, merged, w_out, gain, w_route, b_route)


def _experts_kernel(be_ref, nused_ref, first_ref, wslot_ref, nexte_ref, src_ref, src_next_ref, dst_ref,
                    h_hbm, wg_hbm, wu_hbm, wd_hbm, y_hbm, xbuf, ybuf, wgbuf, wubuf, wdbuf, gsem, ssem, wsem):
    g = pl.program_id(0)
    nused = nused_ref[0]
    blk = MOE_BLOCK
    nc = wg_hbm.shape[1] // LANES

    def weight_copies(e, ws):
        return [pltpu.make_async_copy(hbm.at[e], buf.at[ws], wsem.at[ws])
                for hbm, buf in ((wg_hbm, wgbuf), (wu_hbm, wubuf), (wd_hbm, wdbuf))]

    def start_gather(idx_ref, slot):
        def body(r2, carry):
            for u in range(2):
                r = 2 * r2 + u
                row = pl.multiple_of(idx_ref[0, 0, r], nc)
                pltpu.make_async_copy(h_hbm.at[pl.ds(row, nc), :],
                                      xbuf.at[slot, pl.ds(pl.multiple_of(r * nc, nc), nc), :],
                                      gsem.at[slot]).start(priority=u)
            return carry
        lax.fori_loop(0, blk // 2, body, 0, unroll=4)

    def start_scatter(slot):
        def body(r2, carry):
            for u in range(2):
                r = 2 * r2 + u
                row = pl.multiple_of(dst_ref[0, 0, r], nc)
                pltpu.make_async_copy(ybuf.at[slot, pl.ds(pl.multiple_of(r * nc, nc), nc), :],
                                      y_hbm.at[pl.ds(row, nc), :], ssem.at[slot]).start(priority=u)
            return carry
        lax.fori_loop(0, blk // 2, body, 0, unroll=4)

    def wait_gather(slot):
        pltpu.make_async_copy(h_hbm.at[pl.ds(0, blk * nc), :], xbuf.at[slot], gsem.at[slot]).wait()

    def wait_scatter(slot):
        pltpu.make_async_copy(ybuf.at[slot], y_hbm.at[pl.ds(0, blk * nc), :], ssem.at[slot]).wait()

    @pl.when(g < nused)
    def _():
        slot = g % 2

        @pl.when(g == 0)
        def _():
            ybuf[1] = jnp.zeros(ybuf.shape[1:], F32)
            fill = pltpu.make_async_copy(ybuf.at[1], y_hbm.at[pl.ds(y_hbm.shape[0] - blk * nc, blk * nc), :],
                                         ssem.at[1])
            fill.start()
            fill.wait()
            start_gather(src_ref, slot)
            for cp in weight_copies(be_ref[0], wslot_ref[0]):
                cp.start()

        @pl.when(g + 1 < nused)
        def _():
            start_gather(src_next_ref, 1 - slot)

        ws = wslot_ref[g]

        @pl.when(first_ref[g] == 1)
        def _():
            @pl.when(nexte_ref[g] >= 0)
            def _():
                for cp in weight_copies(nexte_ref[g], 1 - ws):
                    cp.start()
            for cp in weight_copies(be_ref[g], ws):
                cp.wait()

        wait_gather(slot)
        x = jnp.concatenate([xbuf[slot, pl.ds(c, blk, stride=nc), :] for c in range(nc)], axis=1)
        hdn = _silu(_dot(x, wgbuf[ws])) * _dot(x, wubuf[ws])
        y = _dot(hdn, wdbuf[ws])
        for c in range(nc):
            ybuf[slot, pl.ds(c, blk, stride=nc), :] = y[:, c * LANES:(c + 1) * LANES]

        @pl.when(g > 0)
        def _():
            wait_scatter(1 - slot)

        start_scatter(slot)

        @pl.when(g == nused - 1)
        def _():
            wait_scatter(slot)


def _experts(h2, block_e, src_tok, dst_slot, nused, w_gate, w_up, w_down):
    d, ff = w_gate.shape[1], w_gate.shape[2]
    nc = d // LANES
    t = h2.shape[0] // nc
    n_blocks = block_e.shape[0]
    n_slots = t * TOP_K
    kern = _experts_kernel
    src3 = (src_tok * nc).reshape(n_blocks, 1, MOE_BLOCK)
    dst3 = (dst_slot * nc).reshape(n_blocks, 1, MOE_BLOCK)
    blk_id = jnp.arange(n_blocks, dtype=jnp.int32)
    used = blk_id < nused[0]
    first = used & ((blk_id == 0) | (block_e != jnp.roll(block_e, 1)))
    run_id = jnp.cumsum(first.astype(jnp.int32)) - 1
    wslot = (run_id % 2).astype(jnp.int32)
    run_expert = jnp.full((n_blocks + 1,), -1, jnp.int32).at[jnp.where(first, run_id, n_blocks)].set(block_e)
    next_e = run_expert[jnp.minimum(run_id + 1, n_blocks)]
    idx_spec = lambda f: pl.BlockSpec((1, 1, MOE_BLOCK), lambda g, *_: (f(g), 0, 0), memory_space=pltpu.SMEM)
    anyspec = pl.BlockSpec(memory_space=pl.ANY)
    grid_spec = pltpu.PrefetchScalarGridSpec(
        num_scalar_prefetch=5,
        grid=(n_blocks,),
        in_specs=[idx_spec(lambda g: g),
                  idx_spec(lambda g: jnp.minimum(g + 1, n_blocks - 1)),
                  idx_spec(lambda g: g),
                  anyspec, anyspec, anyspec, anyspec],
        out_specs=anyspec,
        scratch_shapes=[pltpu.VMEM((2, MOE_BLOCK * nc, LANES), F32), pltpu.VMEM((2, MOE_BLOCK * nc, LANES), F32),
                        pltpu.VMEM((2, d, ff), F32), pltpu.VMEM((2, d, ff), F32), pltpu.VMEM((2, ff, d), F32),
                        pltpu.SemaphoreType.DMA((2,)), pltpu.SemaphoreType.DMA((2,)), pltpu.SemaphoreType.DMA((2,))],
    )
    return pl.pallas_call(
        kern,
        out_shape=jax.ShapeDtypeStruct(((n_slots + MOE_BLOCK) * nc, LANES), F32),
        grid_spec=grid_spec,
        compiler_params=_cparams("arbitrary"),
        name="experts",
    )(block_e, nused, first.astype(jnp.int32), wslot, next_e, src3, src3, dst3, h2, w_gate, w_up, w_down)


def _expert_plan(eidx, rank, counts, tp):
    t = eidx.shape[0]
    n_experts = counts.shape[0]
    s = t * TOP_K
    padded = (counts + MOE_BLOCK - 1) // MOE_BLOCK * MOE_BLOCK
    ends = jnp.cumsum(padded)
    pad_start = ends - padded
    dest = (pad_start[eidx] + rank).reshape(-1)
    n_blocks = -(-(s + n_experts * (MOE_BLOCK - 1)) // MOE_BLOCK)
    rows = n_blocks * MOE_BLOCK
    slot_of_row = jnp.full((rows,), -1, jnp.int32).at[dest].set(jnp.arange(s, dtype=jnp.int32))
    valid = slot_of_row >= 0
    tok = slot_of_row // TOP_K
    src_tok = jnp.where(valid, tok, 0)
    kk = slot_of_row % TOP_K
    dst_real = jnp.where(tok < tp, kk * tp + tok, TOP_K * tp + kk * (t - tp) + (tok - tp))
    dst_slot = jnp.where(valid, dst_real, s + jnp.arange(rows, dtype=jnp.int32) % MOE_BLOCK)
    block_start = jnp.arange(n_blocks, dtype=jnp.int32) * MOE_BLOCK
    block_e = jnp.minimum(jnp.sum((ends[None, :] <= block_start[:, None]).astype(jnp.int32), axis=1), n_experts - 1)
    nused = (ends[-1:] // MOE_BLOCK).astype(jnp.int32)
    return block_e, src_tok, dst_slot, nused


def _combine_ple_kernel(x1_ref, ys0_ref, ys1_ref, rw_ref, p_ref, wp_ref, g_ref, wpg_ref, y_ref):
    rw = rw_ref[...]
    tm, d = x1_ref.shape
    nc = d // LANES

    def token_rows(ref):
        return jnp.concatenate([ref[pl.ds(c, tm, stride=nc), :] for c in range(nc)], axis=1)

    moe = rw[:, 0:1] * token_rows(ys0_ref) + rw[:, 1:2] * token_rows(ys1_ref)
    x2 = x1_ref[...] + moe
    hn = x2 * lax.rsqrt(jnp.mean(x2 * x2, axis=-1, keepdims=True) + EPS) * g_ref[...]
    gate = _sigmoid(_dot(hn.astype(BF16), wpg_ref[...]))
    y_ref[...] = x2 + _dot(p_ref[...], wp_ref[...]) * gate


def _combine_ple(x1, y_slots, route_w, p, w_ple, gain, w_ple_gate, *, row0, rows, slot_rows):
    t, d = x1.shape
    tm = _row_tile(math.gcd(math.gcd(rows, row0), math.gcd(*slot_rows)), 256)
    r0, s0, s1 = row0 // tm, slot_rows[0] // tm, slot_rows[1] // tm
    row = lambda w: pl.BlockSpec((tm, w), lambda i: (r0 + i, 0))
    once = lambda shape: pl.BlockSpec(shape, lambda i: (0, 0), pipeline_mode=pl.Buffered(1))
    return pl.pallas_call(
        _combine_ple_kernel,
        out_shape=jax.ShapeDtypeStruct((rows, d), F32),
        grid=(rows // tm,),
        in_specs=[row(d),
                  pl.BlockSpec((tm * (d // LANES), LANES), lambda i: (s0 + i, 0)),
                  pl.BlockSpec((tm * (d // LANES), LANES), lambda i: (s1 + i, 0)),
                  row(LANES), row(p.shape[1]),
                  once(w_ple.shape), once((1, d)), once(w_ple_gate.shape)],
        out_specs=pl.BlockSpec((tm, d), lambda i: (i, 0)),
        compiler_params=_cparams("parallel"),
        name="combine_ple_r%d" % row0,
    )(x1, y_slots, y_slots, route_w, p, w_ple, gain, w_ple_gate)


def _pad_rows(a, rows):
    return jnp.pad(a, ((0, 0), (0, rows - a.shape[1]), (0, 0)))


def _layer(x_all, p_all, nb_p, seq_p, nb_s, seq_s, past_len, state_conv, state_delta, cache_k, cache_v,
           page_table, lam_init, lw):
    tp = nb_p * seq_p
    ts = nb_s * seq_s
    d = x_all.shape[1]
    heads_d = lw['a_log'].shape[0]
    dk = lw['delta_norm'].shape[0]
    hk = heads_d * dk
    conv_dim = 3 * hk
    comp = lw['q_norm'].shape[0]
    dv = 2 * comp
    width = lw['w_ob'].shape[0]
    heads_a = width // dv
    n_groups = lw['w_rg'].shape[1]
    n_experts = lw['w_re'].shape[1]
    assert seq_s <= 8 and conv_dim % 1024 == 0 and hk == 1024 and width == 1024 and d % 1024 == 0
    assert 2 * heads_d <= LANES and n_groups + n_experts <= LANES and lw['conv_w'].shape[0] == 4

    w_in = lw['w_in']
    ba0 = conv_dim + hk
    q0 = ba0 + 2 * heads_d
    gate0 = q0 + 3 * width
    w_parts = (w_in[:, :ba0].astype(BF16), w_in[:, gate0:].astype(BF16), w_in[:, q0:gate0].astype(BF16))
    w_ba = jnp.pad(w_in[:, ba0:q0], ((0, 0), (0, LANES - 2 * heads_d))).astype(BF16)
    z_cb = conv_dim // hk
    gate_cb = (conv_dim + hk) // d
    q_cb = (conv_dim + hk + 2 * d) // width
    assert (conv_dim + hk) % d == 0 and (conv_dim + hk + 2 * d) % width == 0

    proj, ba = _inproj(x_all, lw['norm_attn'].reshape(1, d), w_parts, w_ba)

    alog_row = jnp.pad(lw['a_log'], (heads_d, LANES - 2 * heads_d)).reshape(1, LANES)
    dtb_row = jnp.pad(lw['dt_bias'], (heads_d, LANES - 2 * heads_d)).reshape(1, LANES)
    dn = lw['delta_norm'].reshape(1, dk)
    chunk_p = math.gcd(seq_p, DELTA_CHUNK)
    oa_p, s_p = _delta_branch(proj, ba, 0, nb_p, seq_p,
                              jnp.zeros((nb_p, 8, conv_dim), F32), jnp.zeros((nb_p, heads_d, dk, dk), F32),
                              lw['conv_w'], alog_row, dtb_row, dn,
                              chunk=chunk_p, valid=chunk_p, heads=heads_d, dk=dk, z_col_block=z_cb)
    qkv_s = proj[tp:, :conv_dim].reshape(nb_s, seq_s, conv_dim)
    proj_s8 = _pad_rows(proj[tp:].reshape(nb_s, seq_s, -1), 8).reshape(nb_s * 8, -1)
    ba_s8 = _pad_rows(ba[tp:].reshape(nb_s, seq_s, LANES), 8).reshape(nb_s * 8, LANES)
    tail_s = jnp.pad(state_conv, ((0, 0), (8 - state_conv.shape[1], 0), (0, 0)))
    oa_s8, s_s = _delta_branch(proj_s8, ba_s8, 0, nb_s, 8, tail_s, state_delta,
                               lw['conv_w'], alog_row, dtb_row, dn,
                               chunk=8, valid=seq_s, heads=heads_d, dk=dk, z_col_block=z_cb)
    oa_s = oa_s8.reshape(nb_s, 8, hk)[:, :seq_s].reshape(ts, hk)
    o_a = jnp.concatenate([oa_p, oa_s], axis=0)
    conv_p = jnp.stack([proj[(b + 1) * seq_p - 3:(b + 1) * seq_p, :conv_dim] for b in range(nb_p)])
    conv_s = jnp.concatenate([state_conv, qkv_s], axis=1)[:, -3:, :]

    half = comp // 2
    inv_freq = ROPE_THETA ** (-jnp.arange(half, dtype=F32) / half)
    def rope_tables(pos):
        ang = pos.astype(F32)[:, None] * inv_freq[None, :]
        sin_h = jnp.sin(ang)
        return (jnp.tile(jnp.cos(ang), (1, LANES // half)),
                jnp.tile(jnp.concatenate([-sin_h, sin_h], axis=1), (1, LANES // comp)))

    q_gain = jnp.tile(lw['q_norm'], LANES // comp).reshape(1, LANES)
    k_gain = jnp.tile(lw['k_norm'], LANES // comp).reshape(1, LANES)
    tk = _row_tile(seq_p, 512, align=LANES)
    q_t, k_bf, v_t, k_t = _qkprep(proj, 0, nb_p, seq_p, *rope_tables(jnp.arange(seq_p, dtype=jnp.int32)),
                                  q_gain, k_gain, q_col_block=q_cb, width=width, comp=comp, for_prompt=True, tk=tk)
    pos_s = jnp.tile(past_len + jnp.arange(seq_s, dtype=jnp.int32), nb_s)
    q_fs, k_fs = _qkprep(proj, tp, 1, ts, *rope_tables(pos_s), q_gain, k_gain,
                         q_col_block=q_cb, width=width, comp=comp, for_prompt=False)
    lam_vecs = jnp.stack([lw['lam_q1'], lw['lam_k1'], lw['lam_q2'], lw['lam_k2']])
    subln = lw['subln'].reshape(1, dv)
    ob_p = _attn_prompt(q_t, k_bf, v_t, lam_vecs, lw['subln'].reshape(dv, 1), nb=nb_p, seq=seq_p, tk=tk,
                        heads=heads_a, comp=comp, lam_init=lam_init)
    v_f_s = proj[tp:, (q_cb + 2) * width:(q_cb + 3) * width]
    q8 = _pad_rows(q_fs.reshape(nb_s, seq_s, width), 8)
    k8 = _pad_rows(k_fs.reshape(nb_s, seq_s, width), 8)
    v8 = _pad_rows(v_f_s.reshape(nb_s, seq_s, width), 8)
    n_phys, page = cache_k.shape[0], cache_k.shape[1]
    cache_kt = jnp.transpose(cache_k, (0, 2, 3, 4, 1)).reshape(n_phys, width, page)
    cache_v2 = cache_v.reshape(n_phys, page * heads_a, dv)
    ob_s8 = _attn_sample(q8, k8, v8, cache_kt, cache_v2, page_table, lam_vecs, subln,
                         new_len=seq_s, heads=heads_a, comp=comp, lam_init=lam_init)
    o_b = jnp.concatenate([ob_p, ob_s8[:, :seq_s].reshape(ts, width)], axis=0)

    merged = _merge(o_a, o_b, proj, lw['w_oa'].astype(BF16), lw['w_ob'].astype(BF16), gate_col_block=gate_cb)
    w_route = jnp.pad(jnp.concatenate([lw['w_rg'], lw['w_re']], axis=1),
                      ((0, 0), (0, LANES - n_groups - n_experts))).astype(BF16)
    b_route = jnp.pad(jnp.concatenate([lw['b_rg'], lw['b_re']]), (0, LANES - n_groups - n_experts)).reshape(1, LANES)
    x1, h2, route_w, route_i, route_cnt = _outproj_route(x_all, merged, lw['w_out'].astype(BF16),
                                                         lw['norm_ffn'].reshape(1, d), w_route, b_route,
                                                         n_groups=n_groups, per_group=n_experts // n_groups)

    counts = route_cnt[0, n_groups:n_groups + n_experts].astype(jnp.int32)
    block_e, src_tok, dst_slot, nused = _expert_plan(route_i[:, :TOP_K], route_i[:, TOP_K:2 * TOP_K], counts, tp)
    y_slots = _experts(h2, block_e, src_tok, dst_slot, nused, lw['exp_gate'], lw['exp_up'], lw['exp_down'])

    ple_args = (x1, y_slots, route_w, p_all.astype(BF16), lw['w_ple'].astype(BF16),
                lw['norm_ple'].reshape(1, d), lw['w_ple_gate'].astype(BF16))
    y_p = _combine_ple(*ple_args, row0=0, rows=tp, slot_rows=(0, tp))
    y_s = _combine_ple(*ple_args, row0=tp, rows=ts, slot_rows=(TOP_K * tp, TOP_K * tp + ts))
    y = (y_p, y_s)

    k_p = jnp.transpose(k_t.reshape(nb_p, heads_a, 2, comp, seq_p), (0, 4, 1, 2, 3))
    v_p = proj[:tp, (q_cb + 2) * width:(q_cb + 3) * width].reshape(nb_p, seq_p, heads_a, dv)
    k_s = k_fs.reshape(nb_s, seq_s, heads_a, 2, comp)
    v_s = v_f_s.reshape(nb_s, seq_s, heads_a, dv)
    return y, k_p, v_p, k_s, v_s, conv_p, conv_s, s_p, s_s


def kernel(x_prompt, x_sample, cache_k, cache_v, state_conv, state_delta, page_table, p_prompt, p_sample,
           norm_attn, w_in, conv_w, a_log, dt_bias, delta_norm, q_norm, k_norm, lam_q1, lam_k1, lam_q2, lam_k2,
           subln, w_oa, w_ob, w_out, norm_ffn, w_rg, b_rg, w_re, b_re, exp_gate, exp_up, exp_down,
           norm_ple, w_ple, w_ple_gate):
    nb_p, seq_p, d = x_prompt.shape
    nb_s, seq_s, _ = x_sample.shape
    tp = nb_p * seq_p
    depth = w_in.shape[0]
    past_len = page_table.shape[1] * cache_k.shape[2]
    y_p, y_s = x_prompt.reshape(tp, d), x_sample.reshape(nb_s * seq_s, d)
    outs = [[] for _ in range(8)]
    for i in range(depth):
        x_all = jnp.concatenate([y_p, y_s], axis=0)
        lw = dict(norm_attn=norm_attn[i], w_in=w_in[i], conv_w=conv_w[i], a_log=a_log[i], dt_bias=dt_bias[i],
                  delta_norm=delta_norm[i], q_norm=q_norm[i], k_norm=k_norm[i], lam_q1=lam_q1[i], lam_k1=lam_k1[i],
                  lam_q2=lam_q2[i], lam_k2=lam_k2[i], subln=subln[i], w_oa=w_oa[i], w_ob=w_ob[i], w_out=w_out[i],
                  norm_ffn=norm_ffn[i], w_rg=w_rg[i], b_rg=b_rg[i], w_re=w_re[i], b_re=b_re[i],
                  exp_gate=exp_gate[i], exp_up=exp_up[i], exp_down=exp_down[i],
                  norm_ple=norm_ple[i], w_ple=w_ple[i], w_ple_gate=w_ple_gate[i])
        lam_init = 0.8 - 0.6 * math.exp(-0.3 * i)
        p_all = jnp.concatenate([p_prompt[i].reshape(tp, -1), p_sample[i].reshape(nb_s * seq_s, -1)], axis=0)
        res = _layer(x_all, p_all, nb_p, seq_p, nb_s, seq_s, past_len, state_conv[i], state_delta[i],
                     cache_k[i], cache_v[i], page_table, lam_init, lw)
        y_p, y_s = res[0]
        for lst, val in zip(outs, res[1:]):
            lst.append(val)
    return (y_p.reshape(nb_p, seq_p, d), y_s.reshape(nb_s, seq_s, d)) + tuple(jnp.stack(lst) for lst in outs)
```

```python
import functools
import math

import jax
import jax.numpy as jnp
from jax import lax
from jax.experimental import pallas as pl
from jax.experimental.pallas import tpu as pltpu

F32 = jnp.float32
BF16 = jnp.bfloat16
HIGHEST = lax.Precision.HIGHEST

LANES = 128
DELTA_CHUNK = 64
ROPE_THETA = 10000.0
TOP_K = 2
MOE_BLOCK = 128
EPS = 1e-6
VMEM_LIMIT = 48 * 1024 * 1024


def _cparams(*sem):
    return pltpu.CompilerParams(dimension_semantics=sem, vmem_limit_bytes=VMEM_LIMIT)


def _row_tile(n, target, align=16):
    best = None
    for t in range(align, min(n, target) + 1, align):
        if n % t == 0:
            best = t
    assert best is not None, (n, target, align)
    return best


def _nt_dot(a, b, precision=None):
    return lax.dot_general(a, b, (((1,), (1,)), ((), ())), precision=precision,
                           preferred_element_type=F32)


def _tn_dot(a, b, precision=None):
    return lax.dot_general(a, b, (((0,), (0,)), ((), ())), precision=precision,
                           preferred_element_type=F32)


def _dot(a, b, precision=None):
    return jnp.dot(a, b, precision=precision, preferred_element_type=F32)


def _split_bf16(a):
    hi = a.astype(BF16)
    return hi, (a - hi.astype(F32)).astype(BF16)


def _dot_split(a, b):
    return _dot(a[0], b[0]) + (_dot(a[0], b[1]) + _dot(a[1], b[0]))


def _sigmoid(x):
    return 1.0 / (1.0 + jnp.exp(-x))


def _silu(x):
    return x * _sigmoid(x)


def _iota(shape, dim):
    return lax.broadcasted_iota(jnp.int32, shape, dim)


def _inproj_kernel(x_ref, g_ref, *rest, part_tiles):
    w_refs = rest[:len(part_tiles)]
    wba_ref, p_ref, ba_ref, h_scr = rest[len(part_tiles):]
    j = pl.program_id(1)

    @pl.when(j == 0)
    def _():
        x = x_ref[...]
        h = x * lax.rsqrt(jnp.mean(x * x, axis=-1, keepdims=True) + EPS) * g_ref[...]
        h_scr[...] = h.astype(BF16)
        ba_ref[...] = _dot(h_scr[...], wba_ref[...])

    lo = 0
    for w_ref, n_tiles in zip(w_refs, part_tiles):
        @pl.when((j >= lo) & (j < lo + n_tiles))
        def _(w_ref=w_ref):
            p_ref[...] = _dot(h_scr[...], w_ref[...])
        lo += n_tiles


def _inproj(x, gain, w_parts, w_ba):
    t, d = x.shape
    tm = _row_tile(t, 640)
    tn = 1024
    part_tiles = tuple(w.shape[1] // tn for w in w_parts)
    assert all(w.shape[1] % tn == 0 for w in w_parts)
    n_tiles = sum(part_tiles)
    starts = [sum(part_tiles[:k]) for k in range(len(w_parts))]

    def w_spec(k):
        return pl.BlockSpec((d, tn), lambda i, j: (0, jnp.clip(j - starts[k], 0, part_tiles[k] - 1)))

    return pl.pallas_call(
        functools.partial(_inproj_kernel, part_tiles=part_tiles),
        out_shape=(jax.ShapeDtypeStruct((t, n_tiles * tn), F32), jax.ShapeDtypeStruct((t, LANES), F32)),
        grid=(t // tm, n_tiles),
        in_specs=[pl.BlockSpec((tm, d), lambda i, j: (i, 0)),
                  pl.BlockSpec((1, d), lambda i, j: (0, 0))]
                 + [w_spec(k) for k in range(len(w_parts))]
                 + [pl.BlockSpec((d, LANES), lambda i, j: (0, 0))],
        out_specs=(pl.BlockSpec((tm, tn), lambda i, j: (i, j)),
                   pl.BlockSpec((tm, LANES), lambda i, j: (i, 0))),
        scratch_shapes=[pltpu.VMEM((tm, d), BF16)],
        compiler_params=_cparams("parallel", "arbitrary"),
        name="inproj",
    )(x, gain, *w_parts, w_ba)


def _delta_kernel(qkv_ref, z_ref, ba_ref, tail0_ref, s0_ref, cw_ref, alog_ref, dtb_ref, dn_ref,
                  o_ref, sfin_ref, ext_scr, s_scr, *, chunk, valid, heads, dk):
    n = pl.program_id(1)
    c = chunk
    hk = heads * dk

    @pl.when(n == 0)
    def _():
        ext_scr[0:8, :] = tail0_ref[0]
        s_scr[...] = s0_ref[0]

    ext_scr[8:8 + c, :] = qkv_ref[...]

    row = _iota((c, 1), 0)
    rowmask = (row < valid).astype(F32) if valid < c else None

    ba = ba_ref[...]
    beta_all = _sigmoid(ba)
    xg = ba + dtb_ref[...]
    softplus = jnp.maximum(xg, 0.0) + jnp.log1p(jnp.exp(-jnp.abs(xg)))
    g_all = -jnp.exp(alog_ref[...]) * softplus
    if rowmask is not None:
        beta_all = beta_all * rowmask
        g_all = g_all * rowmask
    r_i = _iota((c, c), 0)
    c_i = _iota((c, c), 1)
    incl = r_i >= c_i
    strict = r_i > c_i
    gc_all = _dot(incl.astype(F32), g_all, precision=HIGHEST)
    gc_t = jnp.transpose(gc_all)
    eye = (r_i == c_i).astype(F32)

    def conv(col0):
        sl = slice(col0, col0 + dk)
        acc = ext_scr[8:8 + c, sl] * cw_ref[3:4, sl]
        acc = acc + ext_scr[7:7 + c, sl] * cw_ref[2:3, sl]
        acc = acc + ext_scr[6:6 + c, sl] * cw_ref[1:2, sl]
        acc = acc + ext_scr[5:5 + c, sl] * cw_ref[0:1, sl]
        return _silu(acc)

    hs = range(heads)
    q, k, v, beta, gc, egc, g_last, decay = [], [], [], [], [], [], [], []
    for h in hs:
        qh = conv(h * dk)
        kh = conv(hk + h * dk)
        vh = conv(2 * hk + h * dk)
        qh = qh * lax.rsqrt(jnp.sum(qh * qh, axis=-1, keepdims=True) + EPS) * (dk ** -0.5)
        kh = kh * lax.rsqrt(jnp.sum(kh * kh, axis=-1, keepdims=True) + EPS)
        if rowmask is not None:
            qh, kh, vh = qh * rowmask, kh * rowmask, vh * rowmask
        q.append(qh)
        k.append(kh)
        v.append(vh)
        beta.append(beta_all[:, h:h + 1])
        gc.append(gc_all[:, heads + h:heads + h + 1])
        egc.append(jnp.exp(gc[h]))
        g_last.append(gc_all[c - 1:c, heads + h:heads + h + 1])
        gc_row = gc_t[heads + h:heads + h + 1, :]
        decay.append(jnp.where(incl, jnp.exp(jnp.where(incl, gc[h] - gc_row, 0.0)), 0.0))
    kb = [k[h] * beta[h] for h in hs]
    kk = [_nt_dot(kb[h], k[h]) for h in hs]
    qk = [_nt_dot(q[h], k[h]) for h in hs]
    apow = [jnp.where(strict, kk[h] * decay[h], 0.0) for h in hs]
    qk = [qk[h] * decay[h] for h in hs]
    a_mat = apow
    same_block = lambda b: (r_i // b) == (c_i // b)
    b0 = min(8, c)
    diag0 = same_block(b0)
    apb = [jnp.where(diag0, a_mat[h], 0.0) for h in hs]
    tinv = [eye - apb[h] for h in hs]
    apb = [apb[h].astype(BF16) for h in hs]
    span = 2
    while span < b0:
        apb = [_dot(apb[h], apb[h]).astype(BF16) for h in hs]
        tinv = [tinv[h] + _dot(tinv[h].astype(BF16), apb[h]) for h in hs]
        span *= 2
    b = b0
    while b < c:
        ring = same_block(2 * b) & ((r_i // b) != (c_i // b))
        off = [jnp.where(ring, a_mat[h], 0.0).astype(BF16) for h in hs]
        tb = [tinv[h].astype(BF16) for h in hs]
        tl = [_dot(tb[h], off[h]).astype(BF16) for h in hs]
        tinv = [tinv[h] - _dot(tl[h], tb[h]) for h in hs]
        b *= 2
    resid = [eye - _dot_split(_split_bf16(eye + a_mat[h]), _split_bf16(tinv[h])) for h in hs]
    tinv = [tinv[h] + _dot(tinv[h].astype(BF16), resid[h].astype(BF16)) for h in hs]
    u = [_dot(tinv[h], v[h] * beta[h]) for h in hs]
    w = [_dot(tinv[h], kb[h] * egc[h]) for h in hs]
    s = [s_scr[h] for h in hs]
    ws = [_dot(w[h], s[h]) for h in hs]
    qs = [_dot(q[h] * egc[h], s[h]) for h in hs]
    v_new = [u[h] - ws[h] for h in hs]
    o = [qs[h] + _dot(qk[h], v_new[h]) for h in hs]
    ds = [_tn_dot(k[h] * jnp.exp(g_last[h] - gc[h]), v_new[h]) for h in hs]
    for h in hs:
        s_scr[h] = s[h] * jnp.exp(g_last[h]) + ds[h]
        on = o[h] * lax.rsqrt(jnp.mean(o[h] * o[h], axis=-1, keepdims=True) + EPS) * dn_ref[...]
        zh = z_ref[:, h * dk:(h + 1) * dk]
        o_ref[:, h * dk:(h + 1) * dk] = (on * _silu(zh)).astype(o_ref.dtype)

    ext_scr[0:8, :] = ext_scr[c:c + 8, :]

    @pl.when(n == pl.num_programs(1) - 1)
    def _():
        sfin_ref[0] = s_scr[...]


def _delta_branch(proj, ba, row0, nb, seq, tail0, s0, conv_w, alog_row, dtb_row, delta_norm, *,
                  chunk, valid, heads, dk, z_col_block):
    hk = heads * dk
    nchunks = seq // chunk
    assert seq % chunk == 0 and row0 % chunk == 0
    r0 = row0 // chunk
    kern = functools.partial(_delta_kernel, chunk=chunk, valid=valid, heads=heads, dk=dk)
    return pl.pallas_call(
        kern,
        out_shape=(jax.ShapeDtypeStruct((nb * seq, hk), BF16),
                   jax.ShapeDtypeStruct((nb, heads, dk, dk), F32)),
        grid=(nb, nchunks),
        in_specs=[pl.BlockSpec((chunk, 3 * hk), lambda b, n: (r0 + b * nchunks + n, 0)),
                  pl.BlockSpec((chunk, hk), lambda b, n: (r0 + b * nchunks + n, z_col_block)),
                  pl.BlockSpec((chunk, LANES), lambda b, n: (r0 + b * nchunks + n, 0)),
                  pl.BlockSpec((1, 8, 3 * hk), lambda b, n: (b, 0, 0)),
                  pl.BlockSpec((1, heads, dk, dk), lambda b, n: (b, 0, 0, 0)),
                  pl.BlockSpec((4, 3 * hk), lambda b, n: (0, 0)),
                  pl.BlockSpec((1, LANES), lambda b, n: (0, 0)),
                  pl.BlockSpec((1, LANES), lambda b, n: (0, 0)),
                  pl.BlockSpec((1, dk), lambda b, n: (0, 0))],
        out_specs=(pl.BlockSpec((chunk, hk), lambda b, n: (b * nchunks + n, 0)),
                   pl.BlockSpec((1, heads, dk, dk), lambda b, n: (b, 0, 0, 0))),
        scratch_shapes=[pltpu.VMEM((8 + chunk, 3 * hk), F32),
                        pltpu.VMEM((heads, dk, dk), F32)],
        compiler_params=_cparams("parallel", "arbitrary"),
        name="delta_c%d" % chunk,
    )(proj, proj, ba, tail0, s0, conv_w, alog_row, dtb_row, delta_norm)


def _qkprep_kernel(q_ref, k_ref, v_ref, cos_ref, sin_ref, qg_ref, kg_ref, *out_refs, comp, scale, for_prompt, tk):
    tm = q_ref.shape[0]
    r_i = _iota((LANES, LANES), 0)
    c_i = _iota((LANES, LANES), 1)
    group = (r_i // comp == c_i // comp).astype(BF16)
    lane = _iota((tm, LANES), 1)
    first_half = (lane % comp) < (comp // 2)
    cos = cos_ref[...]
    sin = sin_ref[...]

    def norm_rope(x, gain):
        sq = x * x
        hi = sq.astype(BF16)
        lo = (sq - hi.astype(F32)).astype(BF16)
        ms = (_dot(hi, group) + _dot(lo, group)) * (1.0 / comp)
        y = x * lax.rsqrt(ms + EPS) * gain
        swapped = jnp.where(first_half, pltpu.roll(y, LANES - comp // 2, 1), pltpu.roll(y, comp // 2, 1))
        return y * cos + swapped * sin

    for j in range(q_ref.shape[1] // LANES):
        sl = slice(j * LANES, (j + 1) * LANES)
        qr = norm_rope(q_ref[:, sl], qg_ref[...]) * scale
        kr = norm_rope(k_ref[:, sl], kg_ref[...])
        if for_prompt:
            qt_ref, kbo_ref, vt_ref, kt_ref = out_refs
            qt_ref[0, sl, :] = jnp.transpose(qr).astype(BF16)
            kbo_ref[:, sl] = kr.astype(BF16)
            kt_ref[0, sl, :] = jnp.transpose(kr)
            vt = jnp.transpose(v_ref[:, sl]).astype(BF16)
            for u in range(tm // tk):
                vt_ref[0, u, sl, :] = vt[:, u * tk:(u + 1) * tk]
        else:
            qo_ref, kfo_ref = out_refs
            qo_ref[:, sl] = qr
            kfo_ref[:, sl] = kr


def _qkprep(proj, row0, nb, seq, cos_t, sin_t, q_gain, k_gain, *, q_col_block, width, comp, for_prompt, tk=None):
    tm = _row_tile(seq, 512, align=LANES) if for_prompt else seq
    assert row0 % tm == 0 and seq % tm == 0 and (not for_prompt or tm % tk == 0)
    r0, per_seq = row0 // tm, seq // tm
    scale = comp ** -0.5 * (math.log2(math.e) if for_prompt else 1.0)
    kern = functools.partial(_qkprep_kernel, comp=comp, scale=scale, for_prompt=for_prompt, tk=tk)
    col = lambda cb: pl.BlockSpec((tm, width), lambda i: (r0 + i, cb))
    small = pl.BlockSpec((tm, LANES), lambda i: (i % per_seq, 0))
    gain = pl.BlockSpec((1, LANES), lambda i: (0, 0))
    out = pl.BlockSpec((tm, width), lambda i: (i, 0))
    rows = nb * seq
    if for_prompt:
        out_t = pl.BlockSpec((1, width, tm), lambda i: (i // per_seq, 0, i % per_seq))
        out_shape = (jax.ShapeDtypeStruct((nb, width, seq), BF16), jax.ShapeDtypeStruct((rows, width), BF16),
                     jax.ShapeDtypeStruct((nb, seq // tk, width, tk), BF16),
                     jax.ShapeDtypeStruct((nb, width, seq), F32))
        out_specs = (out_t, out,
                     pl.BlockSpec((1, tm // tk, width, tk), lambda i: (i // per_seq, i % per_seq, 0, 0)), out_t)
    else:
        out_shape = (jax.ShapeDtypeStruct((rows, width), F32), jax.ShapeDtypeStruct((rows, width), F32))
        out_specs = (out, out)
    return pl.pallas_call(
        kern,
        out_shape=out_shape,
        grid=(rows // tm,),
        in_specs=[col(q_col_block), col(q_col_block + 1), col(q_col_block + 2), small, small, gain, gain],
        out_specs=out_specs,
        compiler_params=_cparams("parallel"),
        name="qkprep_prompt" if for_prompt else "qkprep_sample",
    )(proj, proj, proj, cos_t, sin_t, q_gain, k_gain)


def _lambda_value(lam_ref, lam_init):
    lv = lam_ref[...]
    s1 = jnp.sum(lv[0:1] * lv[1:2], axis=-1, keepdims=True)
    s2 = jnp.sum(lv[2:3] * lv[3:4], axis=-1, keepdims=True)
    return jnp.exp(s1) - jnp.exp(s2) + lam_init


def _attn_prompt_kernel(q_ref, k_ref, v_ref, lam_ref, sub_ref, o_ref, *, tq, tk, comp, heads_per_step, lam_init):
    i = pl.program_id(2)
    dv = 2 * comp
    hr = range(heads_per_step)
    row = _iota((dv, tq), 0)
    qs = []
    for hh in hr:
        qt = q_ref[0, hh * dv:(hh + 1) * dv, :]
        zero = jnp.zeros_like(qt)
        qs.append(jnp.concatenate([jnp.where(row < comp, qt, zero), jnp.where(row >= comp, qt, zero)], axis=1))

    def update(carry, j, mask):
        start = pl.multiple_of(j * tk, tk)
        kt = [k_ref[pl.ds(start, tk), hh * dv:(hh + 1) * dv] for hh in hr]
        vt = [v_ref[0, j, hh * dv:(hh + 1) * dv, :] for hh in hr]
        s = [_dot(kt[hh], qs[hh]) for hh in hr]
        if mask is not None:
            s = [jnp.where(mask, sn, -jnp.inf) for sn in s]
        m_new = [jnp.maximum(carry[hh][0], jnp.max(s[hh], axis=0, keepdims=True)) for hh in hr]
        p = [jnp.exp2(s[hh] - m_new[hh]) for hh in hr]
        corr = [jnp.exp2(carry[hh][0] - m_new[hh]) for hh in hr]
        l = [carry[hh][1] * corr[hh] + jnp.sum(p[hh], axis=0, keepdims=True) for hh in hr]
        pv = [_dot(vt[hh], p[hh].astype(BF16)) for hh in hr]
        return tuple((m_new[hh], l[hh], carry[hh][2] * corr[hh] + pv[hh]) for hh in hr)

    def body(j, carry):
        return update(carry, j, None)

    init = tuple((jnp.full((1, 2 * tq), -jnp.inf, F32), jnp.zeros((1, 2 * tq), F32),
                  jnp.zeros((dv, 2 * tq), F32)) for _ in hr)
    n_full = (i * tq) // tk
    carry = lax.fori_loop(0, n_full, body, init)
    kpos = n_full * tk + _iota((tk, 2 * tq), 0)
    qpos = i * tq + _iota((tk, 2 * tq), 1) % tq
    carry = update(carry, n_full, qpos >= kpos)
    lam = _lambda_value(lam_ref, lam_init)
    for hh in hr:
        _, l, a = carry[hh]
        n = a / l
        o = n[:, 0:tq] - lam * n[:, tq:2 * tq]
        on = o * lax.rsqrt(jnp.mean(o * o, axis=0, keepdims=True) + EPS) * sub_ref[...]
        o_ref[:, hh * dv:(hh + 1) * dv] = jnp.transpose(on * (1.0 - lam_init)).astype(o_ref.dtype)


def _attn_prompt(q_t, k, v_t, lam_vecs, subln_col, *, nb, seq, tk, heads, comp, lam_init):
    dv = 2 * comp
    tq = _row_tile(seq, 512, align=LANES)
    assert tk % tq == 0 and seq % tk == 0
    nq = seq // tq
    hps = 2
    assert heads % hps == 0
    kern = functools.partial(_attn_prompt_kernel, tq=tq, tk=tk, comp=comp, heads_per_step=hps, lam_init=lam_init)
    return pl.pallas_call(
        kern,
        out_shape=jax.ShapeDtypeStruct((nb * seq, heads * dv), BF16),
        grid=(nb, heads // hps, nq),
        in_specs=[pl.BlockSpec((1, hps * dv, tq), lambda b, h, i: (b, h, i)),
                  pl.BlockSpec((seq, hps * dv), lambda b, h, i: (b, h)),
                  pl.BlockSpec((1, seq // tk, hps * dv, tk), lambda b, h, i: (b, 0, h, 0)),
                  pl.BlockSpec((4, comp), lambda b, h, i: (0, 0)),
                  pl.BlockSpec((dv, 1), lambda b, h, i: (0, 0))],
        out_specs=pl.BlockSpec((tq, hps * dv), lambda b, h, i: (b * nq + i, h)),
        compiler_params=_cparams("parallel", "parallel", "arbitrary"),
        name="attn_prompt",
    )(q_t, k, v_t, lam_vecs, subln_col)


def _attn_sample_kernel(pt_ref, q_ref, kn_ref, vn_ref, lam_ref, sub_ref, *rest,
                        pages_per_step, new_len, heads, comp, lam_init):
    pp = pages_per_step
    kt_refs = rest[:pp]
    v_refs = rest[pp:2 * pp]
    o_ref = rest[2 * pp]
    qbd_scr, m_scr, l_scr, acc_scr = rest[2 * pp + 1:]
    p_id = pl.program_id(1)
    rph = 2 * new_len
    rows = heads * rph
    dv = 2 * comp
    width = heads * dv
    page = kt_refs[0].shape[2]

    @pl.when(p_id == 0)
    def _():
        q = q_ref[0][0:new_len, :]
        lane_grp = _iota((new_len, width), 1) // comp
        for h in range(heads):
            for c in range(2):
                r0 = h * rph + c * new_len
                qbd_scr[r0:r0 + new_len, :] = jnp.where(lane_grp == 2 * h + c, q, 0.0)
        m_scr[...] = jnp.full(m_scr.shape, -jnp.inf, F32)
        l_scr[...] = jnp.zeros(l_scr.shape, F32)
        acc_scr[...] = jnp.zeros(acc_scr.shape, F32)

    qbd = qbd_scr[...]

    def online(s_list, v_of):
        m = m_scr[...]
        m_new = m
        for s in s_list:
            m_new = jnp.maximum(m_new, jnp.max(s, axis=-1, keepdims=True))
        corr = jnp.exp(m - m_new)
        p_list = [jnp.exp(s - m_new) for s in s_list]
        l = l_scr[...] * corr
        for p in p_list:
            l = l + jnp.sum(p, axis=-1, keepdims=True)
        m_scr[...] = m_new
        l_scr[...] = l
        for h in range(heads):
            hs = slice(h * rph, (h + 1) * rph)
            acc = acc_scr[hs, :] * corr[hs, :]
            for j, p in enumerate(p_list):
                acc = acc + _dot(p[hs, :], v_of(j, h))
            acc_scr[hs, :] = acc

    online([_dot(qbd, kt_refs[r][0]) for r in range(pp)],
           lambda j, h: v_refs[j][0, pl.ds(h, page, stride=heads), :])

    @pl.when(p_id == pl.num_programs(1) - 1)
    def _():
        s = _nt_dot(qbd, kn_ref[0])
        tok = _iota((rows, 8), 0) % new_len
        key = _iota((rows, 8), 1)
        s = jnp.where((tok >= key) & (key < new_len), s, -jnp.inf)
        online([s], lambda j, h: vn_ref[0][:, h * dv:(h + 1) * dv])
        lam = _lambda_value(lam_ref, lam_init)
        nrm = acc_scr[...] / l_scr[...]
        for h in range(heads):
            r0 = h * rph
            oh = nrm[r0:r0 + new_len, :] - lam * nrm[r0 + new_len:r0 + rph, :]
            on = oh * lax.rsqrt(jnp.mean(oh * oh, axis=-1, keepdims=True) + EPS) * sub_ref[...]
            o_ref[0, 0:new_len, h * dv:(h + 1) * dv] = (on * (1.0 - lam_init)).astype(o_ref.dtype)
        if new_len < 8:
            o_ref[0, new_len:8, :] = jnp.zeros((8 - new_len, width), o_ref.dtype)


def _attn_sample(q8, k8, v8, cache_kt, cache_v2, page_table, lam_vecs, subln, *, new_len, heads, comp, lam_init):
    nb, n_pages = page_table.shape
    width, page = cache_kt.shape[1], cache_kt.shape[2]
    dv = 2 * comp
    pp = math.gcd(n_pages, 16)
    assert 2 * new_len == 8
    rows = heads * 2 * new_len
    kern = functools.partial(_attn_sample_kernel, pages_per_step=pp, new_len=new_len, heads=heads,
                             comp=comp, lam_init=lam_init)
    per_req = pl.BlockSpec((1, 8, width), lambda b, p, pt: (b, 0, 0))

    def page_spec(r, shape):
        return pl.BlockSpec((1,) + shape, lambda b, p, pt: (pt[b * n_pages + p * pp + r], 0, 0))

    grid_spec = pltpu.PrefetchScalarGridSpec(
        num_scalar_prefetch=1,
        grid=(nb, n_pages // pp),
        in_specs=[per_req, per_req, per_req,
                  pl.BlockSpec((4, comp), lambda b, p, pt: (0, 0)),
                  pl.BlockSpec((1, dv), lambda b, p, pt: (0, 0))]
                 + [page_spec(r, (width, page)) for r in range(pp)]
                 + [page_spec(r, (page * heads, dv)) for r in range(pp)],
        out_specs=pl.BlockSpec((1, 8, width), lambda b, p, pt: (b, 0, 0)),
        scratch_shapes=[pltpu.VMEM((rows, width), F32), pltpu.VMEM((rows, 1), F32),
                        pltpu.VMEM((rows, 1), F32), pltpu.VMEM((rows, dv), F32)],
    )
    return pl.pallas_call(
        kern,
        out_shape=jax.ShapeDtypeStruct((nb, 8, width), BF16),
        grid_spec=grid_spec,
        compiler_params=_cparams("parallel", "arbitrary"),
        name="attn_sample",
    )(page_table.reshape(-1), q8, k8, v8, lam_vecs, subln, *([cache_kt] * pp), *([cache_v2] * pp))


def _merge_kernel(oa_ref, ob_ref, ga_ref, gb_ref, woa_ref, wob_ref, m_ref):
    ua = _dot(oa_ref[...], woa_ref[...])
    ub = _dot(ob_ref[...], wob_ref[...])
    m_ref[...] = (_sigmoid(ga_ref[...]) * ua + _sigmoid(gb_ref[...]) * ub).astype(m_ref.dtype)


def _merge(o_a, o_b, proj, w_oa, w_ob, *, gate_col_block):
    t, zd = o_a.shape
    d = w_oa.shape[1]
    tm = _row_tile(t, 640)
    once = lambda shape: pl.BlockSpec(shape, lambda i: (0, 0), pipeline_mode=pl.Buffered(1))
    return pl.pallas_call(
        _merge_kernel,
        out_shape=jax.ShapeDtypeStruct((t, d), BF16),
        grid=(t // tm,),
        in_specs=[pl.BlockSpec((tm, zd), lambda i: (i, 0)),
                  pl.BlockSpec((tm, o_b.shape[1]), lambda i: (i, 0)),
                  pl.BlockSpec((tm, d), lambda i: (i, gate_col_block)),
                  pl.BlockSpec((tm, d), lambda i: (i, gate_col_block + 1)),
                  once(w_oa.shape), once(w_ob.shape)],
        out_specs=pl.BlockSpec((tm, d), lambda i: (i, 0)),
        compiler_params=_cparams("parallel"),
        name="merge",
    )(o_a, o_b, proj, proj, w_oa, w_ob)


def _outproj_route_kernel(x_ref, m_ref, wo_ref, g_ref, wr_ref, br_ref,
                          x1_ref, h2_ref, rw_ref, re_ref, cnt_ref, *, n_groups, per_group, parts):
    @pl.when(pl.program_id(0) == 0)
    def _():
        cnt_ref[...] = jnp.zeros(cnt_ref.shape, F32)

    tm = x_ref.shape[0] // parts
    nc = x_ref.shape[1] // LANES
    rows = [slice(k * tm, (k + 1) * tm) for k in range(parts)]
    x1s = [x_ref[sl, :] + _dot(m_ref[sl, :], wo_ref[...]) for sl in rows]
    lane = _iota((tm, LANES), 1).astype(F32)
    big = float(LANES)
    before = (_iota((tm, tm), 0) > _iota((tm, tm), 1)).astype(BF16)

    for k, sl in enumerate(rows):
        x1 = x1s[k]
        x1_ref[sl, :] = x1
        h2 = x1 * lax.rsqrt(jnp.mean(x1 * x1, axis=-1, keepdims=True) + EPS) * g_ref[...]
        for c in range(nc):
            h2_ref[pl.ds(k * tm * nc + c, tm, stride=nc), :] = h2[:, c * LANES:(c + 1) * LANES]
        logits = _dot(h2.astype(BF16), wr_ref[...]) + br_ref[...]

        def masked_softmax(mask):
            lm = jnp.where(mask, logits, -jnp.inf)
            e = jnp.exp(lm - jnp.max(lm, axis=-1, keepdims=True))
            return e / jnp.sum(e, axis=-1, keepdims=True)

        def top1(p, mask):
            v = jnp.max(jnp.where(mask, p, -1.0), axis=-1, keepdims=True)
            idx = jnp.min(jnp.where(mask & (p == v), lane, big), axis=-1, keepdims=True)
            return v, idx

        gmask = lane < n_groups
        pg = masked_softmax(gmask)
        g_w, g_top = top1(pg, gmask)
        lo = n_groups + g_top * per_group
        emask = (lane >= lo) & (lane < lo + per_group)
        pe = masked_softmax(emask)
        v1, i1 = top1(pe, emask)
        emask2 = emask & (lane != i1)
        v2, i2 = top1(pe, emask2)
        denom = v1 + v2
        w1 = g_w * v1 / denom
        w2 = g_w * v2 / denom
        rw_ref[sl, :] = jnp.where(lane == 0.0, w1, jnp.where(lane == 1.0, w2, 0.0))

        sel1 = lane == i1
        sel2 = lane == i2
        picked = jnp.where(sel1 | sel2, 1.0, 0.0)
        prior = _dot(before, picked.astype(BF16)) + cnt_ref[...]
        r1 = jnp.sum(jnp.where(sel1, prior, 0.0), axis=-1, keepdims=True)
        r2 = jnp.sum(jnp.where(sel2, prior, 0.0), axis=-1, keepdims=True)
        cnt_ref[...] = cnt_ref[...] + jnp.sum(picked, axis=0, keepdims=True)
        re_ref[sl, :] = jnp.where(lane == 0.0, i1 - n_groups,
                                  jnp.where(lane == 1.0, i2 - n_groups,
                                            jnp.where(lane == 2.0, r1,
                                                      jnp.where(lane == 3.0, r2, 0.0)))).astype(jnp.int32)


def _outproj_route(x, merged, w_out, gain, w_route, b_route, *, n_groups, per_group):
    t, d = x.shape
    parts = 2
    tm = _row_tile(t, 416, align=16 * parts)
    kern = functools.partial(_outproj_route_kernel, n_groups=n_groups, per_group=per_group, parts=parts)
    row = lambda w: pl.BlockSpec((tm, w), lambda i: (i, 0))
    once = lambda shape: pl.BlockSpec(shape, lambda i: (0, 0), pipeline_mode=pl.Buffered(1))
    return pl.pallas_call(
        kern,
        out_shape=(jax.ShapeDtypeStruct((t, d), F32), jax.ShapeDtypeStruct((t * (d // LANES), LANES), F32),
                   jax.ShapeDtypeStruct((t, LANES), F32), jax.ShapeDtypeStruct((t, LANES), jnp.int32),
                   jax.ShapeDtypeStruct((1, LANES), F32)),
        grid=(t // tm,),
        in_specs=[row(d), row(d), once(w_out.shape), once((1, d)), once((d, LANES)), once((1, LANES))],
        out_specs=(row(d), pl.BlockSpec((tm * (d // LANES), LANES), lambda i: (i, 0)), row(LANES), row(LANES),
                   pl.BlockSpec((1, LANES), lambda i: (0, 0))),
        compiler_params=_cparams("arbitrary"),
        name="outproj_route",
    )(x, merged, w_out, gain, w_route, b_route)


def _experts_kernel(be_ref, nused_ref, first_ref, wslot_ref, nexte_ref, src_ref, src_next_ref, dst_ref,
                    h_hbm, wg_hbm, wu_hbm, wd_hbm, y_hbm, xbuf, ybuf, wgbuf, wubuf, wdbuf, gsem, ssem, wsem):
    g = pl.program_id(0)
    nused = nused_ref[0]
    blk = MOE_BLOCK
    nc = wg_hbm.shape[1] // LANES

    def weight_copies(e, ws):
        return [pltpu.make_async_copy(hbm.at[e], buf.at[ws], wsem.at[ws])
                for hbm, buf in ((wg_hbm, wgbuf), (wu_hbm, wubuf), (wd_hbm, wdbuf))]

    def start_gather(idx_ref, slot):
        def body(r2, carry):
            for u in range(2):
                r = 2 * r2 + u
                row = pl.multiple_of(idx_ref[0, 0, r], nc)
                pltpu.make_async_copy(h_hbm.at[pl.ds(row, nc), :],
                                      xbuf.at[slot, pl.ds(pl.multiple_of(r * nc, nc), nc), :],
                                      gsem.at[slot]).start(priority=u)
            return carry
        lax.fori_loop(0, blk // 2, body, 0, unroll=4)

    def start_scatter(slot):
        def body(r2, carry):
            for u in range(2):
                r = 2 * r2 + u
                row = pl.multiple_of(dst_ref[0, 0, r], nc)
                pltpu.make_async_copy(ybuf.at[slot, pl.ds(pl.multiple_of(r * nc, nc), nc), :],
                                      y_hbm.at[pl.ds(row, nc), :], ssem.at[slot]).start(priority=u)
            return carry
        lax.fori_loop(0, blk // 2, body, 0, unroll=4)

    def wait_gather(slot):
        pltpu.make_async_copy(h_hbm.at[pl.ds(0, blk * nc), :], xbuf.at[slot], gsem.at[slot]).wait()

    def wait_scatter(slot):
        pltpu.make_async_copy(ybuf.at[slot], y_hbm.at[pl.ds(0, blk * nc), :], ssem.at[slot]).wait()

    @pl.when(g < nused)
    def _():
        slot = g % 2

        @pl.when(g == 0)
        def _():
            ybuf[1] = jnp.zeros(ybuf.shape[1:], F32)
            fill = pltpu.make_async_copy(ybuf.at[1], y_hbm.at[pl.ds(y_hbm.shape[0] - blk * nc, blk * nc), :],
                                         ssem.at[1])
            fill.start()
            fill.wait()
            start_gather(src_ref, slot)
            for cp in weight_copies(be_ref[0], wslot_ref[0]):
                cp.start()

        @pl.when(g + 1 < nused)
        def _():
            start_gather(src_next_ref, 1 - slot)

        ws = wslot_ref[g]

        @pl.when(first_ref[g] == 1)
        def _():
            @pl.when(nexte_ref[g] >= 0)
            def _():
                for cp in weight_copies(nexte_ref[g], 1 - ws):
                    cp.start()
            for cp in weight_copies(be_ref[g], ws):
                cp.wait()

        wait_gather(slot)
        x = jnp.concatenate([xbuf[slot, pl.ds(c, blk, stride=nc), :] for c in range(nc)], axis=1)
        hdn = _silu(_dot(x, wgbuf[ws])) * _dot(x, wubuf[ws])
        y = _dot(hdn, wdbuf[ws])
        for c in range(nc):
            ybuf[slot, pl.ds(c, blk, stride=nc), :] = y[:, c * LANES:(c + 1) * LANES]

        @pl.when(g > 0)
        def _():
            wait_scatter(1 - slot)

        start_scatter(slot)

        @pl.when(g == nused - 1)
        def _():
            wait_scatter(slot)


def _experts(h2, block_e, src_tok, dst_slot, nused, w_gate, w_up, w_down):
    d, ff = w_gate.shape[1], w_gate.shape[2]
    nc = d // LANES
    t = h2.shape[0] // nc
    n_blocks = block_e.shape[0]
    n_slots = t * TOP_K
    kern = _experts_kernel
    src3 = (src_tok * nc).reshape(n_blocks, 1, MOE_BLOCK)
    dst3 = (dst_slot * nc).reshape(n_blocks, 1, MOE_BLOCK)
    blk_id = jnp.arange(n_blocks, dtype=jnp.int32)
    used = blk_id < nused[0]
    first = used & ((blk_id == 0) | (block_e != jnp.roll(block_e, 1)))
    run_id = jnp.cumsum(first.astype(jnp.int32)) - 1
    wslot = (run_id % 2).astype(jnp.int32)
    run_expert = jnp.full((n_blocks + 1,), -1, jnp.int32).at[jnp.where(first, run_id, n_blocks)].set(block_e)
    next_e = run_expert[jnp.minimum(run_id + 1, n_blocks)]
    idx_spec = lambda f: pl.BlockSpec((1, 1, MOE_BLOCK), lambda g, *_: (f(g), 0, 0), memory_space=pltpu.SMEM)
    anyspec = pl.BlockSpec(memory_space=pl.ANY)
    grid_spec = pltpu.PrefetchScalarGridSpec(
        num_scalar_prefetch=5,
        grid=(n_blocks,),
        in_specs=[idx_spec(lambda g: g),
                  idx_spec(lambda g: jnp.minimum(g + 1, n_blocks - 1)),
                  idx_spec(lambda g: g),
                  anyspec, anyspec, anyspec, anyspec],
        out_specs=anyspec,
        scratch_shapes=[pltpu.VMEM((2, MOE_BLOCK * nc, LANES), F32), pltpu.VMEM((2, MOE_BLOCK * nc, LANES), F32),
                        pltpu.VMEM((2, d, ff), F32), pltpu.VMEM((2, d, ff), F32), pltpu.VMEM((2, ff, d), F32),
                        pltpu.SemaphoreType.DMA((2,)), pltpu.SemaphoreType.DMA((2,)), pltpu.SemaphoreType.DMA((2,))],
    )
    return pl.pallas_call(
        kern,
        out_shape=jax.ShapeDtypeStruct(((n_slots + MOE_BLOCK) * nc, LANES), F32),
        grid_spec=grid_spec,
        compiler_params=_cparams("arbitrary"),
        name="experts",
    )(block_e, nused, first.astype(jnp.int32), wslot, next_e, src3, src3, dst3, h2, w_gate, w_up, w_down)


def _expert_plan(eidx, rank, counts, tp):
    t = eidx.shape[0]
    n_experts = counts.shape[0]
    s = t * TOP_K
    padded = (counts + MOE_BLOCK - 1) // MOE_BLOCK * MOE_BLOCK
    ends = jnp.cumsum(padded)
    pad_start = ends - padded
    dest = (pad_start[eidx] + rank).reshape(-1)
    n_blocks = -(-(s + n_experts * (MOE_BLOCK - 1)) // MOE_BLOCK)
    rows = n_blocks * MOE_BLOCK
    slot_of_row = jnp.full((rows,), -1, jnp.int32).at[dest].set(jnp.arange(s, dtype=jnp.int32))
    valid = slot_of_row >= 0
    tok = slot_of_row // TOP_K
    src_tok = jnp.where(valid, tok, 0)
    kk = slot_of_row % TOP_K
    dst_real = jnp.where(tok < tp, kk * tp + tok, TOP_K * tp + kk * (t - tp) + (tok - tp))
    dst_slot = jnp.where(valid, dst_real, s + jnp.arange(rows, dtype=jnp.int32) % MOE_BLOCK)
    block_start = jnp.arange(n_blocks, dtype=jnp.int32) * MOE_BLOCK
    block_e = jnp.minimum(jnp.sum((ends[None, :] <= block_start[:, None]).astype(jnp.int32), axis=1), n_experts - 1)
    nused = (ends[-1:] // MOE_BLOCK).astype(jnp.int32)
    return block_e, src_tok, dst_slot, nused


def _combine_ple_kernel(x1_ref, ys0_ref, ys1_ref, rw_ref, p_ref, wp_ref, g_ref, wpg_ref, y_ref):
    rw = rw_ref[...]
    tm, d = x1_ref.shape
    nc = d // LANES

    def token_rows(ref):
        return jnp.concatenate([ref[pl.ds(c, tm, stride=nc), :] for c in range(nc)], axis=1)

    moe = rw[:, 0:1] * token_rows(ys0_ref) + rw[:, 1:2] * token_rows(ys1_ref)
    x2 = x1_ref[...] + moe
    hn = x2 * lax.rsqrt(jnp.mean(x2 * x2, axis=-1, keepdims=True) + EPS) * g_ref[...]
    gate = _sigmoid(_dot(hn.astype(BF16), wpg_ref[...]))
    y_ref[...] = x2 + _dot(p_ref[...], wp_ref[...]) * gate


def _combine_ple(x1, y_slots, route_w, p, w_ple, gain, w_ple_gate, *, row0, rows, slot_rows):
    t, d = x1.shape
    tm = _row_tile(math.gcd(math.gcd(rows, row0), math.gcd(*slot_rows)), 256)
    r0, s0, s1 = row0 // tm, slot_rows[0] // tm, slot_rows[1] // tm
    row = lambda w: pl.BlockSpec((tm, w), lambda i: (r0 + i, 0))
    once = lambda shape: pl.BlockSpec(shape, lambda i: (0, 0), pipeline_mode=pl.Buffered(1))
    return pl.pallas_call(
        _combine_ple_kernel,
        out_shape=jax.ShapeDtypeStruct((rows, d), F32),
        grid=(rows // tm,),
        in_specs=[row(d),
                  pl.BlockSpec((tm * (d // LANES), LANES), lambda i: (s0 + i, 0)),
                  pl.BlockSpec((tm * (d // LANES), LANES), lambda i: (s1 + i, 0)),
                  row(LANES), row(p.shape[1]),
                  once(w_ple.shape), once((1, d)), once(w_ple_gate.shape)],
        out_specs=pl.BlockSpec((tm, d), lambda i: (i, 0)),
        compiler_params=_cparams("parallel"),
        name="combine_ple_r%d" % row0,
    )(x1, y_slots, y_slots, route_w, p, w_ple, gain, w_ple_gate)


def _pad_rows(a, rows):
    return jnp.pad(a, ((0, 0), (0, rows - a.shape[1]), (0, 0)))


def _layer(x_all, p_all, nb_p, seq_p, nb_s, seq_s, past_len, state_conv, state_delta, cache_k, cache_v,
           page_table, lam_init, lw):
    tp = nb_p * seq_p
    ts = nb_s * seq_s
    d = x_all.shape[1]
    heads_d = lw['a_log'].shape[0]
    dk = lw['delta_norm'].shape[0]
    hk = heads_d * dk
    conv_dim = 3 * hk
    comp = lw['q_norm'].shape[0]
    dv = 2 * comp
    width = lw['w_ob'].shape[0]
    heads_a = width // dv
    n_groups = lw['w_rg'].shape[1]
    n_experts = lw['w_re'].shape[1]
    assert seq_s <= 8 and conv_dim % 1024 == 0 and hk == 1024 and width == 1024 and d % 1024 == 0
    assert 2 * heads_d <= LANES and n_groups + n_experts <= LANES and lw['conv_w'].shape[0] == 4

    w_in = lw['w_in']
    ba0 = conv_dim + hk
    q0 = ba0 + 2 * heads_d
    gate0 = q0 + 3 * width
    w_parts = (w_in[:, :ba0].astype(BF16), w_in[:, gate0:].astype(BF16), w_in[:, q0:gate0].astype(BF16))
    w_ba = jnp.pad(w_in[:, ba0:q0], ((0, 0), (0, LANES - 2 * heads_d))).astype(BF16)
    z_cb = conv_dim // hk
    gate_cb = (conv_dim + hk) // d
    q_cb = (conv_dim + hk + 2 * d) // width
    assert (conv_dim + hk) % d == 0 and (conv_dim + hk + 2 * d) % width == 0

    proj, ba = _inproj(x_all, lw['norm_attn'].reshape(1, d), w_parts, w_ba)

    alog_row = jnp.pad(lw['a_log'], (heads_d, LANES - 2 * heads_d)).reshape(1, LANES)
    dtb_row = jnp.pad(lw['dt_bias'], (heads_d, LANES - 2 * heads_d)).reshape(1, LANES)
    dn = lw['delta_norm'].reshape(1, dk)
    chunk_p = math.gcd(seq_p, DELTA_CHUNK)
    oa_p, s_p = _delta_branch(proj, ba, 0, nb_p, seq_p,
                              jnp.zeros((nb_p, 8, conv_dim), F32), jnp.zeros((nb_p, heads_d, dk, dk), F32),
                              lw['conv_w'], alog_row, dtb_row, dn,
                              chunk=chunk_p, valid=chunk_p, heads=heads_d, dk=dk, z_col_block=z_cb)
    qkv_s = proj[tp:, :conv_dim].reshape(nb_s, seq_s, conv_dim)
    proj_s8 = _pad_rows(proj[tp:].reshape(nb_s, seq_s, -1), 8).reshape(nb_s * 8, -1)
    ba_s8 = _pad_rows(ba[tp:].reshape(nb_s, seq_s, LANES), 8).reshape(nb_s * 8, LANES)
    tail_s = jnp.pad(state_conv, ((0, 0), (8 - state_conv.shape[1], 0), (0, 0)))
    oa_s8, s_s = _delta_branch(proj_s8, ba_s8, 0, nb_s, 8, tail_s, state_delta,
                               lw['conv_w'], alog_row, dtb_row, dn,
                               chunk=8, valid=seq_s, heads=heads_d, dk=dk, z_col_block=z_cb)
    oa_s = oa_s8.reshape(nb_s, 8, hk)[:, :seq_s].reshape(ts, hk)
    o_a = jnp.concatenate([oa_p, oa_s], axis=0)
    conv_p = jnp.stack([proj[(b + 1) * seq_p - 3:(b + 1) * seq_p, :conv_dim] for b in range(nb_p)])
    conv_s = jnp.concatenate([state_conv, qkv_s], axis=1)[:, -3:, :]

    half = comp // 2
    inv_freq = ROPE_THETA ** (-jnp.arange(half, dtype=F32) / half)
    def rope_tables(pos):
        ang = pos.astype(F32)[:, None] * inv_freq[None, :]
        sin_h = jnp.sin(ang)
        return (jnp.tile(jnp.cos(ang), (1, LANES // half)),
                jnp.tile(jnp.concatenate([-sin_h, sin_h], axis=1), (1, LANES // comp)))

    q_gain = jnp.tile(lw['q_norm'], LANES // comp).reshape(1, LANES)
    k_gain = jnp.tile(lw['k_norm'], LANES // comp).reshape(1, LANES)
    tk = _row_tile(seq_p, 512, align=LANES)
    q_t, k_bf, v_t, k_t = _qkprep(proj, 0, nb_p, seq_p, *rope_tables(jnp.arange(seq_p, dtype=jnp.int32)),
                                  q_gain, k_gain, q_col_block=q_cb, width=width, comp=comp, for_prompt=True, tk=tk)
    pos_s = jnp.tile(past_len + jnp.arange(seq_s, dtype=jnp.int32), nb_s)
    q_fs, k_fs = _qkprep(proj, tp, 1, ts, *rope_tables(pos_s), q_gain, k_gain,
                         q_col_block=q_cb, width=width, comp=comp, for_prompt=False)
    lam_vecs = jnp.stack([lw['lam_q1'], lw['lam_k1'], lw['lam_q2'], lw['lam_k2']])
    subln = lw['subln'].reshape(1, dv)
    ob_p = _attn_prompt(q_t, k_bf, v_t, lam_vecs, lw['subln'].reshape(dv, 1), nb=nb_p, seq=seq_p, tk=tk,
                        heads=heads_a, comp=comp, lam_init=lam_init)
    v_f_s = proj[tp:, (q_cb + 2) * width:(q_cb + 3) * width]
    q8 = _pad_rows(q_fs.reshape(nb_s, seq_s, width), 8)
    k8 = _pad_rows(k_fs.reshape(nb_s, seq_s, width), 8)
    v8 = _pad_rows(v_f_s.reshape(nb_s, seq_s, width), 8)
    n_phys, page = cache_k.shape[0], cache_k.shape[1]
    cache_kt = jnp.transpose(cache_k, (0, 2, 3, 4, 1)).reshape(n_phys, width, page)
    cache_v2 = cache_v.reshape(n_phys, page * heads_a, dv)
    ob_s8 = _attn_sample(q8, k8, v8, cache_kt, cache_v2, page_table, lam_vecs, subln,
                         new_len=seq_s, heads=heads_a, comp=comp, lam_init=lam_init)
    o_b = jnp.concatenate([ob_p, ob_s8[:, :seq_s].reshape(ts, width)], axis=0)

    merged = _merge(o_a, o_b, proj, lw['w_oa'].astype(BF16), lw['w_ob'].astype(BF16), gate_col_block=gate_cb)
    w_route = jnp.pad(jnp.concatenate([lw['w_rg'], lw['w_re']], axis=1),
                      ((0, 0), (0, LANES - n_groups - n_experts))).astype(BF16)
    b_route = jnp.pad(jnp.concatenate([lw['b_rg'], lw['b_re']]), (0, LANES - n_groups - n_experts)).reshape(1, LANES)
    x1, h2, route_w, route_i, route_cnt = _outproj_route(x_all, merged, lw['w_out'].astype(BF16),
                                                         lw['norm_ffn'].reshape(1, d), w_route, b_route,
                                                         n_groups=n_groups, per_group=n_experts // n_groups)

    counts = route_cnt[0, n_groups:n_groups + n_experts].astype(jnp.int32)
    block_e, src_tok, dst_slot, nused = _expert_plan(route_i[:, :TOP_K], route_i[:, TOP_K:2 * TOP_K], counts, tp)
    y_slots = _experts(h2, block_e, src_tok, dst_slot, nused, lw['exp_gate'], lw['exp_up'], lw['exp_down'])

    ple_args = (x1, y_slots, route_w, p_all.astype(BF16), lw['w_ple'].astype(BF16),
                lw['norm_ple'].reshape(1, d), lw['w_ple_gate'].astype(BF16))
    y_p = _combine_ple(*ple_args, row0=0, rows=tp, slot_rows=(0, tp))
    y_s = _combine_ple(*ple_args, row0=tp, rows=ts, slot_rows=(TOP_K * tp, TOP_K * tp + ts))
    y = (y_p, y_s)

    k_p = jnp.transpose(k_t.reshape(nb_p, heads_a, 2, comp, seq_p), (0, 4, 1, 2, 3))
    v_p = proj[:tp, (q_cb + 2) * width:(q_cb + 3) * width].reshape(nb_p, seq_p, heads_a, dv)
    k_s = k_fs.reshape(nb_s, seq_s, heads_a, 2, comp)
    v_s = v_f_s.reshape(nb_s, seq_s, heads_a, dv)
    return y, k_p, v_p, k_s, v_s, conv_p, conv_s, s_p, s_s


def kernel(x_prompt, x_sample, cache_k, cache_v, state_conv, state_delta, page_table, p_prompt, p_sample,
           norm_attn, w_in, conv_w, a_log, dt_bias, delta_norm, q_norm, k_norm, lam_q1, lam_k1, lam_q2, lam_k2,
           subln, w_oa, w_ob, w_out, norm_ffn, w_rg, b_rg, w_re, b_re, exp_gate, exp_up, exp_down,
           norm_ple, w_ple, w_ple_gate):
    nb_p, seq_p, d = x_prompt.shape
    nb_s, seq_s, _ = x_sample.shape
    tp = nb_p * seq_p
    depth = w_in.shape[0]
    past_len = page_table.shape[1] * cache_k.shape[2]
    y_p, y_s = x_prompt.reshape(tp, d), x_sample.reshape(nb_s * seq_s, d)
    outs = [[] for _ in range(8)]
    for i in range(depth):
        x_all = jnp.concatenate([y_p, y_s], axis=0)
        lw = dict(norm_attn=norm_attn[i], w_in=w_in[i], conv_w=conv_w[i], a_log=a_log[i], dt_bias=dt_bias[i],
                  delta_norm=delta_norm[i], q_norm=q_norm[i], k_norm=k_norm[i], lam_q1=lam_q1[i], lam_k1=lam_k1[i],
                  lam_q2=lam_q2[i], lam_k2=lam_k2[i], subln=subln[i], w_oa=w_oa[i], w_ob=w_ob[i], w_out=w_out[i],
                  norm_ffn=norm_ffn[i], w_rg=w_rg[i], b_rg=b_rg[i], w_re=w_re[i], b_re=b_re[i],
                  exp_gate=exp_gate[i], exp_up=exp_up[i], exp_down=exp_down[i],
                  norm_ple=norm_ple[i], w_ple=w_ple[i], w_ple_gate=w_ple_gate[i])
        lam_init = 0.8 - 0.6 * math.exp(-0.3 * i)
        p_all = jnp.concatenate([p_prompt[i].reshape(tp, -1), p_sample[i].reshape(nb_s * seq_s, -1)], axis=0)
        res = _layer(x_all, p_all, nb_p, seq_p, nb_s, seq_s, past_len, state_conv[i], state_delta[i],
                     cache_k[i], cache_v[i], page_table, lam_init, lw)
        y_p, y_s = res[0]
        for lst, val in zip(outs, res[1:]):
            lst.append(val)
    return (y_p.reshape(nb_p, seq_p, d), y_s.reshape(nb_s, seq_s, d)) + tuple(jnp.stack(lst) for lst in outs)
```

```python
import functools
import math

import jax
import jax.numpy as jnp
from jax import lax
from jax.experimental import pallas as pl
from jax.experimental.pallas import tpu as pltpu

F32 = jnp.float32
BF16 = jnp.bfloat16
HIGHEST = lax.Precision.HIGHEST

LANES = 128
DELTA_CHUNK = 64
ROPE_THETA = 10000.0
TOP_K = 2
MOE_BLOCK = 128
EPS = 1e-6
VMEM_LIMIT = 48 * 1024 * 1024


def _cparams(*sem):
    return pltpu.CompilerParams(dimension_semantics=sem, vmem_limit_bytes=VMEM_LIMIT)


def _row_tile(n, target, align=16):
    best = None
    for t in range(align, min(n, target) + 1, align):
        if n % t == 0:
            best = t
    assert best is not None, (n, target, align)
    return best


def _nt_dot(a, b, precision=None):
    return lax.dot_general(a, b, (((1,), (1,)), ((), ())), precision=precision,
                           preferred_element_type=F32)


def _tn_dot(a, b, precision=None):
    return lax.dot_general(a, b, (((0,), (0,)), ((), ())), precision=precision,
                           preferred_element_type=F32)


def _dot(a, b, precision=None):
    return jnp.dot(a, b, precision=precision, preferred_element_type=F32)


def _split_bf16(a):
    hi = a.astype(BF16)
    return hi, (a - hi.astype(F32)).astype(BF16)


def _dot_split(a, b):
    return _dot(a[0], b[0]) + (_dot(a[0], b[1]) + _dot(a[1], b[0]))


def _sigmoid(x):
    return 1.0 / (1.0 + jnp.exp(-x))


def _silu(x):
    return x * _sigmoid(x)


def _iota(shape, dim):
    return lax.broadcasted_iota(jnp.int32, shape, dim)


def _inproj_kernel(x_ref, g_ref, *rest, part_tiles):
    w_refs = rest[:len(part_tiles)]
    wba_ref, p_ref, ba_ref, h_scr = rest[len(part_tiles):]
    j = pl.program_id(1)

    @pl.when(j == 0)
    def _():
        x = x_ref[...]
        h = x * lax.rsqrt(jnp.mean(x * x, axis=-1, keepdims=True) + EPS) * g_ref[...]
        h_scr[...] = h.astype(BF16)
        ba_ref[...] = _dot(h_scr[...], wba_ref[...])

    lo = 0
    for w_ref, n_tiles in zip(w_refs, part_tiles):
        @pl.when((j >= lo) & (j < lo + n_tiles))
        def _(w_ref=w_ref):
            p_ref[...] = _dot(h_scr[...], w_ref[...])
        lo += n_tiles


def _inproj(x, gain, w_parts, w_ba):
    t, d = x.shape
    tm = _row_tile(t, 640)
    tn = 1024
    part_tiles = tuple(w.shape[1] // tn for w in w_parts)
    assert all(w.shape[1] % tn == 0 for w in w_parts)
    n_tiles = sum(part_tiles)
    starts = [sum(part_tiles[:k]) for k in range(len(w_parts))]

    def w_spec(k):
        return pl.BlockSpec((d, tn), lambda i, j: (0, jnp.clip(j - starts[k], 0, part_tiles[k] - 1)))

    return pl.pallas_call(
        functools.partial(_inproj_kernel, part_tiles=part_tiles),
        out_shape=(jax.ShapeDtypeStruct((t, n_tiles * tn), F32), jax.ShapeDtypeStruct((t, LANES), F32)),
        grid=(t // tm, n_tiles),
        in_specs=[pl.BlockSpec((tm, d), lambda i, j: (i, 0)),
                  pl.BlockSpec((1, d), lambda i, j: (0, 0))]
                 + [w_spec(k) for k in range(len(w_parts))]
                 + [pl.BlockSpec((d, LANES), lambda i, j: (0, 0))],
        out_specs=(pl.BlockSpec((tm, tn), lambda i, j: (i, j)),
                   pl.BlockSpec((tm, LANES), lambda i, j: (i, 0))),
        scratch_shapes=[pltpu.VMEM((tm, d), BF16)],
        compiler_params=_cparams("parallel", "arbitrary"),
        name="inproj",
    )(x, gain, *w_parts, w_ba)


def _delta_kernel(qkv_ref, z_ref, ba_ref, tail0_ref, s0_ref, cw_ref, alog_ref, dtb_ref, dn_ref,
                  o_ref, sfin_ref, ext_scr, s_scr, *, chunk, valid, heads, dk):
    n = pl.program_id(1)
    c = chunk
    hk = heads * dk

    @pl.when(n == 0)
    def _():
        ext_scr[0:8, :] = tail0_ref[0]
        s_scr[...] = s0_ref[0]

    ext_scr[8:8 + c, :] = qkv_ref[...]

    row = _iota((c, 1), 0)
    rowmask = (row < valid).astype(F32) if valid < c else None

    ba = ba_ref[...]
    beta_all = _sigmoid(ba)
    xg = ba + dtb_ref[...]
    softplus = jnp.maximum(xg, 0.0) + jnp.log1p(jnp.exp(-jnp.abs(xg)))
    g_all = -jnp.exp(alog_ref[...]) * softplus
    if rowmask is not None:
        beta_all = beta_all * rowmask
        g_all = g_all * rowmask
    r_i = _iota((c, c), 0)
    c_i = _iota((c, c), 1)
    incl = r_i >= c_i
    strict = r_i > c_i
    gc_all = _dot(incl.astype(F32), g_all, precision=HIGHEST)
    gc_t = jnp.transpose(gc_all)
    eye = (r_i == c_i).astype(F32)

    def conv(col0):
        sl = slice(col0, col0 + dk)
        acc = ext_scr[8:8 + c, sl] * cw_ref[3:4, sl]
        acc = acc + ext_scr[7:7 + c, sl] * cw_ref[2:3, sl]
        acc = acc + ext_scr[6:6 + c, sl] * cw_ref[1:2, sl]
        acc = acc + ext_scr[5:5 + c, sl] * cw_ref[0:1, sl]
        return _silu(acc)

    hs = range(heads)
    q, k, v, beta, gc, egc, g_last, decay = [], [], [], [], [], [], [], []
    for h in hs:
        qh = conv(h * dk)
        kh = conv(hk + h * dk)
        vh = conv(2 * hk + h * dk)
        qh = qh * lax.rsqrt(jnp.sum(qh * qh, axis=-1, keepdims=True) + EPS) * (dk ** -0.5)
        kh = kh * lax.rsqrt(jnp.sum(kh * kh, axis=-1, keepdims=True) + EPS)
        if rowmask is not None:
            qh, kh, vh = qh * rowmask, kh * rowmask, vh * rowmask
        q.append(qh)
        k.append(kh)
        v.append(vh)
        beta.append(beta_all[:, h:h + 1])
        gc.append(gc_all[:, heads + h:heads + h + 1])
        egc.append(jnp.exp(gc[h]))
        g_last.append(gc_all[c - 1:c, heads + h:heads + h + 1])
        gc_row = gc_t[heads + h:heads + h + 1, :]
        decay.append(jnp.where(incl, jnp.exp(jnp.where(incl, gc[h] - gc_row, 0.0)), 0.0))
    kb = [k[h] * beta[h] for h in hs]
    kk = [_nt_dot(kb[h], k[h]) for h in hs]
    qk = [_nt_dot(q[h], k[h]) for h in hs]
    apow = [jnp.where(strict, kk[h] * decay[h], 0.0) for h in hs]
    qk = [qk[h] * decay[h] for h in hs]
    a_mat = apow
    same_block = lambda b: (r_i // b) == (c_i // b)
    b0 = min(8, c)
    diag0 = same_block(b0)
    apb = [jnp.where(diag0, a_mat[h], 0.0) for h in hs]
    tinv = [eye - apb[h] for h in hs]
    apb = [apb[h].astype(BF16) for h in hs]
    span = 2
    while span < b0:
        apb = [_dot(apb[h], apb[h]).astype(BF16) for h in hs]
        tinv = [tinv[h] + _dot(tinv[h].astype(BF16), apb[h]) for h in hs]
        span *= 2
    b = b0
    while b < c:
        ring = same_block(2 * b) & ((r_i // b) != (c_i // b))
        off = [jnp.where(ring, a_mat[h], 0.0).astype(BF16) for h in hs]
        tb = [tinv[h].astype(BF16) for h in hs]
        tl = [_dot(tb[h], off[h]).astype(BF16) for h in hs]
        tinv = [tinv[h] - _dot(tl[h], tb[h]) for h in hs]
        b *= 2
    resid = [eye - _dot_split(_split_bf16(eye + a_mat[h]), _split_bf16(tinv[h])) for h in hs]
    tinv = [tinv[h] + _dot(tinv[h].astype(BF16), resid[h].astype(BF16)) for h in hs]
    u = [_dot(tinv[h], v[h] * beta[h]) for h in hs]
    w = [_dot(tinv[h], kb[h] * egc[h]) for h in hs]
    s = [s_scr[h] for h in hs]
    ws = [_dot(w[h], s[h]) for h in hs]
    qs = [_dot(q[h] * egc[h], s[h]) for h in hs]
    v_new = [u[h] - ws[h] for h in hs]
    o = [qs[h] + _dot(qk[h], v_new[h]) for h in hs]
    ds = [_tn_dot(k[h] * jnp.exp(g_last[h] - gc[h]), v_new[h]) for h in hs]
    for h in hs:
        s_scr[h] = s[h] * jnp.exp(g_last[h]) + ds[h]
        on = o[h] * lax.rsqrt(jnp.mean(o[h] * o[h], axis=-1, keepdims=True) + EPS) * dn_ref[...]
        zh = z_ref[:, h * dk:(h + 1) * dk]
        o_ref[:, h * dk:(h + 1) * dk] = (on * _silu(zh)).astype(o_ref.dtype)

    ext_scr[0:8, :] = ext_scr[c:c + 8, :]

    @pl.when(n == pl.num_programs(1) - 1)
    def _():
        sfin_ref[0] = s_scr[...]


def _delta_branch(proj, ba, row0, nb, seq, tail0, s0, conv_w, alog_row, dtb_row, delta_norm, *,
                  chunk, valid, heads, dk, z_col_block):
    hk = heads * dk
    nchunks = seq // chunk
    assert seq % chunk == 0 and row0 % chunk == 0
    r0 = row0 // chunk
    kern = functools.partial(_delta_kernel, chunk=chunk, valid=valid, heads=heads, dk=dk)
    return pl.pallas_call(
        kern,
        out_shape=(jax.ShapeDtypeStruct((nb * seq, hk), BF16),
                   jax.ShapeDtypeStruct((nb, heads, dk, dk), F32)),
        grid=(nb, nchunks),
        in_specs=[pl.BlockSpec((chunk, 3 * hk), lambda b, n: (r0 + b * nchunks + n, 0)),
                  pl.BlockSpec((chunk, hk), lambda b, n: (r0 + b * nchunks + n, z_col_block)),
                  pl.BlockSpec((chunk, LANES), lambda b, n: (r0 + b * nchunks + n, 0)),
                  pl.BlockSpec((1, 8, 3 * hk), lambda b, n: (b, 0, 0)),
                  pl.BlockSpec((1, heads, dk, dk), lambda b, n: (b, 0, 0, 0)),
                  pl.BlockSpec((4, 3 * hk), lambda b, n: (0, 0)),
                  pl.BlockSpec((1, LANES), lambda b, n: (0, 0)),
                  pl.BlockSpec((1, LANES), lambda b, n: (0, 0)),
                  pl.BlockSpec((1, dk), lambda b, n: (0, 0))],
        out_specs=(pl.BlockSpec((chunk, hk), lambda b, n: (b * nchunks + n, 0)),
                   pl.BlockSpec((1, heads, dk, dk), lambda b, n: (b, 0, 0, 0))),
        scratch_shapes=[pltpu.VMEM((8 + chunk, 3 * hk), F32),
                        pltpu.VMEM((heads, dk, dk), F32)],
        compiler_params=_cparams("parallel", "arbitrary"),
        name="delta_c%d" % chunk,
    )(proj, proj, ba, tail0, s0, conv_w, alog_row, dtb_row, delta_norm)


def _qkprep_kernel(q_ref, k_ref, v_ref, cos_ref, sin_ref, qg_ref, kg_ref, *out_refs, comp, scale, for_prompt, tk):
    tm = q_ref.shape[0]
    r_i = _iota((LANES, LANES), 0)
    c_i = _iota((LANES, LANES), 1)
    group = (r_i // comp == c_i // comp).astype(BF16)
    lane = _iota((tm, LANES), 1)
    first_half = (lane % comp) < (comp // 2)
    cos = cos_ref[...]
    sin = sin_ref[...]

    def norm_rope(x, gain):
        sq = x * x
        hi = sq.astype(BF16)
        lo = (sq - hi.astype(F32)).astype(BF16)
        ms = (_dot(hi, group) + _dot(lo, group)) * (1.0 / comp)
        y = x * lax.rsqrt(ms + EPS) * gain
        swapped = jnp.where(first_half, pltpu.roll(y, LANES - comp // 2, 1), pltpu.roll(y, comp // 2, 1))
        return y * cos + swapped * sin

    for j in range(q_ref.shape[1] // LANES):
        sl = slice(j * LANES, (j + 1) * LANES)
        qr = norm_rope(q_ref[:, sl], qg_ref[...]) * scale
        kr = norm_rope(k_ref[:, sl], kg_ref[...])
        if for_prompt:
            qt_ref, kbo_ref, vt_ref, kt_ref = out_refs
            qt_ref[0, sl, :] = jnp.transpose(qr).astype(BF16)
            kbo_ref[:, sl] = kr.astype(BF16)
            kt_ref[0, sl, :] = jnp.transpose(kr)
            vt = jnp.transpose(v_ref[:, sl]).astype(BF16)
            for u in range(tm // tk):
                vt_ref[0, u, sl, :] = vt[:, u * tk:(u + 1) * tk]
        else:
            qo_ref, kfo_ref = out_refs
            qo_ref[:, sl] = qr
            kfo_ref[:, sl] = kr


def _qkprep(proj, row0, nb, seq, cos_t, sin_t, q_gain, k_gain, *, q_col_block, width, comp, for_prompt, tk=None):
    tm = _row_tile(seq, 512, align=LANES) if for_prompt else seq
    assert row0 % tm == 0 and seq % tm == 0 and (not for_prompt or tm % tk == 0)
    r0, per_seq = row0 // tm, seq // tm
    scale = comp ** -0.5 * (math.log2(math.e) if for_prompt else 1.0)
    kern = functools.partial(_qkprep_kernel, comp=comp, scale=scale, for_prompt=for_prompt, tk=tk)
    col = lambda cb: pl.BlockSpec((tm, width), lambda i: (r0 + i, cb))
    small = pl.BlockSpec((tm, LANES), lambda i: (i % per_seq, 0))
    gain = pl.BlockSpec((1, LANES), lambda i: (0, 0))
    out = pl.BlockSpec((tm, width), lambda i: (i, 0))
    rows = nb * seq
    if for_prompt:
        out_t = pl.BlockSpec((1, width, tm), lambda i: (i // per_seq, 0, i % per_seq))
        out_shape = (jax.ShapeDtypeStruct((nb, width, seq), BF16), jax.ShapeDtypeStruct((rows, width), BF16),
                     jax.ShapeDtypeStruct((nb, seq // tk, width, tk), BF16),
                     jax.ShapeDtypeStruct((nb, width, seq), F32))
        out_specs = (out_t, out,
                     pl.BlockSpec((1, tm // tk, width, tk), lambda i: (i // per_seq, i % per_seq, 0, 0)), out_t)
    else:
        out_shape = (jax.ShapeDtypeStruct((rows, width), F32), jax.ShapeDtypeStruct((rows, width), F32))
        out_specs = (out, out)
    return pl.pallas_call(
        kern,
        out_shape=out_shape,
        grid=(rows // tm,),
        in_specs=[col(q_col_block), col(q_col_block + 1), col(q_col_block + 2), small, small, gain, gain],
        out_specs=out_specs,
        compiler_params=_cparams("parallel"),
        name="qkprep_prompt" if for_prompt else "qkprep_sample",
    )(proj, proj, proj, cos_t, sin_t, q_gain, k_gain)


def _lambda_value(lam_ref, lam_init):
    lv = lam_ref[...]
    s1 = jnp.sum(lv[0:1] * lv[1:2], axis=-1, keepdims=True)
    s2 = jnp.sum(lv[2:3] * lv[3:4], axis=-1, keepdims=True)
    return jnp.exp(s1) - jnp.exp(s2) + lam_init


def _attn_prompt_kernel(q_ref, k_ref, v_ref, lam_ref, sub_ref, o_ref, *, tq, tk, comp, heads_per_step, lam_init):
    i = pl.program_id(2)
    dv = 2 * comp
    hr = range(heads_per_step)
    row = _iota((dv, tq), 0)
    qs = []
    for hh in hr:
        qt = q_ref[0, hh * dv:(hh + 1) * dv, :]
        zero = jnp.zeros_like(qt)
        qs.append(jnp.concatenate([jnp.where(row < comp, qt, zero), jnp.where(row >= comp, qt, zero)], axis=1))

    def update(carry, j, mask):
        start = pl.multiple_of(j * tk, tk)
        kt = [k_ref[pl.ds(start, tk), hh * dv:(hh + 1) * dv] for hh in hr]
        vt = [v_ref[0, j, hh * dv:(hh + 1) * dv, :] for hh in hr]
        s = [_dot(kt[hh], qs[hh]) for hh in hr]
        if mask is not None:
            s = [jnp.where(mask, sn, -jnp.inf) for sn in s]
        m_new = [jnp.maximum(carry[hh][0], jnp.max(s[hh], axis=0, keepdims=True)) for hh in hr]
        p = [jnp.exp2(s[hh] - m_new[hh]) for hh in hr]
        corr = [jnp.exp2(carry[hh][0] - m_new[hh]) for hh in hr]
        l = [carry[hh][1] * corr[hh] + jnp.sum(p[hh], axis=0, keepdims=True) for hh in hr]
        pv = [_dot(vt[hh], p[hh].astype(BF16)) for hh in hr]
        return tuple((m_new[hh], l[hh], carry[hh][2] * corr[hh] + pv[hh]) for hh in hr)

    def body(j, carry):
        return update(carry, j, None)

    init = tuple((jnp.full((1, 2 * tq), -jnp.inf, F32), jnp.zeros((1, 2 * tq), F32),
                  jnp.zeros((dv, 2 * tq), F32)) for _ in hr)
    n_full = (i * tq) // tk
    carry = lax.fori_loop(0, n_full, body, init)
    kpos = n_full * tk + _iota((tk, 2 * tq), 0)
    qpos = i * tq + _iota((tk, 2 * tq), 1) % tq
    carry = update(carry, n_full, qpos >= kpos)
    lam = _lambda_value(lam_ref, lam_init)
    for hh in hr:
        _, l, a = carry[hh]
        n = a / l
        o = n[:, 0:tq] - lam * n[:, tq:2 * tq]
        on = o * lax.rsqrt(jnp.mean(o * o, axis=0, keepdims=True) + EPS) * sub_ref[...]
        o_ref[:, hh * dv:(hh + 1) * dv] = jnp.transpose(on * (1.0 - lam_init)).astype(o_ref.dtype)


def _attn_prompt(q_t, k, v_t, lam_vecs, subln_col, *, nb, seq, tk, heads, comp, lam_init):
    dv = 2 * comp
    tq = _row_tile(seq, 512, align=LANES)
    assert tk % tq == 0 and seq % tk == 0
    nq = seq // tq
    hps = 4
    assert heads % hps == 0
    kern = functools.partial(_attn_prompt_kernel, tq=tq, tk=tk, comp=comp, heads_per_step=hps, lam_init=lam_init)
    return pl.pallas_call(
        kern,
        out_shape=jax.ShapeDtypeStruct((nb * seq, heads * dv), BF16),
        grid=(nb, heads // hps, nq),
        in_specs=[pl.BlockSpec((1, hps * dv, tq), lambda b, h, i: (b, h, i)),
                  pl.BlockSpec((seq, hps * dv), lambda b, h, i: (b, h)),
                  pl.BlockSpec((1, seq // tk, hps * dv, tk), lambda b, h, i: (b, 0, h, 0)),
                  pl.BlockSpec((4, comp), lambda b, h, i: (0, 0)),
                  pl.BlockSpec((dv, 1), lambda b, h, i: (0, 0))],
        out_specs=pl.BlockSpec((tq, hps * dv), lambda b, h, i: (b * nq + i, h)),
        compiler_params=_cparams("parallel", "parallel", "arbitrary"),
        name="attn_prompt",
    )(q_t, k, v_t, lam_vecs, subln_col)


def _attn_sample_kernel(pt_ref, q_ref, kn_ref, vn_ref, lam_ref, sub_ref, *rest,
                        pages_per_step, new_len, heads, comp, lam_init):
    pp = pages_per_step
    kt_refs = rest[:pp]
    v_refs = rest[pp:2 * pp]
    o_ref = rest[2 * pp]
    qbd_scr, m_scr, l_scr, acc_scr = rest[2 * pp + 1:]
    p_id = pl.program_id(1)
    rph = 2 * new_len
    rows = heads * rph
    dv = 2 * comp
    width = heads * dv
    page = kt_refs[0].shape[2]

    @pl.when(p_id == 0)
    def _():
        q = q_ref[0][0:new_len, :]
        lane_grp = _iota((new_len, width), 1) // comp
        for h in range(heads):
            for c in range(2):
                r0 = h * rph + c * new_len
                qbd_scr[r0:r0 + new_len, :] = jnp.where(lane_grp == 2 * h + c, q, 0.0)
        m_scr[...] = jnp.full(m_scr.shape, -jnp.inf, F32)
        l_scr[...] = jnp.zeros(l_scr.shape, F32)
        acc_scr[...] = jnp.zeros(acc_scr.shape, F32)

    qbd = qbd_scr[...]

    def online(s_list, v_of):
        m = m_scr[...]
        m_new = m
        for s in s_list:
            m_new = jnp.maximum(m_new, jnp.max(s, axis=-1, keepdims=True))
        corr = jnp.exp(m - m_new)
        p_list = [jnp.exp(s - m_new) for s in s_list]
        l = l_scr[...] * corr
        for p in p_list:
            l = l + jnp.sum(p, axis=-1, keepdims=True)
        m_scr[...] = m_new
        l_scr[...] = l
        for h in range(heads):
            hs = slice(h * rph, (h + 1) * rph)
            acc = acc_scr[hs, :] * corr[hs, :]
            for j, p in enumerate(p_list):
                acc = acc + _dot(p[hs, :], v_of(j, h))
            acc_scr[hs, :] = acc

    online([_dot(qbd, kt_refs[r][0]) for r in range(pp)],
           lambda j, h: v_refs[j][0, pl.ds(h, page, stride=heads), :])

    @pl.when(p_id == pl.num_programs(1) - 1)
    def _():
        s = _nt_dot(qbd, kn_ref[0])
        tok = _iota((rows, 8), 0) % new_len
        key = _iota((rows, 8), 1)
        s = jnp.where((tok >= key) & (key < new_len), s, -jnp.inf)
        online([s], lambda j, h: vn_ref[0][:, h * dv:(h + 1) * dv])
        lam = _lambda_value(lam_ref, lam_init)
        nrm = acc_scr[...] / l_scr[...]
        for h in range(heads):
            r0 = h * rph
            oh = nrm[r0:r0 + new_len, :] - lam * nrm[r0 + new_len:r0 + rph, :]
            on = oh * lax.rsqrt(jnp.mean(oh * oh, axis=-1, keepdims=True) + EPS) * sub_ref[...]
            o_ref[0, 0:new_len, h * dv:(h + 1) * dv] = (on * (1.0 - lam_init)).astype(o_ref.dtype)
        if new_len < 8:
            o_ref[0, new_len:8, :] = jnp.zeros((8 - new_len, width), o_ref.dtype)


def _attn_sample(q8, k8, v8, cache_kt, cache_v2, page_table, lam_vecs, subln, *, new_len, heads, comp, lam_init):
    nb, n_pages = page_table.shape
    width, page = cache_kt.shape[1], cache_kt.shape[2]
    dv = 2 * comp
    pp = math.gcd(n_pages, 16)
    assert 2 * new_len == 8
    rows = heads * 2 * new_len
    kern = functools.partial(_attn_sample_kernel, pages_per_step=pp, new_len=new_len, heads=heads,
                             comp=comp, lam_init=lam_init)
    per_req = pl.BlockSpec((1, 8, width), lambda b, p, pt: (b, 0, 0))

    def page_spec(r, shape):
        return pl.BlockSpec((1,) + shape, lambda b, p, pt: (pt[b * n_pages + p * pp + r], 0, 0))

    grid_spec = pltpu.PrefetchScalarGridSpec(
        num_scalar_prefetch=1,
        grid=(nb, n_pages // pp),
        in_specs=[per_req, per_req, per_req,
                  pl.BlockSpec((4, comp), lambda b, p, pt: (0, 0)),
                  pl.BlockSpec((1, dv), lambda b, p, pt: (0, 0))]
                 + [page_spec(r, (width, page)) for r in range(pp)]
                 + [page_spec(r, (page * heads, dv)) for r in range(pp)],
        out_specs=pl.BlockSpec((1, 8, width), lambda b, p, pt: (b, 0, 0)),
        scratch_shapes=[pltpu.VMEM((rows, width), F32), pltpu.VMEM((rows, 1), F32),
                        pltpu.VMEM((rows, 1), F32), pltpu.VMEM((rows, dv), F32)],
    )
    return pl.pallas_call(
        kern,
        out_shape=jax.ShapeDtypeStruct((nb, 8, width), BF16),
        grid_spec=grid_spec,
        compiler_params=_cparams("parallel", "arbitrary"),
        name="attn_sample",
    )(page_table.reshape(-1), q8, k8, v8, lam_vecs, subln, *([cache_kt] * pp), *([cache_v2] * pp))


def _merge_kernel(oa_ref, ob_ref, ga_ref, gb_ref, woa_ref, wob_ref, m_ref):
    ua = _dot(oa_ref[...], woa_ref[...])
    ub = _dot(ob_ref[...], wob_ref[...])
    m_ref[...] = (_sigmoid(ga_ref[...]) * ua + _sigmoid(gb_ref[...]) * ub).astype(m_ref.dtype)


def _merge(o_a, o_b, proj, w_oa, w_ob, *, gate_col_block):
    t, zd = o_a.shape
    d = w_oa.shape[1]
    tm = _row_tile(t, 640)
    once = lambda shape: pl.BlockSpec(shape, lambda i: (0, 0), pipeline_mode=pl.Buffered(1))
    return pl.pallas_call(
        _merge_kernel,
        out_shape=jax.ShapeDtypeStruct((t, d), BF16),
        grid=(t // tm,),
        in_specs=[pl.BlockSpec((tm, zd), lambda i: (i, 0)),
                  pl.BlockSpec((tm, o_b.shape[1]), lambda i: (i, 0)),
                  pl.BlockSpec((tm, d), lambda i: (i, gate_col_block)),
                  pl.BlockSpec((tm, d), lambda i: (i, gate_col_block + 1)),
                  once(w_oa.shape), once(w_ob.shape)],
        out_specs=pl.BlockSpec((tm, d), lambda i: (i, 0)),
        compiler_params=_cparams("parallel"),
        name="merge",
    )(o_a, o_b, proj, proj, w_oa, w_ob)


def _outproj_route_kernel(x_ref, m_ref, wo_ref, g_ref, wr_ref, br_ref,
                          x1_ref, h2_ref, rw_ref, re_ref, cnt_ref, *, n_groups, per_group, parts):
    @pl.when(pl.program_id(0) == 0)
    def _():
        cnt_ref[...] = jnp.zeros(cnt_ref.shape, F32)

    tm = x_ref.shape[0] // parts
    nc = x_ref.shape[1] // LANES
    rows = [slice(k * tm, (k + 1) * tm) for k in range(parts)]
    x1s = [x_ref[sl, :] + _dot(m_ref[sl, :], wo_ref[...]) for sl in rows]
    lane = _iota((tm, LANES), 1).astype(F32)
    big = float(LANES)
    before = (_iota((tm, tm), 0) > _iota((tm, tm), 1)).astype(BF16)

    for k, sl in enumerate(rows):
        x1 = x1s[k]
        x1_ref[sl, :] = x1
        h2 = x1 * lax.rsqrt(jnp.mean(x1 * x1, axis=-1, keepdims=True) + EPS) * g_ref[...]
        for c in range(nc):
            h2_ref[pl.ds(k * tm * nc + c, tm, stride=nc), :] = h2[:, c * LANES:(c + 1) * LANES]
        logits = _dot(h2.astype(BF16), wr_ref[...]) + br_ref[...]

        def masked_softmax(mask):
            lm = jnp.where(mask, logits, -jnp.inf)
            e = jnp.exp(lm - jnp.max(lm, axis=-1, keepdims=True))
            return e / jnp.sum(e, axis=-1, keepdims=True)

        def top1(p, mask):
            v = jnp.max(jnp.where(mask, p, -1.0), axis=-1, keepdims=True)
            idx = jnp.min(jnp.where(mask & (p == v), lane, big), axis=-1, keepdims=True)
            return v, idx

        gmask = lane < n_groups
        pg = masked_softmax(gmask)
        g_w, g_top = top1(pg, gmask)
        lo = n_groups + g_top * per_group
        emask = (lane >= lo) & (lane < lo + per_group)
        pe = masked_softmax(emask)
        v1, i1 = top1(pe, emask)
        emask2 = emask & (lane != i1)
        v2, i2 = top1(pe, emask2)
        denom = v1 + v2
        w1 = g_w * v1 / denom
        w2 = g_w * v2 / denom
        rw_ref[sl, :] = jnp.where(lane == 0.0, w1, jnp.where(lane == 1.0, w2, 0.0))

        sel1 = lane == i1
        sel2 = lane == i2
        picked = jnp.where(sel1 | sel2, 1.0, 0.0)
        prior = _dot(before, picked.astype(BF16)) + cnt_ref[...]
        r1 = jnp.sum(jnp.where(sel1, prior, 0.0), axis=-1, keepdims=True)
        r2 = jnp.sum(jnp.where(sel2, prior, 0.0), axis=-1, keepdims=True)
        cnt_ref[...] = cnt_ref[...] + jnp.sum(picked, axis=0, keepdims=True)
        re_ref[sl, :] = jnp.where(lane == 0.0, i1 - n_groups,
                                  jnp.where(lane == 1.0, i2 - n_groups,
                                            jnp.where(lane == 2.0, r1,
                                                      jnp.where(lane == 3.0, r2, 0.0)))).astype(jnp.int32)


def _outproj_route(x, merged, w_out, gain, w_route, b_route, *, n_groups, per_group):
    t, d = x.shape
    parts = 2
    tm = _row_tile(t, 416, align=16 * parts)
    kern = functools.partial(_outproj_route_kernel, n_groups=n_groups, per_group=per_group, parts=parts)
    row = lambda w: pl.BlockSpec((tm, w), lambda i: (i, 0))
    once = lambda shape: pl.BlockSpec(shape, lambda i: (0, 0), pipeline_mode=pl.Buffered(1))
    return pl.pallas_call(
        kern,
        out_shape=(jax.ShapeDtypeStruct((t, d), F32), jax.ShapeDtypeStruct((t * (d // LANES), LANES), F32),
                   jax.ShapeDtypeStruct((t, LANES), F32), jax.ShapeDtypeStruct((t, LANES), jnp.int32),
                   jax.ShapeDtypeStruct((1, LANES), F32)),
        grid=(t // tm,),
        in_specs=[row(d), row(d), once(w_out.shape), once((1, d)), once((d, LANES)), once((1, LANES))],
        out_specs=(row(d), pl.BlockSpec((tm * (d // LANES), LANES), lambda i: (i, 0)), row(LANES), row(LANES),
                   pl.BlockSpec((1, LANES), lambda i: (0, 0))),
        compiler_params=_cparams("arbitrary"),
        name="outproj_route",
    )(x, merged, w_out, gain, w_route, b_route)


def _experts_kernel(be_ref, nused_ref, first_ref, wslot_ref, nexte_ref, src_ref, src_next_ref, dst_ref,
                    h_hbm, wg_hbm, wu_hbm, wd_hbm, y_hbm, xbuf, ybuf, wgbuf, wubuf, wdbuf, gsem, ssem, wsem):
    g = pl.program_id(0)
    nused = nused_ref[0]
    blk = MOE_BLOCK
    nc = wg_hbm.shape[1] // LANES

    def weight_copies(e, ws):
        return [pltpu.make_async_copy(hbm.at[e], buf.at[ws], wsem.at[ws])
                for hbm, buf in ((wg_hbm, wgbuf), (wu_hbm, wubuf), (wd_hbm, wdbuf))]

    def start_gather(idx_ref, slot):
        def body(r2, carry):
            for u in range(2):
                r = 2 * r2 + u
                row = pl.multiple_of(idx_ref[0, 0, r], nc)
                pltpu.make_async_copy(h_hbm.at[pl.ds(row, nc), :],
                                      xbuf.at[slot, pl.ds(pl.multiple_of(r * nc, nc), nc), :],
                                      gsem.at[slot]).start(priority=u)
            return carry
        lax.fori_loop(0, blk // 2, body, 0, unroll=4)

    def start_scatter(slot):
        def body(r2, carry):
            for u in range(2):
                r = 2 * r2 + u
                row = pl.multiple_of(dst_ref[0, 0, r], nc)
                pltpu.make_async_copy(ybuf.at[slot, pl.ds(pl.multiple_of(r * nc, nc), nc), :],
                                      y_hbm.at[pl.ds(row, nc), :], ssem.at[slot]).start(priority=u)
            return carry
        lax.fori_loop(0, blk // 2, body, 0, unroll=4)

    def wait_gather(slot):
        pltpu.make_async_copy(h_hbm.at[pl.ds(0, blk * nc), :], xbuf.at[slot], gsem.at[slot]).wait()

    def wait_scatter(slot):
        pltpu.make_async_copy(ybuf.at[slot], y_hbm.at[pl.ds(0, blk * nc), :], ssem.at[slot]).wait()

    @pl.when(g < nused)
    def _():
        slot = g % 2

        @pl.when(g == 0)
        def _():
            ybuf[1] = jnp.zeros(ybuf.shape[1:], F32)
            fill = pltpu.make_async_copy(ybuf.at[1], y_hbm.at[pl.ds(y_hbm.shape[0] - blk * nc, blk * nc), :],
                                         ssem.at[1])
            fill.start()
            fill.wait()
            start_gather(src_ref, slot)
            for cp in weight_copies(be_ref[0], wslot_ref[0]):
                cp.start()

        @pl.when(g + 1 < nused)
        def _():
            start_gather(src_next_ref, 1 - slot)

        ws = wslot_ref[g]

        @pl.when(first_ref[g] == 1)
        def _():
            @pl.when(nexte_ref[g] >= 0)
            def _():
                for cp in weight_copies(nexte_ref[g], 1 - ws):
                    cp.start()
            for cp in weight_copies(be_ref[g], ws):
                cp.wait()

        wait_gather(slot)
        x = jnp.concatenate([xbuf[slot, pl.ds(c, blk, stride=nc), :] for c in range(nc)], axis=1)
        hdn = _silu(_dot(x, wgbuf[ws])) * _dot(x, wubuf[ws])
        y = _dot(hdn, wdbuf[ws])
        for c in range(nc):
            ybuf[slot, pl.ds(c, blk, stride=nc), :] = y[:, c * LANES:(c + 1) * LANES]

        @pl.when(g > 0)
        def _():
            wait_scatter(1 - slot)

        start_scatter(slot)

        @pl.when(g == nused - 1)
        def _():
            wait_scatter(slot)


def _experts(h2, block_e, src_tok, dst_slot, nused, w_gate, w_up, w_down):
    d, ff = w_gate.shape[1], w_gate.shape[2]
    nc = d // LANES
    t = h2.shape[0] // nc
    n_blocks = block_e.shape[0]
    n_slots = t * TOP_K
    kern = _experts_kernel
    src3 = (src_tok * nc).reshape(n_blocks, 1, MOE_BLOCK)
    dst3 = (dst_slot * nc).reshape(n_blocks, 1, MOE_BLOCK)
    blk_id = jnp.arange(n_blocks, dtype=jnp.int32)
    used = blk_id < nused[0]
    first = used & ((blk_id == 0) | (block_e != jnp.roll(block_e, 1)))
    run_id = jnp.cumsum(first.astype(jnp.int32)) - 1
    wslot = (run_id % 2).astype(jnp.int32)
    run_expert = jnp.full((n_blocks + 1,), -1, jnp.int32).at[jnp.where(first, run_id, n_blocks)].set(block_e)
    next_e = run_expert[jnp.minimum(run_id + 1, n_blocks)]
    idx_spec = lambda f: pl.BlockSpec((1, 1, MOE_BLOCK), lambda g, *_: (f(g), 0, 0), memory_space=pltpu.SMEM)
    anyspec = pl.BlockSpec(memory_space=pl.ANY)
    grid_spec = pltpu.PrefetchScalarGridSpec(
        num_scalar_prefetch=5,
        grid=(n_blocks,),
        in_specs=[idx_spec(lambda g: g),
                  idx_spec(lambda g: jnp.minimum(g + 1, n_blocks - 1)),
                  idx_spec(lambda g: g),
                  anyspec, anyspec, anyspec, anyspec],
        out_specs=anyspec,
        scratch_shapes=[pltpu.VMEM((2, MOE_BLOCK * nc, LANES), F32), pltpu.VMEM((2, MOE_BLOCK * nc, LANES), F32),
                        pltpu.VMEM((2, d, ff), F32), pltpu.VMEM((2, d, ff), F32), pltpu.VMEM((2, ff, d), F32),
                        pltpu.SemaphoreType.DMA((2,)), pltpu.SemaphoreType.DMA((2,)), pltpu.SemaphoreType.DMA((2,))],
    )
    return pl.pallas_call(
        kern,
        out_shape=jax.ShapeDtypeStruct(((n_slots + MOE_BLOCK) * nc, LANES), F32),
        grid_spec=grid_spec,
        compiler_params=_cparams("arbitrary"),
        name="experts",
    )(block_e, nused, first.astype(jnp.int32), wslot, next_e, src3, src3, dst3, h2, w_gate, w_up, w_down)


def _expert_plan(eidx, rank, counts, tp):
    t = eidx.shape[0]
    n_experts = counts.shape[0]
    s = t * TOP_K
    padded = (counts + MOE_BLOCK - 1) // MOE_BLOCK * MOE_BLOCK
    ends = jnp.cumsum(padded)
    pad_start = ends - padded
    dest = (pad_start[eidx] + rank).reshape(-1)
    n_blocks = -(-(s + n_experts * (MOE_BLOCK - 1)) // MOE_BLOCK)
    rows = n_blocks * MOE_BLOCK
    slot_of_row = jnp.full((rows,), -1, jnp.int32).at[dest].set(jnp.arange(s, dtype=jnp.int32))
    valid = slot_of_row >= 0
    tok = slot_of_row // TOP_K
    src_tok = jnp.where(valid, tok, 0)
    kk = slot_of_row % TOP_K
    dst_real = jnp.where(tok < tp, kk * tp + tok, TOP_K * tp + kk * (t - tp) + (tok - tp))
    dst_slot = jnp.where(valid, dst_real, s + jnp.arange(rows, dtype=jnp.int32) % MOE_BLOCK)
    block_start = jnp.arange(n_blocks, dtype=jnp.int32) * MOE_BLOCK
    block_e = jnp.minimum(jnp.sum((ends[None, :] <= block_start[:, None]).astype(jnp.int32), axis=1), n_experts - 1)
    nused = (ends[-1:] // MOE_BLOCK).astype(jnp.int32)
    return block_e, src_tok, dst_slot, nused


def _combine_ple_kernel(x1_ref, ys0_ref, ys1_ref, rw_ref, p_ref, wp_ref, g_ref, wpg_ref, y_ref):
    rw = rw_ref[...]
    tm, d = x1_ref.shape
    nc = d // LANES

    def token_rows(ref):
        return jnp.concatenate([ref[pl.ds(c, tm, stride=nc), :] for c in range(nc)], axis=1)

    moe = rw[:, 0:1] * token_rows(ys0_ref) + rw[:, 1:2] * token_rows(ys1_ref)
    x2 = x1_ref[...] + moe
    hn = x2 * lax.rsqrt(jnp.mean(x2 * x2, axis=-1, keepdims=True) + EPS) * g_ref[...]
    gate = _sigmoid(_dot(hn.astype(BF16), wpg_ref[...]))
    y_ref[...] = x2 + _dot(p_ref[...], wp_ref[...]) * gate


def _combine_ple(x1, y_slots, route_w, p, w_ple, gain, w_ple_gate, *, row0, rows, slot_rows):
    t, d = x1.shape
    tm = _row_tile(math.gcd(math.gcd(rows, row0), math.gcd(*slot_rows)), 256)
    r0, s0, s1 = row0 // tm, slot_rows[0] // tm, slot_rows[1] // tm
    row = lambda w: pl.BlockSpec((tm, w), lambda i: (r0 + i, 0))
    once = lambda shape: pl.BlockSpec(shape, lambda i: (0, 0), pipeline_mode=pl.Buffered(1))
    return pl.pallas_call(
        _combine_ple_kernel,
        out_shape=jax.ShapeDtypeStruct((rows, d), F32),
        grid=(rows // tm,),
        in_specs=[row(d),
                  pl.BlockSpec((tm * (d // LANES), LANES), lambda i: (s0 + i, 0)),
                  pl.BlockSpec((tm * (d // LANES), LANES), lambda i: (s1 + i, 0)),
                  row(LANES), row(p.shape[1]),
                  once(w_ple.shape), once((1, d)), once(w_ple_gate.shape)],
        out_specs=pl.BlockSpec((tm, d), lambda i: (i, 0)),
        compiler_params=_cparams("parallel"),
        name="combine_ple_r%d" % row0,
    )(x1, y_slots, y_slots, route_w, p, w_ple, gain, w_ple_gate)


def _pad_rows(a, rows):
    return jnp.pad(a, ((0, 0), (0, rows - a.shape[1]), (0, 0)))


def _layer(x_all, p_all, nb_p, seq_p, nb_s, seq_s, past_len, state_conv, state_delta, cache_k, cache_v,
           page_table, lam_init, lw):
    tp = nb_p * seq_p
    ts = nb_s * seq_s
    d = x_all.shape[1]
    heads_d = lw['a_log'].shape[0]
    dk = lw['delta_norm'].shape[0]
    hk = heads_d * dk
    conv_dim = 3 * hk
    comp = lw['q_norm'].shape[0]
    dv = 2 * comp
    width = lw['w_ob'].shape[0]
    heads_a = width // dv
    n_groups = lw['w_rg'].shape[1]
    n_experts = lw['w_re'].shape[1]
    assert seq_s <= 8 and conv_dim % 1024 == 0 and hk == 1024 and width == 1024 and d % 1024 == 0
    assert 2 * heads_d <= LANES and n_groups + n_experts <= LANES and lw['conv_w'].shape[0] == 4

    w_in = lw['w_in']
    ba0 = conv_dim + hk
    q0 = ba0 + 2 * heads_d
    gate0 = q0 + 3 * width
    w_parts = (w_in[:, :ba0].astype(BF16), w_in[:, gate0:].astype(BF16), w_in[:, q0:gate0].astype(BF16))
    w_ba = jnp.pad(w_in[:, ba0:q0], ((0, 0), (0, LANES - 2 * heads_d))).astype(BF16)
    z_cb = conv_dim // hk
    gate_cb = (conv_dim + hk) // d
    q_cb = (conv_dim + hk + 2 * d) // width
    assert (conv_dim + hk) % d == 0 and (conv_dim + hk + 2 * d) % width == 0

    proj, ba = _inproj(x_all, lw['norm_attn'].reshape(1, d), w_parts, w_ba)

    alog_row = jnp.pad(lw['a_log'], (heads_d, LANES - 2 * heads_d)).reshape(1, LANES)
    dtb_row = jnp.pad(lw['dt_bias'], (heads_d, LANES - 2 * heads_d)).reshape(1, LANES)
    dn = lw['delta_norm'].reshape(1, dk)
    chunk_p = math.gcd(seq_p, DELTA_CHUNK)
    oa_p, s_p = _delta_branch(proj, ba, 0, nb_p, seq_p,
                              jnp.zeros((nb_p, 8, conv_dim), F32), jnp.zeros((nb_p, heads_d, dk, dk), F32),
                              lw['conv_w'], alog_row, dtb_row, dn,
                              chunk=chunk_p, valid=chunk_p, heads=heads_d, dk=dk, z_col_block=z_cb)
    qkv_s = proj[tp:, :conv_dim].reshape(nb_s, seq_s, conv_dim)
    proj_s8 = _pad_rows(proj[tp:].reshape(nb_s, seq_s, -1), 8).reshape(nb_s * 8, -1)
    ba_s8 = _pad_rows(ba[tp:].reshape(nb_s, seq_s, LANES), 8).reshape(nb_s * 8, LANES)
    tail_s = jnp.pad(state_conv, ((0, 0), (8 - state_conv.shape[1], 0), (0, 0)))
    oa_s8, s_s = _delta_branch(proj_s8, ba_s8, 0, nb_s, 8, tail_s, state_delta,
                               lw['conv_w'], alog_row, dtb_row, dn,
                               chunk=8, valid=seq_s, heads=heads_d, dk=dk, z_col_block=z_cb)
    oa_s = oa_s8.reshape(nb_s, 8, hk)[:, :seq_s].reshape(ts, hk)
    o_a = jnp.concatenate([oa_p, oa_s], axis=0)
    conv_p = jnp.stack([proj[(b + 1) * seq_p - 3:(b + 1) * seq_p, :conv_dim] for b in range(nb_p)])
    conv_s = jnp.concatenate([state_conv, qkv_s], axis=1)[:, -3:, :]

    half = comp // 2
    inv_freq = ROPE_THETA ** (-jnp.arange(half, dtype=F32) / half)
    def rope_tables(pos):
        ang = pos.astype(F32)[:, None] * inv_freq[None, :]
        sin_h = jnp.sin(ang)
        return (jnp.tile(jnp.cos(ang), (1, LANES // half)),
                jnp.tile(jnp.concatenate([-sin_h, sin_h], axis=1), (1, LANES // comp)))

    q_gain = jnp.tile(lw['q_norm'], LANES // comp).reshape(1, LANES)
    k_gain = jnp.tile(lw['k_norm'], LANES // comp).reshape(1, LANES)
    tk = _row_tile(seq_p, 512, align=LANES)
    q_t, k_bf, v_t, k_t = _qkprep(proj, 0, nb_p, seq_p, *rope_tables(jnp.arange(seq_p, dtype=jnp.int32)),
                                  q_gain, k_gain, q_col_block=q_cb, width=width, comp=comp, for_prompt=True, tk=tk)
    pos_s = jnp.tile(past_len + jnp.arange(seq_s, dtype=jnp.int32), nb_s)
    q_fs, k_fs = _qkprep(proj, tp, 1, ts, *rope_tables(pos_s), q_gain, k_gain,
                         q_col_block=q_cb, width=width, comp=comp, for_prompt=False)
    lam_vecs = jnp.stack([lw['lam_q1'], lw['lam_k1'], lw['lam_q2'], lw['lam_k2']])
    subln = lw['subln'].reshape(1, dv)
    ob_p = _attn_prompt(q_t, k_bf, v_t, lam_vecs, lw['subln'].reshape(dv, 1), nb=nb_p, seq=seq_p, tk=tk,
                        heads=heads_a, comp=comp, lam_init=lam_init)
    v_f_s = proj[tp:, (q_cb + 2) * width:(q_cb + 3) * width]
    q8 = _pad_rows(q_fs.reshape(nb_s, seq_s, width), 8)
    k8 = _pad_rows(k_fs.reshape(nb_s, seq_s, width), 8)
    v8 = _pad_rows(v_f_s.reshape(nb_s, seq_s, width), 8)
    n_phys, page = cache_k.shape[0], cache_k.shape[1]
    cache_kt = jnp.transpose(cache_k, (0, 2, 3, 4, 1)).reshape(n_phys, width, page)
    cache_v2 = cache_v.reshape(n_phys, page * heads_a, dv)
    ob_s8 = _attn_sample(q8, k8, v8, cache_kt, cache_v2, page_table, lam_vecs, subln,
                         new_len=seq_s, heads=heads_a, comp=comp, lam_init=lam_init)
    o_b = jnp.concatenate([ob_p, ob_s8[:, :seq_s].reshape(ts, width)], axis=0)

    merged = _merge(o_a, o_b, proj, lw['w_oa'].astype(BF16), lw['w_ob'].astype(BF16), gate_col_block=gate_cb)
    w_route = jnp.pad(jnp.concatenate([lw['w_rg'], lw['w_re']], axis=1),
                      ((0, 0), (0, LANES - n_groups - n_experts))).astype(BF16)
    b_route = jnp.pad(jnp.concatenate([lw['b_rg'], lw['b_re']]), (0, LANES - n_groups - n_experts)).reshape(1, LANES)
    x1, h2, route_w, route_i, route_cnt = _outproj_route(x_all, merged, lw['w_out'].astype(BF16),
                                                         lw['norm_ffn'].reshape(1, d), w_route, b_route,
                                                         n_groups=n_groups, per_group=n_experts // n_groups)

    counts = route_cnt[0, n_groups:n_groups + n_experts].astype(jnp.int32)
    block_e, src_tok, dst_slot, nused = _expert_plan(route_i[:, :TOP_K], route_i[:, TOP_K:2 * TOP_K], counts, tp)
    y_slots = _experts(h2, block_e, src_tok, dst_slot, nused, lw['exp_gate'], lw['exp_up'], lw['exp_down'])

    ple_args = (x1, y_slots, route_w, p_all.astype(BF16), lw['w_ple'].astype(BF16),
                lw['norm_ple'].reshape(1, d), lw['w_ple_gate'].astype(BF16))
    y_p = _combine_ple(*ple_args, row0=0, rows=tp, slot_rows=(0, tp))
    y_s = _combine_ple(*ple_args, row0=tp, rows=ts, slot_rows=(TOP_K * tp, TOP_K * tp + ts))
    y = (y_p, y_s)

    k_p = jnp.transpose(k_t.reshape(nb_p, heads_a, 2, comp, seq_p), (0, 4, 1, 2, 3))
    v_p = proj[:tp, (q_cb + 2) * width:(q_cb + 3) * width].reshape(nb_p, seq_p, heads_a, dv)
    k_s = k_fs.reshape(nb_s, seq_s, heads_a, 2, comp)
    v_s = v_f_s.reshape(nb_s, seq_s, heads_a, dv)
    return y, k_p, v_p, k_s, v_s, conv_p, conv_s, s_p, s_s


def kernel(x_prompt, x_sample, cache_k, cache_v, state_conv, state_delta, page_table, p_prompt, p_sample,
           norm_attn, w_in, conv_w, a_log, dt_bias, delta_norm, q_norm, k_norm, lam_q1, lam_k1, lam_q2, lam_k2,
           subln, w_oa, w_ob, w_out, norm_ffn, w_rg, b_rg, w_re, b_re, exp_gate, exp_up, exp_down,
           norm_ple, w_ple, w_ple_gate):
    nb_p, seq_p, d = x_prompt.shape
    nb_s, seq_s, _ = x_sample.shape
    tp = nb_p * seq_p
    depth = w_in.shape[0]
    past_len = page_table.shape[1] * cache_k.shape[2]
    y_p, y_s = x_prompt.reshape(tp, d), x_sample.reshape(nb_s * seq_s, d)
    outs = [[] for _ in range(8)]
    for i in range(depth):
        x_all = jnp.concatenate([y_p, y_s], axis=0)
        lw = dict(norm_attn=norm_attn[i], w_in=w_in[i], conv_w=conv_w[i], a_log=a_log[i], dt_bias=dt_bias[i],
                  delta_norm=delta_norm[i], q_norm=q_norm[i], k_norm=k_norm[i], lam_q1=lam_q1[i], lam_k1=lam_k1[i],
                  lam_q2=lam_q2[i], lam_k2=lam_k2[i], subln=subln[i], w_oa=w_oa[i], w_ob=w_ob[i], w_out=w_out[i],
                  norm_ffn=norm_ffn[i], w_rg=w_rg[i], b_rg=b_rg[i], w_re=w_re[i], b_re=b_re[i],
                  exp_gate=exp_gate[i], exp_up=exp_up[i], exp_down=exp_down[i],
                  norm_ple=norm_ple[i], w_ple=w_ple[i], w_ple_gate=w_ple_gate[i])
        lam_init = 0.8 - 0.6 * math.exp(-0.3 * i)
        p_all = jnp.concatenate([p_prompt[i].reshape(tp, -1), p_sample[i].reshape(nb_s * seq_s, -1)], axis=0)
        res = _layer(x_all, p_all, nb_p, seq_p, nb_s, seq_s, past_len, state_conv[i], state_delta[i],
                     cache_k[i], cache_v[i], page_table, lam_init, lw)
        y_p, y_s = res[0]
        for lst, val in zip(outs, res[1:]):
            lst.append(val)
    return (y_p.reshape(nb_p, seq_p, d), y_s.reshape(nb_s, seq_s, d)) + tuple(jnp.stack(lst) for lst in outs)
```

```python
import functools
import math

import jax
import jax.numpy as jnp
from jax import lax
from jax.experimental import pallas as pl
from jax.experimental.pallas import tpu as pltpu

F32 = jnp.float32
BF16 = jnp.bfloat16
HIGHEST = lax.Precision.HIGHEST

LANES = 128
DELTA_CHUNK = 64
ROPE_THETA = 10000.0
TOP_K = 2
MOE_BLOCK = 128
EPS = 1e-6
VMEM_LIMIT = 48 * 1024 * 1024
VMEM_LIMIT_LARGE = 56 * 1024 * 1024


def _cparams(*sem, vmem=VMEM_LIMIT):
    return pltpu.CompilerParams(dimension_semantics=sem, vmem_limit_bytes=vmem)


def _row_tile(n, target, align=16):
    best = None
    for t in range(align, min(n, target) + 1, align):
        if n % t == 0:
            best = t
    assert best is not None, (n, target, align)
    return best


def _nt_dot(a, b, precision=None):
    return lax.dot_general(a, b, (((1,), (1,)), ((), ())), precision=precision,
                           preferred_element_type=F32)


def _tn_dot(a, b, precision=None):
    return lax.dot_general(a, b, (((0,), (0,)), ((), ())), precision=precision,
                           preferred_element_type=F32)


def _dot(a, b, precision=None):
    return jnp.dot(a, b, precision=precision, preferred_element_type=F32)


def _split_bf16(a):
    hi = a.astype(BF16)
    return hi, (a - hi.astype(F32)).astype(BF16)


def _dot_split(a, b):
    return _dot(a[0], b[0]) + (_dot(a[0], b[1]) + _dot(a[1], b[0]))


def _sigmoid(x):
    return 1.0 / (1.0 + jnp.exp(-x))


def _silu(x):
    return x * _sigmoid(x)


def _iota(shape, dim):
    return lax.broadcasted_iota(jnp.int32, shape, dim)


def _inproj_kernel(x_ref, g_ref, *rest, part_tiles):
    w_refs = rest[:len(part_tiles)]
    wba_ref, p_ref, ba_ref, h_scr = rest[len(part_tiles):]
    j = pl.program_id(1)

    @pl.when(j == 0)
    def _():
        x = x_ref[...]
        h = x * lax.rsqrt(jnp.mean(x * x, axis=-1, keepdims=True) + EPS) * g_ref[...]
        h_scr[...] = h.astype(BF16)
        ba_ref[...] = _dot(h_scr[...], wba_ref[...])

    lo = 0
    for w_ref, n_tiles in zip(w_refs, part_tiles):
        @pl.when((j >= lo) & (j < lo + n_tiles))
        def _(w_ref=w_ref):
            p_ref[...] = _dot(h_scr[...], w_ref[...])
        lo += n_tiles


def _inproj(x, gain, w_parts, w_ba):
    t, d = x.shape
    tm = _row_tile(t, 832)
    tn = 1024
    part_tiles = tuple(w.shape[1] // tn for w in w_parts)
    assert all(w.shape[1] % tn == 0 for w in w_parts)
    n_tiles = sum(part_tiles)
    starts = [sum(part_tiles[:k]) for k in range(len(w_parts))]

    def w_spec(k):
        return pl.BlockSpec((d, tn), lambda i, j: (0, jnp.clip(j - starts[k], 0, part_tiles[k] - 1)))

    return pl.pallas_call(
        functools.partial(_inproj_kernel, part_tiles=part_tiles),
        out_shape=(jax.ShapeDtypeStruct((t, n_tiles * tn), F32), jax.ShapeDtypeStruct((t, LANES), F32)),
        grid=(t // tm, n_tiles),
        in_specs=[pl.BlockSpec((tm, d), lambda i, j: (i, 0)),
                  pl.BlockSpec((1, d), lambda i, j: (0, 0))]
                 + [w_spec(k) for k in range(len(w_parts))]
                 + [pl.BlockSpec((d, LANES), lambda i, j: (0, 0))],
        out_specs=(pl.BlockSpec((tm, tn), lambda i, j: (i, j)),
                   pl.BlockSpec((tm, LANES), lambda i, j: (i, 0))),
        scratch_shapes=[pltpu.VMEM((tm, d), BF16)],
        compiler_params=_cparams("parallel", "arbitrary", vmem=VMEM_LIMIT_LARGE),
        name="inproj",
    )(x, gain, *w_parts, w_ba)


def _delta_kernel(qkv_ref, z_ref, ba_ref, tail0_ref, s0_ref, cw_ref, alog_ref, dtb_ref, dn_ref,
                  o_ref, sfin_ref, ext_scr, s_scr, *, chunk, valid, heads, dk):
    n = pl.program_id(1)
    c = chunk
    hk = heads * dk

    @pl.when(n == 0)
    def _():
        ext_scr[0:8, :] = tail0_ref[0]
        s_scr[...] = s0_ref[0]

    ext_scr[8:8 + c, :] = qkv_ref[...]

    row = _iota((c, 1), 0)
    rowmask = (row < valid).astype(F32) if valid < c else None

    ba = ba_ref[...]
    beta_all = _sigmoid(ba)
    xg = ba + dtb_ref[...]
    softplus = jnp.maximum(xg, 0.0) + jnp.log1p(jnp.exp(-jnp.abs(xg)))
    g_all = -jnp.exp(alog_ref[...]) * softplus
    if rowmask is not None:
        beta_all = beta_all * rowmask
        g_all = g_all * rowmask
    r_i = _iota((c, c), 0)
    c_i = _iota((c, c), 1)
    incl = r_i >= c_i
    strict = r_i > c_i
    gc_all = _dot(incl.astype(F32), g_all, precision=HIGHEST)
    gc_t = jnp.transpose(gc_all)
    eye = (r_i == c_i).astype(F32)

    def conv(col0):
        sl = slice(col0, col0 + dk)
        acc = ext_scr[8:8 + c, sl] * cw_ref[3:4, sl]
        acc = acc + ext_scr[7:7 + c, sl] * cw_ref[2:3, sl]
        acc = acc + ext_scr[6:6 + c, sl] * cw_ref[1:2, sl]
        acc = acc + ext_scr[5:5 + c, sl] * cw_ref[0:1, sl]
        return _silu(acc)

    hs = range(heads)
    q, k, v, beta, gc, egc, g_last, decay = [], [], [], [], [], [], [], []
    for h in hs:
        qh = conv(h * dk)
        kh = conv(hk + h * dk)
        vh = conv(2 * hk + h * dk)
        qh = qh * lax.rsqrt(jnp.sum(qh * qh, axis=-1, keepdims=True) + EPS) * (dk ** -0.5)
        kh = kh * lax.rsqrt(jnp.sum(kh * kh, axis=-1, keepdims=True) + EPS)
        if rowmask is not None:
            qh, kh, vh = qh * rowmask, kh * rowmask, vh * rowmask
        q.append(qh)
        k.append(kh)
        v.append(vh)
        beta.append(beta_all[:, h:h + 1])
        gc.append(gc_all[:, heads + h:heads + h + 1])
        egc.append(jnp.exp(gc[h]))
        g_last.append(gc_all[c - 1:c, heads + h:heads + h + 1])
        gc_row = gc_t[heads + h:heads + h + 1, :]
        decay.append(jnp.where(incl, jnp.exp(jnp.where(incl, gc[h] - gc_row, 0.0)), 0.0))
    kb = [k[h] * beta[h] for h in hs]
    kk = [_nt_dot(kb[h], k[h]) for h in hs]
    qk = [_nt_dot(q[h], k[h]) for h in hs]
    apow = [jnp.where(strict, kk[h] * decay[h], 0.0) for h in hs]
    qk = [qk[h] * decay[h] for h in hs]
    a_mat = apow
    same_block = lambda b: (r_i // b) == (c_i // b)
    b0 = min(8, c)
    diag0 = same_block(b0)
    apb = [jnp.where(diag0, a_mat[h], 0.0) for h in hs]
    tinv = [eye - apb[h] for h in hs]
    apb = [apb[h].astype(BF16) for h in hs]
    span = 2
    while span < b0:
        apb = [_dot(apb[h], apb[h]).astype(BF16) for h in hs]
        tinv = [tinv[h] + _dot(tinv[h].astype(BF16), apb[h]) for h in hs]
        span *= 2
    b = b0
    while b < c:
        ring = same_block(2 * b) & ((r_i // b) != (c_i // b))
        off = [jnp.where(ring, a_mat[h], 0.0).astype(BF16) for h in hs]
        tb = [tinv[h].astype(BF16) for h in hs]
        tl = [_dot(tb[h], off[h]).astype(BF16) for h in hs]
        tinv = [tinv[h] - _dot(tl[h], tb[h]) for h in hs]
        b *= 2
    resid = [eye - _dot_split(_split_bf16(eye + a_mat[h]), _split_bf16(tinv[h])) for h in hs]
    tinv = [tinv[h] + _dot(tinv[h].astype(BF16), resid[h].astype(BF16)) for h in hs]
    u = [_dot(tinv[h], v[h] * beta[h]) for h in hs]
    w = [_dot(tinv[h], kb[h] * egc[h]) for h in hs]
    s = [s_scr[h] for h in hs]
    ws = [_dot(w[h], s[h]) for h in hs]
    qs = [_dot(q[h] * egc[h], s[h]) for h in hs]
    v_new = [u[h] - ws[h] for h in hs]
    o = [qs[h] + _dot(qk[h], v_new[h]) for h in hs]
    ds = [_tn_dot(k[h] * jnp.exp(g_last[h] - gc[h]), v_new[h]) for h in hs]
    for h in hs:
        s_scr[h] = s[h] * jnp.exp(g_last[h]) + ds[h]
        on = o[h] * lax.rsqrt(jnp.mean(o[h] * o[h], axis=-1, keepdims=True) + EPS) * dn_ref[...]
        zh = z_ref[:, h * dk:(h + 1) * dk]
        o_ref[:, h * dk:(h + 1) * dk] = (on * _silu(zh)).astype(o_ref.dtype)

    ext_scr[0:8, :] = ext_scr[c:c + 8, :]

    @pl.when(n == pl.num_programs(1) - 1)
    def _():
        sfin_ref[0] = s_scr[...]


def _delta_branch(proj, ba, row0, nb, seq, tail0, s0, conv_w, alog_row, dtb_row, delta_norm, *,
                  chunk, valid, heads, dk, z_col_block):
    hk = heads * dk
    nchunks = seq // chunk
    assert seq % chunk == 0 and row0 % chunk == 0
    r0 = row0 // chunk
    kern = functools.partial(_delta_kernel, chunk=chunk, valid=valid, heads=heads, dk=dk)
    return pl.pallas_call(
        kern,
        out_shape=(jax.ShapeDtypeStruct((nb * seq, hk), BF16),
                   jax.ShapeDtypeStruct((nb, heads, dk, dk), F32)),
        grid=(nb, nchunks),
        in_specs=[pl.BlockSpec((chunk, 3 * hk), lambda b, n: (r0 + b * nchunks + n, 0)),
                  pl.BlockSpec((chunk, hk), lambda b, n: (r0 + b * nchunks + n, z_col_block)),
                  pl.BlockSpec((chunk, LANES), lambda b, n: (r0 + b * nchunks + n, 0)),
                  pl.BlockSpec((1, 8, 3 * hk), lambda b, n: (b, 0, 0)),
                  pl.BlockSpec((1, heads, dk, dk), lambda b, n: (b, 0, 0, 0)),
                  pl.BlockSpec((4, 3 * hk), lambda b, n: (0, 0)),
                  pl.BlockSpec((1, LANES), lambda b, n: (0, 0)),
                  pl.BlockSpec((1, LANES), lambda b, n: (0, 0)),
                  pl.BlockSpec((1, dk), lambda b, n: (0, 0))],
        out_specs=(pl.BlockSpec((chunk, hk), lambda b, n: (b * nchunks + n, 0)),
                   pl.BlockSpec((1, heads, dk, dk), lambda b, n: (b, 0, 0, 0))),
        scratch_shapes=[pltpu.VMEM((8 + chunk, 3 * hk), F32),
                        pltpu.VMEM((heads, dk, dk), F32)],
        compiler_params=_cparams("parallel", "arbitrary"),
        name="delta_c%d" % chunk,
    )(proj, proj, ba, tail0, s0, conv_w, alog_row, dtb_row, delta_norm)


def _qkprep_kernel(q_ref, k_ref, v_ref, cos_ref, sin_ref, qg_ref, kg_ref, *out_refs, comp, scale, for_prompt, tk):
    tm = q_ref.shape[0]
    r_i = _iota((LANES, LANES), 0)
    c_i = _iota((LANES, LANES), 1)
    group = (r_i // comp == c_i // comp).astype(BF16)
    lane = _iota((tm, LANES), 1)
    first_half = (lane % comp) < (comp // 2)
    cos = cos_ref[...]
    sin = sin_ref[...]

    def norm_rope(x, gain):
        sq = x * x
        hi = sq.astype(BF16)
        lo = (sq - hi.astype(F32)).astype(BF16)
        ms = (_dot(hi, group) + _dot(lo, group)) * (1.0 / comp)
        y = x * lax.rsqrt(ms + EPS) * gain
        swapped = jnp.where(first_half, pltpu.roll(y, LANES - comp // 2, 1), pltpu.roll(y, comp // 2, 1))
        return y * cos + swapped * sin

    for j in range(q_ref.shape[1] // LANES):
        sl = slice(j * LANES, (j + 1) * LANES)
        qr = norm_rope(q_ref[:, sl], qg_ref[...]) * scale
        kr = norm_rope(k_ref[:, sl], kg_ref[...])
        if for_prompt:
            qt_ref, kbo_ref, vt_ref, kt_ref = out_refs
            qt_ref[0, sl, :] = jnp.transpose(qr).astype(BF16)
            kbo_ref[:, sl] = kr.astype(BF16)
            kt_ref[0, sl, :] = jnp.transpose(kr)
            vt = jnp.transpose(v_ref[:, sl]).astype(BF16)
            for u in range(tm // tk):
                vt_ref[0, u, sl, :] = vt[:, u * tk:(u + 1) * tk]
        else:
            qo_ref, kfo_ref = out_refs
            qo_ref[:, sl] = qr
            kfo_ref[:, sl] = kr


def _qkprep(proj, row0, nb, seq, cos_t, sin_t, q_gain, k_gain, *, q_col_block, width, comp, for_prompt, tk=None):
    tm = _row_tile(seq, 512, align=LANES) if for_prompt else seq
    assert row0 % tm == 0 and seq % tm == 0 and (not for_prompt or tm % tk == 0)
    r0, per_seq = row0 // tm, seq // tm
    scale = comp ** -0.5 * (math.log2(math.e) if for_prompt else 1.0)
    kern = functools.partial(_qkprep_kernel, comp=comp, scale=scale, for_prompt=for_prompt, tk=tk)
    col = lambda cb: pl.BlockSpec((tm, width), lambda i: (r0 + i, cb))
    small = pl.BlockSpec((tm, LANES), lambda i: (i % per_seq, 0))
    gain = pl.BlockSpec((1, LANES), lambda i: (0, 0))
    out = pl.BlockSpec((tm, width), lambda i: (i, 0))
    rows = nb * seq
    if for_prompt:
        out_t = pl.BlockSpec((1, width, tm), lambda i: (i // per_seq, 0, i % per_seq))
        out_shape = (jax.ShapeDtypeStruct((nb, width, seq), BF16), jax.ShapeDtypeStruct((rows, width), BF16),
                     jax.ShapeDtypeStruct((nb, seq // tk, width, tk), BF16),
                     jax.ShapeDtypeStruct((nb, width, seq), F32))
        out_specs = (out_t, out,
                     pl.BlockSpec((1, tm // tk, width, tk), lambda i: (i // per_seq, i % per_seq, 0, 0)), out_t)
    else:
        out_shape = (jax.ShapeDtypeStruct((rows, width), F32), jax.ShapeDtypeStruct((rows, width), F32))
        out_specs = (out, out)
    return pl.pallas_call(
        kern,
        out_shape=out_shape,
        grid=(rows // tm,),
        in_specs=[col(q_col_block), col(q_col_block + 1), col(q_col_block + 2), small, small, gain, gain],
        out_specs=out_specs,
        compiler_params=_cparams("parallel"),
        name="qkprep_prompt" if for_prompt else "qkprep_sample",
    )(proj, proj, proj, cos_t, sin_t, q_gain, k_gain)


def _lambda_value(lam_ref, lam_init):
    lv = lam_ref[...]
    s1 = jnp.sum(lv[0:1] * lv[1:2], axis=-1, keepdims=True)
    s2 = jnp.sum(lv[2:3] * lv[3:4], axis=-1, keepdims=True)
    return jnp.exp(s1) - jnp.exp(s2) + lam_init


def _attn_prompt_kernel(q_ref, k_ref, v_ref, lam_ref, sub_ref, o_ref, *, tq, tk, comp, heads_per_step, lam_init):
    i = pl.program_id(2)
    dv = 2 * comp
    hr = range(heads_per_step)
    row = _iota((dv, tq), 0)
    qs = []
    for hh in hr:
        qt = q_ref[0, hh * dv:(hh + 1) * dv, :]
        zero = jnp.zeros_like(qt)
        qs.append(jnp.concatenate([jnp.where(row < comp, qt, zero), jnp.where(row >= comp, qt, zero)], axis=1))

    def update(carry, j, mask):
        start = pl.multiple_of(j * tk, tk)
        kt = [k_ref[pl.ds(start, tk), hh * dv:(hh + 1) * dv] for hh in hr]
        vt = [v_ref[0, j, hh * dv:(hh + 1) * dv, :] for hh in hr]
        s = [_dot(kt[hh], qs[hh]) for hh in hr]
        if mask is not None:
            s = [jnp.where(mask, sn, -jnp.inf) for sn in s]
        m_new = [jnp.maximum(carry[hh][0], jnp.max(s[hh], axis=0, keepdims=True)) for hh in hr]
        p = [jnp.exp2(s[hh] - m_new[hh]) for hh in hr]
        corr = [jnp.exp2(carry[hh][0] - m_new[hh]) for hh in hr]
        l = [carry[hh][1] * corr[hh] + jnp.sum(p[hh], axis=0, keepdims=True) for hh in hr]
        pv = [_dot(vt[hh], p[hh].astype(BF16)) for hh in hr]
        return tuple((m_new[hh], l[hh], carry[hh][2] * corr[hh] + pv[hh]) for hh in hr)

    def body(j, carry):
        return update(carry, j, None)

    init = tuple((jnp.full((1, 2 * tq), -jnp.inf, F32), jnp.zeros((1, 2 * tq), F32),
                  jnp.zeros((dv, 2 * tq), F32)) for _ in hr)
    n_full = (i * tq) // tk
    carry = lax.fori_loop(0, n_full, body, init)
    kpos = n_full * tk + _iota((tk, 2 * tq), 0)
    qpos = i * tq + _iota((tk, 2 * tq), 1) % tq
    carry = update(carry, n_full, qpos >= kpos)
    lam = _lambda_value(lam_ref, lam_init)
    for hh in hr:
        _, l, a = carry[hh]
        n = a / l
        o = n[:, 0:tq] - lam * n[:, tq:2 * tq]
        on = o * lax.rsqrt(jnp.mean(o * o, axis=0, keepdims=True) + EPS) * sub_ref[...]
        o_ref[:, hh * dv:(hh + 1) * dv] = jnp.transpose(on * (1.0 - lam_init)).astype(o_ref.dtype)


def _attn_prompt(q_t, k, v_t, lam_vecs, subln_col, *, nb, seq, tk, heads, comp, lam_init):
    dv = 2 * comp
    tq = _row_tile(seq, 512, align=LANES)
    assert tk % tq == 0 and seq % tk == 0
    nq = seq // tq
    hps = 4
    assert heads % hps == 0
    kern = functools.partial(_attn_prompt_kernel, tq=tq, tk=tk, comp=comp, heads_per_step=hps, lam_init=lam_init)
    return pl.pallas_call(
        kern,
        out_shape=jax.ShapeDtypeStruct((nb * seq, heads * dv), BF16),
        grid=(nb, heads // hps, nq),
        in_specs=[pl.BlockSpec((1, hps * dv, tq), lambda b, h, i: (b, h, i)),
                  pl.BlockSpec((seq, hps * dv), lambda b, h, i: (b, h)),
                  pl.BlockSpec((1, seq // tk, hps * dv, tk), lambda b, h, i: (b, 0, h, 0)),
                  pl.BlockSpec((4, comp), lambda b, h, i: (0, 0)),
                  pl.BlockSpec((dv, 1), lambda b, h, i: (0, 0))],
        out_specs=pl.BlockSpec((tq, hps * dv), lambda b, h, i: (b * nq + i, h)),
        compiler_params=_cparams("parallel", "parallel", "arbitrary"),
        name="attn_prompt",
    )(q_t, k, v_t, lam_vecs, subln_col)


def _attn_sample_kernel(pt_ref, q_ref, kn_ref, vn_ref, lam_ref, sub_ref, *rest,
                        pages_per_step, new_len, heads, comp, lam_init):
    pp = pages_per_step
    kt_refs = rest[:pp]
    v_refs = rest[pp:2 * pp]
    o_ref = rest[2 * pp]
    qbd_scr, m_scr, l_scr, acc_scr = rest[2 * pp + 1:]
    p_id = pl.program_id(1)
    rph = 2 * new_len
    rows = heads * rph
    dv = 2 * comp
    width = heads * dv
    page = kt_refs[0].shape[2]

    @pl.when(p_id == 0)
    def _():
        q = q_ref[0][0:new_len, :]
        lane_grp = _iota((new_len, width), 1) // comp
        for h in range(heads):
            for c in range(2):
                r0 = h * rph + c * new_len
                qbd_scr[r0:r0 + new_len, :] = jnp.where(lane_grp == 2 * h + c, q, 0.0)
        m_scr[...] = jnp.full(m_scr.shape, -jnp.inf, F32)
        l_scr[...] = jnp.zeros(l_scr.shape, F32)
        acc_scr[...] = jnp.zeros(acc_scr.shape, F32)

    qbd = qbd_scr[...]

    def online(s_list, v_of):
        m = m_scr[...]
        m_new = m
        for s in s_list:
            m_new = jnp.maximum(m_new, jnp.max(s, axis=-1, keepdims=True))
        corr = jnp.exp(m - m_new)
        p_list = [jnp.exp(s - m_new) for s in s_list]
        l = l_scr[...] * corr
        for p in p_list:
            l = l + jnp.sum(p, axis=-1, keepdims=True)
        m_scr[...] = m_new
        l_scr[...] = l
        for h in range(heads):
            hs = slice(h * rph, (h + 1) * rph)
            acc = acc_scr[hs, :] * corr[hs, :]
            for j, p in enumerate(p_list):
                acc = acc + _dot(p[hs, :], v_of(j, h))
            acc_scr[hs, :] = acc

    online([_dot(qbd, kt_refs[r][0]) for r in range(pp)],
           lambda j, h: v_refs[j][0, pl.ds(h, page, stride=heads), :])

    @pl.when(p_id == pl.num_programs(1) - 1)
    def _():
        s = _nt_dot(qbd, kn_ref[0])
        tok = _iota((rows, 8), 0) % new_len
        key = _iota((rows, 8), 1)
        s = jnp.where((tok >= key) & (key < new_len), s, -jnp.inf)
        online([s], lambda j, h: vn_ref[0][:, h * dv:(h + 1) * dv])
        lam = _lambda_value(lam_ref, lam_init)
        nrm = acc_scr[...] / l_scr[...]
        for h in range(heads):
            r0 = h * rph
            oh = nrm[r0:r0 + new_len, :] - lam * nrm[r0 + new_len:r0 + rph, :]
            on = oh * lax.rsqrt(jnp.mean(oh * oh, axis=-1, keepdims=True) + EPS) * sub_ref[...]
            o_ref[0, 0:new_len, h * dv:(h + 1) * dv] = (on * (1.0 - lam_init)).astype(o_ref.dtype)
        if new_len < 8:
            o_ref[0, new_len:8, :] = jnp.zeros((8 - new_len, width), o_ref.dtype)


def _attn_sample(q8, k8, v8, cache_kt, cache_v2, page_table, lam_vecs, subln, *, new_len, heads, comp, lam_init):
    nb, n_pages = page_table.shape
    width, page = cache_kt.shape[1], cache_kt.shape[2]
    dv = 2 * comp
    pp = math.gcd(n_pages, 16)
    assert 2 * new_len == 8
    rows = heads * 2 * new_len
    kern = functools.partial(_attn_sample_kernel, pages_per_step=pp, new_len=new_len, heads=heads,
                             comp=comp, lam_init=lam_init)
    per_req = pl.BlockSpec((1, 8, width), lambda b, p, pt: (b, 0, 0))

    def page_spec(r, shape):
        return pl.BlockSpec((1,) + shape, lambda b, p, pt: (pt[b * n_pages + p * pp + r], 0, 0))

    grid_spec = pltpu.PrefetchScalarGridSpec(
        num_scalar_prefetch=1,
        grid=(nb, n_pages // pp),
        in_specs=[per_req, per_req, per_req,
                  pl.BlockSpec((4, comp), lambda b, p, pt: (0, 0)),
                  pl.BlockSpec((1, dv), lambda b, p, pt: (0, 0))]
                 + [page_spec(r, (width, page)) for r in range(pp)]
                 + [page_spec(r, (page * heads, dv)) for r in range(pp)],
        out_specs=pl.BlockSpec((1, 8, width), lambda b, p, pt: (b, 0, 0)),
        scratch_shapes=[pltpu.VMEM((rows, width), F32), pltpu.VMEM((rows, 1), F32),
                        pltpu.VMEM((rows, 1), F32), pltpu.VMEM((rows, dv), F32)],
    )
    return pl.pallas_call(
        kern,
        out_shape=jax.ShapeDtypeStruct((nb, 8, width), BF16),
        grid_spec=grid_spec,
        compiler_params=_cparams("parallel", "arbitrary"),
        name="attn_sample",
    )(page_table.reshape(-1), q8, k8, v8, lam_vecs, subln, *([cache_kt] * pp), *([cache_v2] * pp))


def _merge_kernel(oa_ref, ob_ref, ga_ref, gb_ref, woa_ref, wob_ref, m_ref):
    ua = _dot(oa_ref[...], woa_ref[...])
    ub = _dot(ob_ref[...], wob_ref[...])
    m_ref[...] = (_sigmoid(ga_ref[...]) * ua + _sigmoid(gb_ref[...]) * ub).astype(m_ref.dtype)


def _merge(o_a, o_b, proj, w_oa, w_ob, *, gate_col_block):
    t, zd = o_a.shape
    d = w_oa.shape[1]
    tm = _row_tile(t, 640)
    once = lambda shape: pl.BlockSpec(shape, lambda i: (0, 0), pipeline_mode=pl.Buffered(1))
    return pl.pallas_call(
        _merge_kernel,
        out_shape=jax.ShapeDtypeStruct((t, d), BF16),
        grid=(t // tm,),
        in_specs=[pl.BlockSpec((tm, zd), lambda i: (i, 0)),
                  pl.BlockSpec((tm, o_b.shape[1]), lambda i: (i, 0)),
                  pl.BlockSpec((tm, d), lambda i: (i, gate_col_block)),
                  pl.BlockSpec((tm, d), lambda i: (i, gate_col_block + 1)),
                  once(w_oa.shape), once(w_ob.shape)],
        out_specs=pl.BlockSpec((tm, d), lambda i: (i, 0)),
        compiler_params=_cparams("parallel"),
        name="merge",
    )(o_a, o_b, proj, proj, w_oa, w_ob)


def _outproj_route_kernel(x_ref, m_ref, wo_ref, g_ref, wr_ref, br_ref,
                          x1_ref, h2_ref, rw_ref, re_ref, cnt_ref, *, n_groups, per_group, parts):
    @pl.when(pl.program_id(0) == 0)
    def _():
        cnt_ref[...] = jnp.zeros(cnt_ref.shape, F32)

    tm = x_ref.shape[0] // parts
    nc = x_ref.shape[1] // LANES
    rows = [slice(k * tm, (k + 1) * tm) for k in range(parts)]
    x1s = [x_ref[sl, :] + _dot(m_ref[sl, :], wo_ref[...]) for sl in rows]
    lane = _iota((tm, LANES), 1).astype(F32)
    big = float(LANES)
    before = (_iota((tm, tm), 0) > _iota((tm, tm), 1)).astype(BF16)

    for k, sl in enumerate(rows):
        x1 = x1s[k]
        x1_ref[sl, :] = x1
        h2 = x1 * lax.rsqrt(jnp.mean(x1 * x1, axis=-1, keepdims=True) + EPS) * g_ref[...]
        for c in range(nc):
            h2_ref[pl.ds(k * tm * nc + c, tm, stride=nc), :] = h2[:, c * LANES:(c + 1) * LANES]
        logits = _dot(h2.astype(BF16), wr_ref[...]) + br_ref[...]

        def masked_softmax(mask):
            lm = jnp.where(mask, logits, -jnp.inf)
            e = jnp.exp(lm - jnp.max(lm, axis=-1, keepdims=True))
            return e / jnp.sum(e, axis=-1, keepdims=True)

        def top1(p, mask):
            v = jnp.max(jnp.where(mask, p, -1.0), axis=-1, keepdims=True)
            idx = jnp.min(jnp.where(mask & (p == v), lane, big), axis=-1, keepdims=True)
            return v, idx

        gmask = lane < n_groups
        pg = masked_softmax(gmask)
        g_w, g_top = top1(pg, gmask)
        lo = n_groups + g_top * per_group
        emask = (lane >= lo) & (lane < lo + per_group)
        pe = masked_softmax(emask)
        v1, i1 = top1(pe, emask)
        emask2 = emask & (lane != i1)
        v2, i2 = top1(pe, emask2)
        denom = v1 + v2
        w1 = g_w * v1 / denom
        w2 = g_w * v2 / denom
        rw_ref[sl, :] = jnp.where(lane == 0.0, w1, jnp.where(lane == 1.0, w2, 0.0))

        sel1 = lane == i1
        sel2 = lane == i2
        picked = jnp.where(sel1 | sel2, 1.0, 0.0)
        prior = _dot(before, picked.astype(BF16)) + cnt_ref[...]
        r1 = jnp.sum(jnp.where(sel1, prior, 0.0), axis=-1, keepdims=True)
        r2 = jnp.sum(jnp.where(sel2, prior, 0.0), axis=-1, keepdims=True)
        cnt_ref[...] = cnt_ref[...] + jnp.sum(picked, axis=0, keepdims=True)
        re_ref[sl, :] = jnp.where(lane == 0.0, i1 - n_groups,
                                  jnp.where(lane == 1.0, i2 - n_groups,
                                            jnp.where(lane == 2.0, r1,
                                                      jnp.where(lane == 3.0, r2, 0.0)))).astype(jnp.int32)


def _outproj_route(x, merged, w_out, gain, w_route, b_route, *, n_groups, per_group):
    t, d = x.shape
    parts = 2
    tm = _row_tile(t, 416, align=16 * parts)
    kern = functools.partial(_outproj_route_kernel, n_groups=n_groups, per_group=per_group, parts=parts)
    row = lambda w: pl.BlockSpec((tm, w), lambda i: (i, 0))
    once = lambda shape: pl.BlockSpec(shape, lambda i: (0, 0), pipeline_mode=pl.Buffered(1))
    return pl.pallas_call(
        kern,
        out_shape=(jax.ShapeDtypeStruct((t, d), F32), jax.ShapeDtypeStruct((t * (d // LANES), LANES), F32),
                   jax.ShapeDtypeStruct((t, LANES), F32), jax.ShapeDtypeStruct((t, LANES), jnp.int32),
                   jax.ShapeDtypeStruct((1, LANES), F32)),
        grid=(t // tm,),
        in_specs=[row(d), row(d), once(w_out.shape), once((1, d)), once((d, LANES)), once((1, LANES))],
        out_specs=(row(d), pl.BlockSpec((tm * (d // LANES), LANES), lambda i: (i, 0)), row(LANES), row(LANES),
                   pl.BlockSpec((1, LANES), lambda i: (0, 0))),
        compiler_params=_cparams("arbitrary"),
        name="outproj_route",
    )(x, merged, w_out, gain, w_route, b_route)


def _experts_kernel(be_ref, nused_ref, first_ref, wslot_ref, nexte_ref, src_ref, src_next_ref, dst_ref,
                    h_hbm, wg_hbm, wu_hbm, wd_hbm, y_hbm, xbuf, ybuf, wgbuf, wubuf, wdbuf, gsem, ssem, wsem):
    g = pl.program_id(0)
    nused = nused_ref[0]
    blk = MOE_BLOCK
    nc = wg_hbm.shape[1] // LANES

    def weight_copies(e, ws):
        return [pltpu.make_async_copy(hbm.at[e], buf.at[ws], wsem.at[ws])
                for hbm, buf in ((wg_hbm, wgbuf), (wu_hbm, wubuf), (wd_hbm, wdbuf))]

    def start_gather(idx_ref, slot):
        def body(r2, carry):
            for u in range(2):
                r = 2 * r2 + u
                row = pl.multiple_of(idx_ref[0, 0, r], nc)
                pltpu.make_async_copy(h_hbm.at[pl.ds(row, nc), :],
                                      xbuf.at[slot, pl.ds(pl.multiple_of(r * nc, nc), nc), :],
                                      gsem.at[slot]).start(priority=u)
            return carry
        lax.fori_loop(0, blk // 2, body, 0, unroll=4)

    def start_scatter(slot):
        def body(r2, carry):
            for u in range(2):
                r = 2 * r2 + u
                row = pl.multiple_of(dst_ref[0, 0, r], nc)
                pltpu.make_async_copy(ybuf.at[slot, pl.ds(pl.multiple_of(r * nc, nc), nc), :],
                                      y_hbm.at[pl.ds(row, nc), :], ssem.at[slot]).start(priority=u)
            return carry
        lax.fori_loop(0, blk // 2, body, 0, unroll=4)

    def wait_gather(slot):
        pltpu.make_async_copy(h_hbm.at[pl.ds(0, blk * nc), :], xbuf.at[slot], gsem.at[slot]).wait()

    def wait_scatter(slot):
        pltpu.make_async_copy(ybuf.at[slot], y_hbm.at[pl.ds(0, blk * nc), :], ssem.at[slot]).wait()

    @pl.when(g < nused)
    def _():
        slot = g % 2

        @pl.when(g == 0)
        def _():
            ybuf[1] = jnp.zeros(ybuf.shape[1:], F32)
            fill = pltpu.make_async_copy(ybuf.at[1], y_hbm.at[pl.ds(y_hbm.shape[0] - blk * nc, blk * nc), :],
                                         ssem.at[1])
            fill.start()
            fill.wait()
            start_gather(src_ref, slot)
            for cp in weight_copies(be_ref[0], wslot_ref[0]):
                cp.start()

        @pl.when(g + 1 < nused)
        def _():
            start_gather(src_next_ref, 1 - slot)

        ws = wslot_ref[g]

        @pl.when(first_ref[g] == 1)
        def _():
            @pl.when(nexte_ref[g] >= 0)
            def _():
                for cp in weight_copies(nexte_ref[g], 1 - ws):
                    cp.start()
            for cp in weight_copies(be_ref[g], ws):
                cp.wait()

        wait_gather(slot)
        x = jnp.concatenate([xbuf[slot, pl.ds(c, blk, stride=nc), :] for c in range(nc)], axis=1)
        hdn = _silu(_dot(x, wgbuf[ws])) * _dot(x, wubuf[ws])
        y = _dot(hdn, wdbuf[ws])
        for c in range(nc):
            ybuf[slot, pl.ds(c, blk, stride=nc), :] = y[:, c * LANES:(c + 1) * LANES]

        @pl.when(g > 0)
        def _():
            wait_scatter(1 - slot)

        start_scatter(slot)

        @pl.when(g == nused - 1)
        def _():
            wait_scatter(slot)


def _experts(h2, block_e, src_tok, dst_slot, nused, w_gate, w_up, w_down):
    d, ff = w_gate.shape[1], w_gate.shape[2]
    nc = d // LANES
    t = h2.shape[0] // nc
    n_blocks = block_e.shape[0]
    n_slots = t * TOP_K
    kern = _experts_kernel
    src3 = (src_tok * nc).reshape(n_blocks, 1, MOE_BLOCK)
    dst3 = (dst_slot * nc).reshape(n_blocks, 1, MOE_BLOCK)
    blk_id = jnp.arange(n_blocks, dtype=jnp.int32)
    used = blk_id < nused[0]
    first = used & ((blk_id == 0) | (block_e != jnp.roll(block_e, 1)))
    run_id = jnp.cumsum(first.astype(jnp.int32)) - 1
    wslot = (run_id % 2).astype(jnp.int32)
    run_expert = jnp.full((n_blocks + 1,), -1, jnp.int32).at[jnp.where(first, run_id, n_blocks)].set(block_e)
    next_e = run_expert[jnp.minimum(run_id + 1, n_blocks)]
    idx_spec = lambda f: pl.BlockSpec((1, 1, MOE_BLOCK), lambda g, *_: (f(g), 0, 0), memory_space=pltpu.SMEM)
    anyspec = pl.BlockSpec(memory_space=pl.ANY)
    grid_spec = pltpu.PrefetchScalarGridSpec(
        num_scalar_prefetch=5,
        grid=(n_blocks,),
        in_specs=[idx_spec(lambda g: g),
                  idx_spec(lambda g: jnp.minimum(g + 1, n_blocks - 1)),
                  idx_spec(lambda g: g),
                  anyspec, anyspec, anyspec, anyspec],
        out_specs=anyspec,
        scratch_shapes=[pltpu.VMEM((2, MOE_BLOCK * nc, LANES), F32), pltpu.VMEM((2, MOE_BLOCK * nc, LANES), F32),
                        pltpu.VMEM((2, d, ff), F32), pltpu.VMEM((2, d, ff), F32), pltpu.VMEM((2, ff, d), F32),
                        pltpu.SemaphoreType.DMA((2,)), pltpu.SemaphoreType.DMA((2,)), pltpu.SemaphoreType.DMA((2,))],
    )
    return pl.pallas_call(
        kern,
        out_shape=jax.ShapeDtypeStruct(((n_slots + MOE_BLOCK) * nc, LANES), F32),
        grid_spec=grid_spec,
        compiler_params=_cparams("arbitrary"),
        name="experts",
    )(block_e, nused, first.astype(jnp.int32), wslot, next_e, src3, src3, dst3, h2, w_gate, w_up, w_down)


def _expert_plan(eidx, rank, counts, tp):
    t = eidx.shape[0]
    n_experts = counts.shape[0]
    s = t * TOP_K
    padded = (counts + MOE_BLOCK - 1) // MOE_BLOCK * MOE_BLOCK
    ends = jnp.cumsum(padded)
    pad_start = ends - padded
    dest = (pad_start[eidx] + rank).reshape(-1)
    n_blocks = -(-(s + n_experts * (MOE_BLOCK - 1)) // MOE_BLOCK)
    rows = n_blocks * MOE_BLOCK
    slot_of_row = jnp.full((rows,), -1, jnp.int32).at[dest].set(jnp.arange(s, dtype=jnp.int32))
    valid = slot_of_row >= 0
    tok = slot_of_row // TOP_K
    src_tok = jnp.where(valid, tok, 0)
    kk = slot_of_row % TOP_K
    dst_real = jnp.where(tok < tp, kk * tp + tok, TOP_K * tp + kk * (t - tp) + (tok - tp))
    dst_slot = jnp.where(valid, dst_real, s + jnp.arange(rows, dtype=jnp.int32) % MOE_BLOCK)
    block_start = jnp.arange(n_blocks, dtype=jnp.int32) * MOE_BLOCK
    block_e = jnp.minimum(jnp.sum((ends[None, :] <= block_start[:, None]).astype(jnp.int32), axis=1), n_experts - 1)
    nused = (ends[-1:] // MOE_BLOCK).astype(jnp.int32)
    return block_e, src_tok, dst_slot, nused


def _combine_ple_kernel(x1_ref, ys0_ref, ys1_ref, rw_ref, p_ref, wp_ref, g_ref, wpg_ref, y_ref):
    rw = rw_ref[...]
    tm, d = x1_ref.shape
    nc = d // LANES

    def token_rows(ref):
        return jnp.concatenate([ref[pl.ds(c, tm, stride=nc), :] for c in range(nc)], axis=1)

    moe = rw[:, 0:1] * token_rows(ys0_ref) + rw[:, 1:2] * token_rows(ys1_ref)
    x2 = x1_ref[...] + moe
    hn = x2 * lax.rsqrt(jnp.mean(x2 * x2, axis=-1, keepdims=True) + EPS) * g_ref[...]
    gate = _sigmoid(_dot(hn.astype(BF16), wpg_ref[...]))
    y_ref[...] = x2 + _dot(p_ref[...], wp_ref[...]) * gate


def _combine_ple(x1, y_slots, route_w, p, w_ple, gain, w_ple_gate, *, row0, rows, slot_rows):
    t, d = x1.shape
    tm = _row_tile(math.gcd(math.gcd(rows, row0), math.gcd(*slot_rows)), 256)
    r0, s0, s1 = row0 // tm, slot_rows[0] // tm, slot_rows[1] // tm
    row = lambda w: pl.BlockSpec((tm, w), lambda i: (r0 + i, 0))
    once = lambda shape: pl.BlockSpec(shape, lambda i: (0, 0), pipeline_mode=pl.Buffered(1))
    return pl.pallas_call(
        _combine_ple_kernel,
        out_shape=jax.ShapeDtypeStruct((rows, d), F32),
        grid=(rows // tm,),
        in_specs=[row(d),
                  pl.BlockSpec((tm * (d // LANES), LANES), lambda i: (s0 + i, 0)),
                  pl.BlockSpec((tm * (d // LANES), LANES), lambda i: (s1 + i, 0)),
                  row(LANES), row(p.shape[1]),
                  once(w_ple.shape), once((1, d)), once(w_ple_gate.shape)],
        out_specs=pl.BlockSpec((tm, d), lambda i: (i, 0)),
        compiler_params=_cparams("parallel"),
        name="combine_ple_r%d" % row0,
    )(x1, y_slots, y_slots, route_w, p, w_ple, gain, w_ple_gate)


def _pad_rows(a, rows):
    return jnp.pad(a, ((0, 0), (0, rows - a.shape[1]), (0, 0)))


def _layer(x_all, p_all, nb_p, seq_p, nb_s, seq_s, past_len, state_conv, state_delta, cache_k, cache_v,
           page_table, lam_init, lw):
    tp = nb_p * seq_p
    ts = nb_s * seq_s
    d = x_all.shape[1]
    heads_d = lw['a_log'].shape[0]
    dk = lw['delta_norm'].shape[0]
    hk = heads_d * dk
    conv_dim = 3 * hk
    comp = lw['q_norm'].shape[0]
    dv = 2 * comp
    width = lw['w_ob'].shape[0]
    heads_a = width // dv
    n_groups = lw['w_rg'].shape[1]
    n_experts = lw['w_re'].shape[1]
    assert seq_s <= 8 and conv_dim % 1024 == 0 and hk == 1024 and width == 1024 and d % 1024 == 0
    assert 2 * heads_d <= LANES and n_groups + n_experts <= LANES and lw['conv_w'].shape[0] == 4

    w_in = lw['w_in']
    ba0 = conv_dim + hk
    q0 = ba0 + 2 * heads_d
    gate0 = q0 + 3 * width
    w_parts = (w_in[:, :ba0].astype(BF16), w_in[:, gate0:].astype(BF16), w_in[:, q0:gate0].astype(BF16))
    w_ba = jnp.pad(w_in[:, ba0:q0], ((0, 0), (0, LANES - 2 * heads_d))).astype(BF16)
    z_cb = conv_dim // hk
    gate_cb = (conv_dim + hk) // d
    q_cb = (conv_dim + hk + 2 * d) // width
    assert (conv_dim + hk) % d == 0 and (conv_dim + hk + 2 * d) % width == 0

    proj, ba = _inproj(x_all, lw['norm_attn'].reshape(1, d), w_parts, w_ba)

    alog_row = jnp.pad(lw['a_log'], (heads_d, LANES - 2 * heads_d)).reshape(1, LANES)
    dtb_row = jnp.pad(lw['dt_bias'], (heads_d, LANES - 2 * heads_d)).reshape(1, LANES)
    dn = lw['delta_norm'].reshape(1, dk)
    chunk_p = math.gcd(seq_p, DELTA_CHUNK)
    oa_p, s_p = _delta_branch(proj, ba, 0, nb_p, seq_p,
                              jnp.zeros((nb_p, 8, conv_dim), F32), jnp.zeros((nb_p, heads_d, dk, dk), F32),
                              lw['conv_w'], alog_row, dtb_row, dn,
                              chunk=chunk_p, valid=chunk_p, heads=heads_d, dk=dk, z_col_block=z_cb)
    qkv_s = proj[tp:, :conv_dim].reshape(nb_s, seq_s, conv_dim)
    proj_s8 = _pad_rows(proj[tp:].reshape(nb_s, seq_s, -1), 8).reshape(nb_s * 8, -1)
    ba_s8 = _pad_rows(ba[tp:].reshape(nb_s, seq_s, LANES), 8).reshape(nb_s * 8, LANES)
    tail_s = jnp.pad(state_conv, ((0, 0), (8 - state_conv.shape[1], 0), (0, 0)))
    oa_s8, s_s = _delta_branch(proj_s8, ba_s8, 0, nb_s, 8, tail_s, state_delta,
                               lw['conv_w'], alog_row, dtb_row, dn,
                               chunk=8, valid=seq_s, heads=heads_d, dk=dk, z_col_block=z_cb)
    oa_s = oa_s8.reshape(nb_s, 8, hk)[:, :seq_s].reshape(ts, hk)
    o_a = jnp.concatenate([oa_p, oa_s], axis=0)
    conv_p = jnp.stack([proj[(b + 1) * seq_p - 3:(b + 1) * seq_p, :conv_dim] for b in range(nb_p)])
    conv_s = jnp.concatenate([state_conv, qkv_s], axis=1)[:, -3:, :]

    half = comp // 2
    inv_freq = ROPE_THETA ** (-jnp.arange(half, dtype=F32) / half)
    def rope_tables(pos):
        ang = pos.astype(F32)[:, None] * inv_freq[None, :]
        sin_h = jnp.sin(ang)
        return (jnp.tile(jnp.cos(ang), (1, LANES // half)),
                jnp.tile(jnp.concatenate([-sin_h, sin_h], axis=1), (1, LANES // comp)))

    q_gain = jnp.tile(lw['q_norm'], LANES // comp).reshape(1, LANES)
    k_gain = jnp.tile(lw['k_norm'], LANES // comp).reshape(1, LANES)
    tk = _row_tile(seq_p, 512, align=LANES)
    q_t, k_bf, v_t, k_t = _qkprep(proj, 0, nb_p, seq_p, *rope_tables(jnp.arange(seq_p, dtype=jnp.int32)),
                                  q_gain, k_gain, q_col_block=q_cb, width=width, comp=comp, for_prompt=True, tk=tk)
    pos_s = jnp.tile(past_len + jnp.arange(seq_s, dtype=jnp.int32), nb_s)
    q_fs, k_fs = _qkprep(proj, tp, 1, ts, *rope_tables(pos_s), q_gain, k_gain,
                         q_col_block=q_cb, width=width, comp=comp, for_prompt=False)
    lam_vecs = jnp.stack([lw['lam_q1'], lw['lam_k1'], lw['lam_q2'], lw['lam_k2']])
    subln = lw['subln'].reshape(1, dv)
    ob_p = _attn_prompt(q_t, k_bf, v_t, lam_vecs, lw['subln'].reshape(dv, 1), nb=nb_p, seq=seq_p, tk=tk,
                        heads=heads_a, comp=comp, lam_init=lam_init)
    v_f_s = proj[tp:, (q_cb + 2) * width:(q_cb + 3) * width]
    q8 = _pad_rows(q_fs.reshape(nb_s, seq_s, width), 8)
    k8 = _pad_rows(k_fs.reshape(nb_s, seq_s, width), 8)
    v8 = _pad_rows(v_f_s.reshape(nb_s, seq_s, width), 8)
    n_phys, page = cache_k.shape[0], cache_k.shape[1]
    cache_kt = jnp.transpose(cache_k, (0, 2, 3, 4, 1)).reshape(n_phys, width, page)
    cache_v2 = cache_v.reshape(n_phys, page * heads_a, dv)
    ob_s8 = _attn_sample(q8, k8, v8, cache_kt, cache_v2, page_table, lam_vecs, subln,
                         new_len=seq_s, heads=heads_a, comp=comp, lam_init=lam_init)
    o_b = jnp.concatenate([ob_p, ob_s8[:, :seq_s].reshape(ts, width)], axis=0)

    merged = _merge(o_a, o_b, proj, lw['w_oa'].astype(BF16), lw['w_ob'].astype(BF16), gate_col_block=gate_cb)
    w_route = jnp.pad(jnp.concatenate([lw['w_rg'], lw['w_re']], axis=1),
                      ((0, 0), (0, LANES - n_groups - n_experts))).astype(BF16)
    b_route = jnp.pad(jnp.concatenate([lw['b_rg'], lw['b_re']]), (0, LANES - n_groups - n_experts)).reshape(1, LANES)
    x1, h2, route_w, route_i, route_cnt = _outproj_route(x_all, merged, lw['w_out'].astype(BF16),
                                                         lw['norm_ffn'].reshape(1, d), w_route, b_route,
                                                         n_groups=n_groups, per_group=n_experts // n_groups)

    counts = route_cnt[0, n_groups:n_groups + n_experts].astype(jnp.int32)
    block_e, src_tok, dst_slot, nused = _expert_plan(route_i[:, :TOP_K], route_i[:, TOP_K:2 * TOP_K], counts, tp)
    y_slots = _experts(h2, block_e, src_tok, dst_slot, nused, lw['exp_gate'], lw['exp_up'], lw['exp_down'])

    ple_args = (x1, y_slots, route_w, p_all.astype(BF16), lw['w_ple'].astype(BF16),
                lw['norm_ple'].reshape(1, d), lw['w_ple_gate'].astype(BF16))
    y_p = _combine_ple(*ple_args, row0=0, rows=tp, slot_rows=(0, tp))
    y_s = _combine_ple(*ple_args, row0=tp, rows=ts, slot_rows=(TOP_K * tp, TOP_K * tp + ts))
    y = (y_p, y_s)

    k_p = jnp.transpose(k_t.reshape(nb_p, heads_a, 2, comp, seq_p), (0, 4, 1, 2, 3))
    v_p = proj[:tp, (q_cb + 2) * width:(q_cb + 3) * width].reshape(nb_p, seq_p, heads_a, dv)
    k_s = k_fs.reshape(nb_s, seq_s, heads_a, 2, comp)
    v_s = v_f_s.reshape(nb_s, seq_s, heads_a, dv)
    return y, k_p, v_p, k_s, v_s, conv_p, conv_s, s_p, s_s


def kernel(x_prompt, x_sample, cache_k, cache_v, state_conv, state_delta, page_table, p_prompt, p_sample,
           norm_attn, w_in, conv_w, a_log, dt_bias, delta_norm, q_norm, k_norm, lam_q1, lam_k1, lam_q2, lam_k2,
           subln, w_oa, w_ob, w_out, norm_ffn, w_rg, b_rg, w_re, b_re, exp_gate, exp_up, exp_down,
           norm_ple, w_ple, w_ple_gate):
    nb_p, seq_p, d = x_prompt.shape
    nb_s, seq_s, _ = x_sample.shape
    tp = nb_p * seq_p
    depth = w_in.shape[0]
    past_len = page_table.shape[1] * cache_k.shape[2]
    y_p, y_s = x_prompt.reshape(tp, d), x_sample.reshape(nb_s * seq_s, d)
    outs = [[] for _ in range(8)]
    for i in range(depth):
        x_all = jnp.concatenate([y_p, y_s], axis=0)
        lw = dict(norm_attn=norm_attn[i], w_in=w_in[i], conv_w=conv_w[i], a_log=a_log[i], dt_bias=dt_bias[i],
                  delta_norm=delta_norm[i], q_norm=q_norm[i], k_norm=k_norm[i], lam_q1=lam_q1[i], lam_k1=lam_k1[i],
                  lam_q2=lam_q2[i], lam_k2=lam_k2[i], subln=subln[i], w_oa=w_oa[i], w_ob=w_ob[i], w_out=w_out[i],
                  norm_ffn=norm_ffn[i], w_rg=w_rg[i], b_rg=b_rg[i], w_re=w_re[i], b_re=b_re[i],
                  exp_gate=exp_gate[i], exp_up=exp_up[i], exp_down=exp_down[i],
                  norm_ple=norm_ple[i], w_ple=w_ple[i], w_ple_gate=w_ple_gate[i])
        lam_init = 0.8 - 0.6 * math.exp(-0.3 * i)
        p_all = jnp.concatenate([p_prompt[i].reshape(tp, -1), p_sample[i].reshape(nb_s * seq_s, -1)], axis=0)
        res = _layer(x_all, p_all, nb_p, seq_p, nb_s, seq_s, past_len, state_conv[i], state_delta[i],
                     cache_k[i], cache_v[i], page_table, lam_init, lw)
        y_p, y_s = res[0]
        for lst, val in zip(outs, res[1:]):
            lst.append(val)
    return (y_p.reshape(nb_p, seq_p, d), y_s.reshape(nb_s, seq_s, d)) + tuple(jnp.stack(lst) for lst in outs)
```
